```python
import math
import jax, jax.numpy as jnp
from jax import lax
import numpy as np

D_MODEL = 1024
BATCH = 2
SEQ = 16384
DEPTH = 1
DEC_BATCH = 32
DEC_SEQ = 16
PAST_LEN = 4096

CHUNK = 64
Q_BLOCK = 128
HEAD_DIM = 64
V_DIM = 2 * HEAD_DIM
ATT_OUT = D_MODEL // 2
N_HEADS = ATT_OUT // V_DIM
IN_QK = N_HEADS * 2 * HEAD_DIM
SSM_WIDTH = D_MODEL - ATT_OUT
SSM_GROUP = 16
N_SSM_GROUPS = SSM_WIDTH // SSM_GROUP
SSM_STATE = 64
IN_COLS = 2 * IN_QK + ATT_OUT + SSM_WIDTH
N_EXPERT_GROUPS = 4
EXPERTS_PER_GROUP = 4
N_EXPERTS = N_EXPERT_GROUPS * EXPERTS_PER_GROUP
TOP_K = 2
D_EXPERT = D_MODEL // 4
ROPE_THETA = 10000.0
ALPHA = (2.0 * DEPTH) ** 0.25
BETA = (8.0 * DEPTH) ** -0.25
LN_EPS = 1e-5
RMS_EPS = 1e-6

kernel_name = 'hymba_diffattn_s5_hmoe_stream_step'

F32 = jnp.float32


def layer_norm(x, g, b):
    xf = x.astype(F32)
    mu = jnp.mean(xf, -1, keepdims=True)
    xc = xf - mu
    var = jnp.mean(xc * xc, -1, keepdims=True)
    return (xc * lax.rsqrt(var + LN_EPS) * g.astype(F32) + b.astype(F32)).astype(x.dtype)


def rms_norm(x, g):
    xf = x.astype(F32)
    return xf * lax.rsqrt(jnp.mean(xf * xf, -1, keepdims=True) + RMS_EPS) * g.astype(F32)


def rope(x, pos):
    half = HEAD_DIM // 2
    inv = 1.0 / (ROPE_THETA ** (jnp.arange(half, dtype=F32) / half))
    ang = pos.astype(F32)[:, None] * inv[None, :]
    cos = jnp.cos(ang)[:, None, None, :]
    sin = jnp.sin(ang)[:, None, None, :]
    xf = x.astype(F32)
    x1, x2 = xf[..., :half], xf[..., half:]
    return jnp.concatenate([x1 * cos - x2 * sin, x1 * sin + x2 * cos], -1).astype(x.dtype)


def diff_attend_block(q, k, v, q_pos, k_pos, lam):
    s = jnp.einsum('bqhmd,bkhmd->bhmqk', q, k, preferred_element_type=F32) * (HEAD_DIM ** -0.5)
    mask = (k_pos[None, :] // CHUNK) <= (q_pos[:, None] // CHUNK)
    s = jnp.where(mask, s, jnp.finfo(F32).min)
    p = jax.nn.softmax(s, axis=-1)
    w = p[:, :, 0] - lam * p[:, :, 1]
    return jnp.einsum('bhqk,bkhe->bqhe', w.astype(v.dtype), v, preferred_element_type=F32)


def diff_attention(q, k, v, q_pos, k_pos, lam):
    B, Lq = q.shape[0], q.shape[1]
    if Lq <= Q_BLOCK:
        return diff_attend_block(q, k, v, q_pos, k_pos, lam)
    nb = Lq // Q_BLOCK
    qb = jnp.moveaxis(q.reshape(B, nb, Q_BLOCK, N_HEADS, 2, HEAD_DIM), 1, 0)
    pb = q_pos.reshape(nb, Q_BLOCK)
    out = lax.map(lambda a: diff_attend_block(a[0], k, v, a[1], k_pos, lam), (qb, pb))
    return jnp.moveaxis(out, 0, 1).reshape(B, Lq, N_HEADS, V_DIM)


def complex_affine_combine(e1, e2):
    a1r, a1i, b1r, b1i = e1
    a2r, a2i, b2r, b2i = e2
    return (a2r * a1r - a2i * a1i, a2r * a1i + a2i * a1r,
            a2r * b1r - a2i * b1i + b2r, a2r * b1i + a2i * b1r + b2i)


def s5_mixer(u, h0_re, h0_im, p):
    B, L = u.shape[0], u.shape[1]
    a_re = p['ssm_a_re'].astype(F32)
    a_im = p['ssm_a_im'].astype(F32)
    dt = jnp.exp(p['ssm_log_dt'].astype(F32))[:, None]
    mag = jnp.exp(a_re * dt)
    ab_re = mag * jnp.cos(a_im * dt)
    ab_im = mag * jnp.sin(a_im * dt)
    nr, ni = ab_re - 1.0, ab_im
    den = a_re * a_re + a_im * a_im
    cf_re = (nr * a_re + ni * a_im) / den
    cf_im = (ni * a_re - nr * a_im) / den
    bu_re = jnp.einsum('blgc,gnc->blgn', u, p['ssm_b_re'].astype(F32))
    bu_im = jnp.einsum('blgc,gnc->blgn', u, p['ssm_b_im'].astype(F32))
    x_re = cf_re * bu_re - cf_im * bu_im
    x_im = cf_re * bu_im + cf_im * bu_re
    x_re = x_re.at[:, 0].add(ab_re * h0_re - ab_im * h0_im)
    x_im = x_im.at[:, 0].add(ab_re * h0_im + ab_im * h0_re)
    abr = jnp.broadcast_to(ab_re, x_re.shape)
    abi = jnp.broadcast_to(ab_im, x_im.shape)
    _, _, h_re, h_im = lax.associative_scan(complex_affine_combine, (abr, abi, x_re, x_im), axis=1)
    y = (jnp.einsum('blgn,gcn->blgc', h_re, p['ssm_c_re'].astype(F32))
         - jnp.einsum('blgn,gcn->blgc', h_im, p['ssm_c_im'].astype(F32))
         + p['ssm_d'].astype(F32).reshape(N_SSM_GROUPS, SSM_GROUP) * u)
    return y.reshape(B, L, SSM_WIDTH), h_re[:, -1], h_im[:, -1]


def hier_moe(x, p):
    T = x.shape[0]
    xf = x.astype(F32)
    g_prob = jax.nn.softmax(xf @ p['w_grp'].astype(F32) + p['b_grp'].astype(F32), axis=-1)
    g_w, g_idx = lax.top_k(g_prob, 1)
    e_logits = (xf @ p['w_rt'].astype(F32) + p['b_rt'].astype(F32)).reshape(T, N_EXPERT_GROUPS, EXPERTS_PER_GROUP)
    sel = jnp.take_along_axis(e_logits, g_idx[:, :, None], axis=1)[:, 0]
    e_w, e_idx = lax.top_k(jax.nn.softmax(sel, axis=-1), TOP_K)
    e_w = e_w / jnp.sum(e_w, -1, keepdims=True)
    weights = g_w * e_w
    expert_id = g_idx * EXPERTS_PER_GROUP + e_idx
    gates = jnp.sum(jax.nn.one_hot(expert_id, N_EXPERTS, dtype=F32) * weights[..., None], axis=1)
    gates = gates.astype(x.dtype)
    out = jnp.zeros((T, D_MODEL), F32)
    for e in range(N_EXPERTS):
        h = jax.nn.silu(x @ p['w_gate'][e]) * (x @ p['w_up'][e])
        out = out + gates[:, e:e + 1] * (h @ p['w_down'][e])
    return out.astype(x.dtype)


def hybrid_layer(x, pos, p, lam_init, k_past, v_past, h0_re, h0_im):
    B, L, _ = x.shape
    proj = jnp.einsum('bld,dc->blc', x, p['w_in'])
    q, k, v, u = jnp.split(proj, [IN_QK, 2 * IN_QK, 2 * IN_QK + ATT_OUT], axis=-1)
    q = rope(q.reshape(B, L, N_HEADS, 2, HEAD_DIM), pos)
    k = rope(k.reshape(B, L, N_HEADS, 2, HEAD_DIM), pos)
    v = v.reshape(B, L, N_HEADS, V_DIM)
    if k_past is None:
        k_all, v_all = k, v
    else:
        k_all = jnp.concatenate([k_past.astype(k.dtype), k], axis=1)
        v_all = jnp.concatenate([v_past.astype(v.dtype), v], axis=1)
    k_pos = jnp.arange(k_all.shape[1])
    lq1, lk1 = p['lam_q1'].astype(F32), p['lam_k1'].astype(F32)
    lq2, lk2 = p['lam_q2'].astype(F32), p['lam_k2'].astype(F32)
    lam = jnp.exp(jnp.sum(lq1 * lk1)) - jnp.exp(jnp.sum(lq2 * lk2)) + lam_init
    o = diff_attention(q, k_all, v_all, pos, k_pos, lam)
    o = rms_norm(o, p['subln_g']) * (1.0 - lam_init)
    if h0_re is None:
        h0_re = jnp.zeros((B, N_SSM_GROUPS, SSM_STATE), F32)
        h0_im = jnp.zeros((B, N_SSM_GROUPS, SSM_STATE), F32)
    y_ssm, h_re, h_im = s5_mixer(u.astype(F32).reshape(B, L, N_SSM_GROUPS, SSM_GROUP),
                                 h0_re.astype(F32), h0_im.astype(F32), p)
    y_ssm = jax.nn.gelu(y_ssm)
    ga, gb = jnp.split(jnp.einsum('blc,ce->ble', y_ssm, p['w_glu'].astype(F32)), 2, axis=-1)
    y_ssm = rms_norm(ga * jax.nn.sigmoid(gb), p['ssm_norm_g'])
    heads = jnp.concatenate([o.reshape(B, L, ATT_OUT), y_ssm], axis=-1).astype(x.dtype)
    mix = jnp.einsum('blc,cd->bld', heads, p['w_out'])
    x1 = layer_norm(ALPHA * x + mix, p['ln1_g'], p['ln1_b'])
    ffn = hier_moe(x1.reshape(B * L, D_MODEL), p).reshape(B, L, D_MODEL)
    x2 = layer_norm(ALPHA * x1 + ffn, p['ln2_g'], p['ln2_b'])
    return x2, k, v, h_re, h_im


def setup_inputs(seed: int = 0) -> dict:
    key = jax.random.key(seed)
    ks = iter(jax.random.split(key, 40))

    def nrm(shape, scale):
        return jax.random.normal(next(ks), shape, F32) * scale

    col_scale = jnp.concatenate([jnp.ones((2 * IN_QK,), F32), jnp.full((ATT_OUT,), BETA, F32),
                                 jnp.ones((SSM_WIDTH,), F32)])
    a_im_base = jnp.pi * jnp.arange(SSM_STATE, dtype=F32)
    return {
        'x_prompt': nrm((BATCH, SEQ, D_MODEL), 1.0),
        'x_sample': nrm((DEC_BATCH, DEC_SEQ, D_MODEL), 1.0),
        'cache_k': nrm((DEPTH, DEC_BATCH, PAST_LEN, N_HEADS, 2, HEAD_DIM), 1.0),
        'cache_v': nrm((DEPTH, DEC_BATCH, PAST_LEN, N_HEADS, V_DIM), BETA),
        'state_ssm_re': nrm((DEPTH, DEC_BATCH, N_SSM_GROUPS, SSM_STATE), 0.5),
        'state_ssm_im': nrm((DEPTH, DEC_BATCH, N_SSM_GROUPS, SSM_STATE), 0.5),
        'w_in': nrm((DEPTH, D_MODEL, IN_COLS), D_MODEL ** -0.5) * col_scale,
        'lam_q1': nrm((DEPTH, HEAD_DIM), 0.1),
        'lam_k1': nrm((DEPTH, HEAD_DIM), 0.1),
        'lam_q2': nrm((DEPTH, HEAD_DIM), 0.1),
        'lam_k2': nrm((DEPTH, HEAD_DIM), 0.1),
        'subln_g': 1.0 + nrm((DEPTH, V_DIM), 0.02),
        'ssm_a_re': -0.5 + nrm((DEPTH, N_SSM_GROUPS, SSM_STATE), 0.01),
        'ssm_a_im': a_im_base + nrm((DEPTH, N_SSM_GROUPS, SSM_STATE), 0.01),
        'ssm_log_dt': jax.random.uniform(next(ks), (DEPTH, N_SSM_GROUPS), F32,
                                         minval=math.log(1e-3), maxval=math.log(1e-1)),
        'ssm_b_re': nrm((DEPTH, N_SSM_GROUPS, SSM_STATE, SSM_GROUP), (2.0 * SSM_GROUP) ** -0.5),
        'ssm_b_im': nrm((DEPTH, N_SSM_GROUPS, SSM_STATE, SSM_GROUP), (2.0 * SSM_GROUP) ** -0.5),
        'ssm_c_re': nrm((DEPTH, N_SSM_GROUPS, SSM_GROUP, SSM_STATE), (2.0 * SSM_STATE) ** -0.5),
        'ssm_c_im': nrm((DEPTH, N_SSM_GROUPS, SSM_GROUP, SSM_STATE), (2.0 * SSM_STATE) ** -0.5),
        'ssm_d': nrm((DEPTH, SSM_WIDTH), 1.0),
        'w_glu': nrm((DEPTH, SSM_WIDTH, 2 * SSM_WIDTH), SSM_WIDTH ** -0.5),
        'ssm_norm_g': 1.0 + nrm((DEPTH, SSM_WIDTH), 0.02),
        'w_out': nrm((DEPTH, D_MODEL, D_MODEL), BETA * D_MODEL ** -0.5),
        'ln1_g': 1.0 + nrm((DEPTH, D_MODEL), 0.02),
        'ln1_b': nrm((DEPTH, D_MODEL), 0.02),
        'w_grp': nrm((DEPTH, D_MODEL, N_EXPERT_GROUPS), D_MODEL ** -0.5),
        'b_grp': nrm((DEPTH, N_EXPERT_GROUPS), 0.01),
        'w_rt': nrm((DEPTH, D_MODEL, N_EXPERTS), D_MODEL ** -0.5),
        'b_rt': nrm((DEPTH, N_EXPERTS), 0.01),
        'w_gate': nrm((DEPTH, N_EXPERTS, D_MODEL, D_EXPERT), D_MODEL ** -0.5),
        'w_up': nrm((DEPTH, N_EXPERTS, D_MODEL, D_EXPERT), BETA * D_MODEL ** -0.5),
        'w_down': nrm((DEPTH, N_EXPERTS, D_EXPERT, D_MODEL), BETA * D_EXPERT ** -0.5),
        'ln2_g': 1.0 + nrm((DEPTH, D_MODEL), 0.02),
        'ln2_b': nrm((DEPTH, D_MODEL), 0.02),
    }


def reference(x_prompt, x_sample, cache_k, cache_v, state_ssm_re, state_ssm_im,
              w_in, lam_q1, lam_k1, lam_q2, lam_k2, subln_g,
              ssm_a_re, ssm_a_im, ssm_log_dt, ssm_b_re, ssm_b_im, ssm_c_re, ssm_c_im, ssm_d,
              w_glu, ssm_norm_g, w_out, ln1_g, ln1_b,
              w_grp, b_grp, w_rt, b_rt, w_gate, w_up, w_down, ln2_g, ln2_b):
    past_len = cache_k.shape[2]
    pos_p = jnp.arange(x_prompt.shape[1])
    pos_s = past_len + jnp.arange(x_sample.shape[1])
    yp, ys = x_prompt, x_sample
    kp_l, vp_l, hrp_l, hip_l = [], [], [], []
    ks_l, vs_l, hrs_l, his_l = [], [], [], []
    for l in range(DEPTH):
        p = {
            'w_in': w_in[l], 'lam_q1': lam_q1[l], 'lam_k1': lam_k1[l], 'lam_q2': lam_q2[l],
            'lam_k2': lam_k2[l], 'subln_g': subln_g[l], 'ssm_a_re': ssm_a_re[l], 'ssm_a_im': ssm_a_im[l],
            'ssm_log_dt': ssm_log_dt[l], 'ssm_b_re': ssm_b_re[l], 'ssm_b_im': ssm_b_im[l],
            'ssm_c_re': ssm_c_re[l], 'ssm_c_im': ssm_c_im[l], 'ssm_d': ssm_d[l], 'w_glu': w_glu[l],
            'ssm_norm_g': ssm_norm_g[l], 'w_out': w_out[l], 'ln1_g': ln1_g[l], 'ln1_b': ln1_b[l],
            'w_grp': w_grp[l], 'b_grp': b_grp[l], 'w_rt': w_rt[l], 'b_rt': b_rt[l],
            'w_gate': w_gate[l], 'w_up': w_up[l], 'w_down': w_down[l], 'ln2_g': ln2_g[l], 'ln2_b': ln2_b[l],
        }
        lam_init = 0.8 - 0.6 * math.exp(-0.3 * l)
        yp, kp, vp, hrp, hip = hybrid_layer(yp, pos_p, p, lam_init, None, None, None, None)
        ys, kn, vn, hrs, his = hybrid_layer(ys, pos_s, p, lam_init, cache_k[l], cache_v[l],
                                            state_ssm_re[l], state_ssm_im[l])
        kp_l.append(kp); vp_l.append(vp); hrp_l.append(hrp); hip_l.append(hip)
        ks_l.append(kn); vs_l.append(vn); hrs_l.append(hrs); his_l.append(his)
    new_k_prompt = jnp.stack(kp_l)
    new_v_prompt = jnp.stack(vp_l)
    new_ssm_re_prompt = jnp.stack(hrp_l)
    new_ssm_im_prompt = jnp.stack(hip_l)
    new_k_sample = jnp.stack(ks_l)
    new_v_sample = jnp.stack(vs_l)
    new_ssm_re_sample = jnp.stack(hrs_l)
    new_ssm_im_sample = jnp.stack(his_l)
    return (yp, ys, new_k_prompt, new_v_prompt, new_ssm_re_prompt, new_ssm_im_prompt,
            new_k_sample, new_v_sample, new_ssm_re_sample, new_ssm_im_sample)
```

```python
import functools
import math

import jax
import jax.numpy as jnp
from jax import lax
from jax.experimental import pallas as pl
from jax.experimental.pallas import tpu as pltpu

F32 = jnp.float32
BF16 = jnp.bfloat16

HEAD_DIM = 64
V_DIM = 2 * HEAD_DIM
CHUNK = 64
SSM_GROUP = 16
SSM_STATE = 64
N_EXPERT_GROUPS = 4
EXPERTS_PER_GROUP = 4
N_EXPERTS = N_EXPERT_GROUPS * EXPERTS_PER_GROUP
ROPE_THETA = 10000.0
LN_EPS = 1e-5
RMS_EPS = 1e-6
LANES = 128
SUBLANES = 8
NEG_BIG = -1e30
LOG2E = 1.4426950408889634
VMEM_LIMIT = 52 * 1024 * 1024


def _cparams(*sem):
    return pltpu.CompilerParams(dimension_semantics=sem, vmem_limit_bytes=VMEM_LIMIT)


def _inproj_kernel(x_ref, w_ref, cos_ref, sin_ref, q_ref, kf_ref, kb_ref, vf_ref, vb_ref, u_ref,
                   *, q_scale, n_qk):
    xb = x_ref[...].astype(BF16)
    cos = cos_ref[...]
    sin = sin_ref[...]
    lane = lax.broadcasted_iota(jnp.int32, cos.shape, 1)
    first = (lane & (HEAD_DIM - 1)) < HEAD_DIM // 2

    def rope(t):
        rot = jnp.where(first, pltpu.roll(t, LANES - HEAD_DIM // 2, 1), pltpu.roll(t, HEAD_DIM // 2, 1))
        return t * cos + rot * sin

    pq = jnp.dot(xb, w_ref[:, 0:n_qk], preferred_element_type=F32)
    for j in range(n_qk // LANES):
        sl = slice(LANES * j, LANES * (j + 1))
        q_ref[:, sl] = (rope(pq[:, sl]) * q_scale).astype(BF16)
    pk = jnp.dot(xb, w_ref[:, n_qk:2 * n_qk], preferred_element_type=F32)
    for j in range(n_qk // LANES):
        sl = slice(LANES * j, LANES * (j + 1))
        kr = rope(pk[:, sl])
        kf_ref[:, sl] = kr
        kb_ref[:, sl] = kr.astype(BF16)
    n_v = vf_ref.shape[1]
    pv = jnp.dot(xb, w_ref[:, 2 * n_qk:2 * n_qk + n_v], preferred_element_type=F32)
    vf_ref[...] = pv
    vb_ref[...] = pv.astype(BF16)
    u_ref[...] = jnp.dot(xb, w_ref[:, 2 * n_qk + n_v:], preferred_element_type=F32)


def _in_proj(x2d, w_bf, cos_t, sin_t, *, tm):
    T, D = x2d.shape
    n_cols = w_bf.shape[1]
    n_qk = n_cols // 4
    n_tab = cos_t.shape[0] // tm
    row = lambda i: (i, 0)
    tab = lambda i: (i % n_tab, 0)
    out_sds = lambda dt: jax.ShapeDtypeStruct((T, n_qk), dt)
    spec = pl.BlockSpec((tm, n_qk), row)
    return pl.pallas_call(
        functools.partial(_inproj_kernel, q_scale=HEAD_DIM ** -0.5 * LOG2E, n_qk=n_qk),
        grid=(T // tm,),
        in_specs=[pl.BlockSpec((tm, D), row),
                  pl.BlockSpec((D, n_cols), lambda i: (0, 0)),
                  pl.BlockSpec((tm, LANES), tab),
                  pl.BlockSpec((tm, LANES), tab)],
        out_specs=[spec] * 6,
        out_shape=[out_sds(BF16), out_sds(F32), out_sds(BF16), out_sds(F32), out_sds(BF16), out_sds(F32)],
        compiler_params=_cparams("parallel"),
        name="in_proj_rope",
    )(x2d, w_bf, cos_t, sin_t)


def _split_maps(q):
    lane = lax.broadcasted_iota(jnp.int32, q.shape, 1)
    zero = jnp.zeros_like(q)
    return jnp.concatenate([jnp.where(lane < HEAD_DIM, q, zero), jnp.where(lane >= HEAD_DIM, q, zero)], axis=0)


def _qk(qs, kb):
    return lax.dot_general(qs, kb, (((1,), (1,)), ((), ())), preferred_element_type=F32)


def _subln(o, g, out_scale):
    ms = jnp.mean(o * o, axis=-1, keepdims=True)
    return o * lax.rsqrt(ms + RMS_EPS) * g * out_scale


def _attn_prompt_kernel(lam_ref, q_ref, k_ref, v_ref, g_ref, o_ref, qs_ref, m_ref, l_ref, acc_ref,
                        *, tq, out_scale):
    i = pl.program_id(2)
    qs_ref[...] = _split_maps(q_ref[...])
    m_ref[...] = jnp.full(m_ref.shape, NEG_BIG, F32)
    l_ref[...] = jnp.zeros(l_ref.shape, F32)
    acc_ref[...] = jnp.zeros(acc_ref.shape, F32)

    def step(kt, masked):
        start = pl.multiple_of(kt * tq, tq)
        kb = k_ref[pl.ds(start, tq), :]
        vb = v_ref[pl.ds(start, tq), :]
        s = _qk(qs_ref[...], kb)
        if masked:
            r = lax.broadcasted_iota(jnp.int32, s.shape, 0)
            c = lax.broadcasted_iota(jnp.int32, s.shape, 1)
            r = jnp.where(r >= tq, r - tq, r)
            s = jnp.where((c // CHUNK) <= (r // CHUNK), s, NEG_BIG)
        m_prev = m_ref[...]
        m_new = jnp.maximum(m_prev, jnp.max(s, axis=-1, keepdims=True))
        alpha = jnp.exp2(m_prev - m_new)
        p = jnp.exp2(s - m_new)
        l_ref[...] = alpha * l_ref[...] + jnp.sum(p, axis=-1, keepdims=True)
        acc_ref[...] = alpha * acc_ref[...] + jnp.dot(p.astype(BF16), vb, preferred_element_type=F32)
        m_ref[...] = m_new

    def body(kt, carry):
        step(kt, False)
        return carry

    lax.fori_loop(0, i, body, 0)
    step(i, True)

    lam = lam_ref[0]
    inv_l = 1.0 / l_ref[...]
    on = acc_ref[...] * inv_l
    o = on[0:tq] - lam * on[tq:2 * tq]
    o_ref[...] = _subln(o, g_ref[...], out_scale).astype(o_ref.dtype)


def _attn_prompt(lam, q, k, v, g, *, tq, out_scale):
    B, L, W = q.shape
    H = W // V_DIM
    blk_q = pl.BlockSpec((None, tq, V_DIM), lambda b, h, i: (b, i, h))
    blk_kv = pl.BlockSpec((None, L, V_DIM), lambda b, h, i: (b, 0, h))
    return pl.pallas_call(
        functools.partial(_attn_prompt_kernel, tq=tq, out_scale=out_scale),
        grid=(B, H, L // tq),
        in_specs=[pl.BlockSpec(memory_space=pltpu.SMEM), blk_q, blk_kv, blk_kv,
                  pl.BlockSpec((1, V_DIM), lambda b, h, i: (0, 0))],
        out_specs=blk_q,
        out_shape=jax.ShapeDtypeStruct((B, L, W), BF16),
        scratch_shapes=[pltpu.VMEM((2 * tq, V_DIM), BF16), pltpu.VMEM((2 * tq, 1), F32),
                        pltpu.VMEM((2 * tq, 1), F32), pltpu.VMEM((2 * tq, V_DIM), F32)],
        compiler_params=_cparams("parallel", "parallel", "parallel"),
        name="diff_attn_prompt",
    )(lam, q, k, v, g)


def _attn_sample_kernel(lam_ref, q_ref, kc_ref, vc_ref, kn_ref, vn_ref, g_ref, o_ref, *, past_len, out_scale):
    tq = q_ref.shape[0]
    qs = _split_maps(q_ref[...])
    s_c = _qk(qs, kc_ref[...].astype(BF16))
    s_n = _qk(qs, kn_ref[...])
    r = lax.broadcasted_iota(jnp.int32, s_n.shape, 0)
    c = lax.broadcasted_iota(jnp.int32, s_n.shape, 1)
    r = jnp.where(r >= tq, r - tq, r)
    s_n = jnp.where(((past_len + c) // CHUNK) <= ((past_len + r) // CHUNK), s_n, NEG_BIG)
    m = jnp.maximum(jnp.max(s_c, axis=-1, keepdims=True), jnp.max(s_n, axis=-1, keepdims=True))
    p_c = jnp.exp2(s_c - m)
    p_n = jnp.exp2(s_n - m)
    inv_l = 1.0 / (jnp.sum(p_c, axis=-1, keepdims=True) + jnp.sum(p_n, axis=-1, keepdims=True))
    lam = lam_ref[0]
    p_c = p_c * inv_l
    p_n = p_n * inv_l
    w_c = p_c[0:tq] - lam * p_c[tq:2 * tq]
    w_n = p_n[0:tq] - lam * p_n[tq:2 * tq]
    o = (jnp.dot(w_c.astype(BF16), vc_ref[...].astype(BF16), preferred_element_type=F32)
         + jnp.dot(w_n.astype(BF16), vn_ref[...], preferred_element_type=F32))
    o_ref[...] = _subln(o, g_ref[...], out_scale).astype(o_ref.dtype)


def _attn_sample(lam, q, cache_k, cache_v, k_new, v_new, g, *, out_scale):
    B, S, W = q.shape
    P = cache_k.shape[1]
    H = W // V_DIM
    blk_new = pl.BlockSpec((None, S, V_DIM), lambda b, h: (b, 0, h))
    blk_cache = pl.BlockSpec((None, P, V_DIM), lambda b, h: (b, 0, h))
    return pl.pallas_call(
        functools.partial(_attn_sample_kernel, past_len=P, out_scale=out_scale),
        grid=(B, H),
        in_specs=[pl.BlockSpec(memory_space=pltpu.SMEM), blk_new, blk_cache, blk_cache, blk_new, blk_new,
                  pl.BlockSpec((1, V_DIM), lambda b, h: (0, 0))],
        out_specs=blk_new,
        out_shape=jax.ShapeDtypeStruct((B, S, W), BF16),
        compiler_params=_cparams("parallel", "parallel"),
        name="diff_attn_sample",
    )(lam, q, cache_k, cache_v, k_new, v_new, g)


def _cmul(ar, ai, br, bi):
    return ar * br - ai * bi, ar * bi + ai * br


def _s5_kernel(*refs, n_steps, chained, lane_chunk):
    if chained:
        (u_ref, perm_ref, permt_ref, wx_ref, cb_ref, d_ref, pre_ref, pim_ref, gre_ref, gim_ref,
         wglu_ref, ng_ref, y_ref, hre_ref, him_ref, xre, xim, cre, cim) = refs
    else:
        (u_ref, perm_ref, permt_ref, wx_ref, cb_ref, d_ref, pre_ref, pim_ref, gre_ref, gim_ref,
         wglu_ref, ng_ref, h0re_ref, h0im_ref, y_ref, hre_ref, him_ref, xre, xim) = refs
    R = n_steps
    n_state = xre.shape[1]
    n_blk = wx_ref.shape[0]
    blk_in = wx_ref.shape[1]
    blk_st = wx_ref.shape[2] // 2

    u = u_ref[...]
    u_hi = u.astype(BF16)
    u_lo = (u - u_hi.astype(F32)).astype(BF16)
    perm = perm_ref[...]
    up = (jnp.dot(perm, u_hi, preferred_element_type=F32) + jnp.dot(perm, u_lo, preferred_element_type=F32))
    upb = up.astype(BF16)
    for j in range(n_blk):
        xj = jnp.dot(upb[:, blk_in * j:blk_in * (j + 1)], wx_ref[j], preferred_element_type=F32)
        xre[:, blk_st * j:blk_st * (j + 1)] = xj[:, :blk_st]
        xim[:, blk_st * j:blk_st * (j + 1)] = xj[:, blk_st:]

    n_chunks = n_state // lane_chunk
    e_re, e_im = [], []
    for c in range(n_chunks):
        cs = slice(lane_chunk * c, lane_chunk * (c + 1))
        ar = jnp.broadcast_to(pre_ref[0:1, cs], (SUBLANES, lane_chunk))
        ai = jnp.broadcast_to(pim_ref[0:1, cs], (SUBLANES, lane_chunk))

        def scan_body(r, h, cs=cs, ar=ar, ai=ai):
            hr, hi = h
            rows = pl.ds(pl.multiple_of(r * SUBLANES, SUBLANES), SUBLANES)
            nr = ar * hr - ai * hi + xre[rows, cs]
            ni = ar * hi + ai * hr + xim[rows, cs]
            xre[rows, cs] = nr
            xim[rows, cs] = ni
            return nr, ni

        z = jnp.zeros((SUBLANES, lane_chunk), F32)
        er, ei = lax.fori_loop(0, R, scan_body, (z, z), unroll=4)
        e_re.append(er)
        e_im.append(ei)
    e_re = jnp.concatenate(e_re, axis=1)
    e_im = jnp.concatenate(e_im, axis=1)

    g1r = jnp.broadcast_to(gre_ref[0:1, :], e_re.shape)
    g1i = jnp.broadcast_to(gim_ref[0:1, :], e_re.shape)
    if chained:
        t = pl.program_id(1)

        @pl.when(t == 0)
        def _():
            cre[...] = jnp.zeros(cre.shape, F32)
            cim[...] = jnp.zeros(cim.shape, F32)

        sub = lax.broadcasted_iota(jnp.int32, e_re.shape, 0)
        c_r = cre[...]
        c_i = cim[...]
        inj_r, inj_i = _cmul(g1r, g1i, c_r, c_i)
        xr = e_re + jnp.where(sub == 0, inj_r, 0.0)
        xi = e_im + jnp.where(sub == 0, inj_i, 0.0)
        for lvl, k in enumerate((1, 2, 4)):
            gr = jnp.broadcast_to(gre_ref[lvl:lvl + 1, :], e_re.shape)
            gi = jnp.broadcast_to(gim_ref[lvl:lvl + 1, :], e_re.shape)
            sr = jnp.where(sub >= k, pltpu.roll(xr, k, 0), 0.0)
            si = jnp.where(sub >= k, pltpu.roll(xi, k, 0), 0.0)
            tr, ti = _cmul(gr, gi, sr, si)
            xr = xr + tr
            xi = xi + ti
        hend_r, hend_i = xr, xi
        hc_r = jnp.where(sub == 0, c_r, pltpu.roll(hend_r, 1, 0))
        hc_i = jnp.where(sub == 0, c_i, pltpu.roll(hend_i, 1, 0))
        last_r = jnp.broadcast_to(hend_r[SUBLANES - 1:SUBLANES, :], e_re.shape)
        last_i = jnp.broadcast_to(hend_i[SUBLANES - 1:SUBLANES, :], e_re.shape)
        cre[...] = last_r
        cim[...] = last_i
        hre_ref[...] = last_r[0:1, :]
        him_ref[...] = last_i[0:1, :]
    else:
        hc_r = h0re_ref[...]
        hc_i = h0im_ref[...]
        tr, ti = _cmul(g1r, g1i, hc_r, hc_i)
        hre_ref[...] = e_re + tr
        him_ref[...] = e_im + ti

    for c in range(n_chunks):
        cs = slice(lane_chunk * c, lane_chunk * (c + 1))
        hcr = hc_r[:, cs]
        hci = hc_i[:, cs]

        def fix_body(r, carry, cs=cs, hcr=hcr, hci=hci):
            rows = pl.ds(pl.multiple_of(r * SUBLANES, SUBLANES), SUBLANES)
            pr = jnp.broadcast_to(pre_ref[pl.ds(r, 1), cs], hcr.shape)
            pi = jnp.broadcast_to(pim_ref[pl.ds(r, 1), cs], hcr.shape)
            tr, ti = _cmul(pr, pi, hcr, hci)
            xre[rows, cs] = xre[rows, cs] + tr
            xim[rows, cs] = xim[rows, cs] + ti
            return carry

        lax.fori_loop(0, R, fix_body, 0, unroll=4)

    ys = []
    for j in range(n_blk):
        st = slice(blk_st * j, blk_st * (j + 1))
        ys.append(jnp.dot(xre[:, st].astype(BF16), cb_ref[j, 0:blk_st, :], preferred_element_type=F32)
                  + jnp.dot(xim[:, st].astype(BF16), cb_ref[j, blk_st:, :], preferred_element_type=F32))
    y = jnp.concatenate(ys, axis=1) + d_ref[...] * up
    y = 0.5 * y * (1.0 + jnp.tanh(math.sqrt(2.0 / math.pi) * (y + 0.044715 * (y * y * y))))
    gl = jnp.dot(y.astype(BF16), wglu_ref[...], preferred_element_type=F32)
    half = gl.shape[1] // 2
    z = gl[:, :half] * (1.0 / (1.0 + jnp.exp(-gl[:, half:])))
    z = z * lax.rsqrt(jnp.mean(z * z, axis=-1, keepdims=True) + RMS_EPS) * ng_ref[...]
    y_ref[...] = jnp.dot(permt_ref[...], z.astype(BF16), preferred_element_type=F32).astype(y_ref.dtype)


def _s5_tables(a_re, a_im, log_dt, b_re, b_im, c_re, c_im, n_steps):
    G, N = a_re.shape
    C = b_re.shape[2]
    dt = jnp.exp(log_dt.astype(F32))[:, None]
    a_re = a_re.astype(F32)
    a_im = a_im.astype(F32)

    def power(k):
        mag = jnp.exp(k * a_re * dt)
        return mag * jnp.cos(k * a_im * dt), mag * jnp.sin(k * a_im * dt)

    ab_re, ab_im = power(1.0)
    nr, ni = ab_re - 1.0, ab_im
    den = a_re * a_re + a_im * a_im
    cf_re = (nr * a_re + ni * a_im) / den
    cf_im = (ni * a_re - nr * a_im) / den
    ks = jnp.arange(1, n_steps + 1, dtype=F32)[:, None, None]
    p_re, p_im = power(ks)
    g_re, g_im = zip(*[power(float(m * n_steps)) for m in (1, 2, 4)])
    pad = jnp.zeros((SUBLANES - 3, G * N), F32)
    g_re = jnp.concatenate([jnp.stack(g_re).reshape(3, G * N), pad])
    g_im = jnp.concatenate([jnp.stack(g_im).reshape(3, G * N), pad])
    wx_re = cf_re[..., None] * b_re - cf_im[..., None] * b_im
    wx_im = cf_re[..., None] * b_im + cf_im[..., None] * b_re
    gpb = LANES // C
    nb = G // gpb
    eye = jnp.eye(gpb, dtype=F32)

    def in_block(w):
        w = w.reshape(nb, gpb, N, C)
        return jnp.einsum('jgnc,gh->jgchn', w, eye).reshape(nb, gpb * C, gpb * N)

    def out_block(w):
        w = w.reshape(nb, gpb, C, N)
        return jnp.einsum('jgcn,gh->jgnhc', w, eye).reshape(nb, gpb * N, gpb * C)

    wx = jnp.concatenate([in_block(wx_re), in_block(wx_im)], axis=2).astype(BF16)
    cb = jnp.concatenate([out_block(c_re.astype(F32)), out_block(-c_im.astype(F32))], axis=1).astype(BF16)
    return (p_re.reshape(n_steps, G * N), p_im.reshape(n_steps, G * N), g_re, g_im, wx, cb)


def _perm_matrices(n_steps):
    tt = SUBLANES * n_steps
    dst = jnp.arange(tt)
    src = (dst % SUBLANES) * n_steps + dst // SUBLANES
    perm = (src[:, None] == jnp.arange(tt)[None, :]).astype(BF16)
    return perm, perm.T


def _s5_mixer(u, tables, d, w_glu_bf, norm_g, *, n_steps, h0=None):
    B, L, W = u.shape
    p_re, p_im, g_re, g_im, wx, cb = tables
    n_state = p_re.shape[1]
    tt = SUBLANES * n_steps
    perm, permt = _perm_matrices(n_steps)
    chained = h0 is None
    const2 = lambda *_: (0, 0)
    const3 = lambda *_: (0, 0, 0)
    common_specs = [pl.BlockSpec(perm.shape, const2), pl.BlockSpec(perm.shape, const2),
                    pl.BlockSpec(wx.shape, const3), pl.BlockSpec(cb.shape, const3),
                    pl.BlockSpec((1, W), const2),
                    pl.BlockSpec(p_re.shape, const2), pl.BlockSpec(p_im.shape, const2),
                    pl.BlockSpec(g_re.shape, const2), pl.BlockSpec(g_im.shape, const2),
                    pl.BlockSpec(w_glu_bf.shape, const2), pl.BlockSpec((1, W), const2)]
    common_args = (perm, permt, wx, cb, d, p_re, p_im, g_re, g_im, w_glu_bf, norm_g)
    state_scratch = [pltpu.VMEM((tt, n_state), F32), pltpu.VMEM((tt, n_state), F32)]
    kern = functools.partial(_s5_kernel, n_steps=n_steps, chained=chained, lane_chunk=512)
    if chained:
        y, hre, him = pl.pallas_call(
            kern,
            grid=(B, L // tt),
            in_specs=[pl.BlockSpec((None, tt, W), lambda b, t: (b, t, 0))] + common_specs,
            out_specs=[pl.BlockSpec((None, tt, W), lambda b, t: (b, t, 0)),
                       pl.BlockSpec((None, 1, n_state), lambda b, t: (b, 0, 0)),
                       pl.BlockSpec((None, 1, n_state), lambda b, t: (b, 0, 0))],
            out_shape=[jax.ShapeDtypeStruct((B, L, W), BF16),
                       jax.ShapeDtypeStruct((B, 1, n_state), F32),
                       jax.ShapeDtypeStruct((B, 1, n_state), F32)],
            scratch_shapes=state_scratch + [pltpu.VMEM((SUBLANES, n_state), F32)] * 2,
            compiler_params=_cparams("parallel", "arbitrary"),
            name="s5_mixer_chained",
        )(u, *common_args)
        return y, hre.reshape(B, n_state), him.reshape(B, n_state)
    u2 = u.reshape(B * L, W)
    h0re, h0im = h0
    y, hre, him = pl.pallas_call(
        kern,
        grid=(B // SUBLANES,),
        in_specs=[pl.BlockSpec((tt, W), lambda i: (i, 0))] + common_specs
        + [pl.BlockSpec((SUBLANES, n_state), lambda i: (i, 0))] * 2,
        out_specs=[pl.BlockSpec((tt, W), lambda i: (i, 0)),
                   pl.BlockSpec((SUBLANES, n_state), lambda i: (i, 0)),
                   pl.BlockSpec((SUBLANES, n_state), lambda i: (i, 0))],
        out_shape=[jax.ShapeDtypeStruct((B * L, W), BF16),
                   jax.ShapeDtypeStruct((B, n_state), F32),
                   jax.ShapeDtypeStruct((B, n_state), F32)],
        scratch_shapes=state_scratch,
        compiler_params=_cparams("parallel"),
        name="s5_mixer_batched",
    )(u2, *common_args, h0re, h0im)
    return y.reshape(B, L, W), hre, him


def _layer_norm(z, g, b):
    mu = jnp.mean(z, axis=-1, keepdims=True)
    zc = z - mu
    var = jnp.mean(zc * zc, axis=-1, keepdims=True)
    return zc * lax.rsqrt(var + LN_EPS) * g + b


def _mix_router_kernel(att_ref, ssm_ref, x_ref, wa_ref, wb_ref, g_ref, b_ref, rhi_ref, rlo_ref, rb_ref,
                       x1_ref, gates_ref, *, alpha):
    mix = (jnp.dot(att_ref[...], wa_ref[...], preferred_element_type=F32)
           + jnp.dot(ssm_ref[...], wb_ref[...], preferred_element_type=F32))
    x1 = _layer_norm(alpha * x_ref[...] + mix, g_ref[...], b_ref[...])
    x1_ref[...] = x1
    hi = x1.astype(BF16)
    lo = (x1 - hi.astype(F32)).astype(BF16)
    lg = (jnp.dot(hi, rhi_ref[...], preferred_element_type=F32)
          + jnp.dot(lo, rhi_ref[...], preferred_element_type=F32)
          + jnp.dot(hi, rlo_ref[...], preferred_element_type=F32)) + rb_ref[...]
    lane = lax.broadcasted_iota(jnp.int32, lg.shape, 1)
    big = jnp.int32(LANES)
    is_grp = (lane >= N_EXPERTS) & (lane < N_EXPERTS + N_EXPERT_GROUPS)
    gl = jnp.where(is_grp, lg, NEG_BIG)
    gexp = jnp.where(is_grp, jnp.exp(gl - jnp.max(gl, axis=-1, keepdims=True)), 0.0)
    gprob = gexp / jnp.sum(gexp, axis=-1, keepdims=True)
    g_w = jnp.max(gprob, axis=-1, keepdims=True)
    g_idx = jnp.min(jnp.where(is_grp & (gprob == g_w), lane - N_EXPERTS, big), axis=-1, keepdims=True)
    valid = (lane < N_EXPERTS) & ((lane // EXPERTS_PER_GROUP) == g_idx)
    el = jnp.where(valid, lg, NEG_BIG)
    eexp = jnp.where(valid, jnp.exp(el - jnp.max(el, axis=-1, keepdims=True)), 0.0)
    eprob = eexp / jnp.sum(eexp, axis=-1, keepdims=True)
    w1 = jnp.max(jnp.where(valid, eprob, -1.0), axis=-1, keepdims=True)
    i1 = jnp.min(jnp.where(valid & (eprob == w1), lane, big), axis=-1, keepdims=True)
    rest = valid & (lane != i1)
    w2 = jnp.max(jnp.where(rest, eprob, -1.0), axis=-1, keepdims=True)
    i2 = jnp.min(jnp.where(rest & (eprob == w2), lane, big), axis=-1, keepdims=True)
    den = w1 + w2
    gates_ref[...] = (jnp.where(lane == i1, g_w * (w1 / den), 0.0)
                      + jnp.where(lane == i2, g_w * (w2 / den), 0.0))


def _mix_router(att, ssm, x2d, wa, wb, g, b, rhi, rlo, rb, *, tm, alpha):
    T, D = x2d.shape
    Wh = att.shape[1]
    row = lambda i: (i, 0)
    const = lambda i: (0, 0)
    return pl.pallas_call(
        functools.partial(_mix_router_kernel, alpha=alpha),
        grid=(T // tm,),
        in_specs=[pl.BlockSpec((tm, Wh), row), pl.BlockSpec((tm, Wh), row), pl.BlockSpec((tm, D), row),
                  pl.BlockSpec(wa.shape, const), pl.BlockSpec(wb.shape, const),
                  pl.BlockSpec((1, D), const), pl.BlockSpec((1, D), const),
                  pl.BlockSpec(rhi.shape, const), pl.BlockSpec(rlo.shape, const), pl.BlockSpec((1, LANES), const)],
        out_specs=[pl.BlockSpec((tm, D), row), pl.BlockSpec((tm, LANES), row)],
        out_shape=[jax.ShapeDtypeStruct((T, D), F32), jax.ShapeDtypeStruct((T, LANES), F32)],
        compiler_params=_cparams("parallel"),
        name="out_proj_ln_router",
    )(att, ssm, x2d, wa, wb, g, b, rhi, rlo, rb)


def _moe_kernel(x1_ref, gates_ref, wg_ref, wu_ref, wd_ref, g_ref, b_ref, o_ref, acc_ref, *, alpha):
    x1 = x1_ref[...]
    xb = x1.astype(BF16)
    gates = gates_ref[...]
    for e in range(wg_ref.shape[0]):
        hg = jnp.dot(xb, wg_ref[e], preferred_element_type=F32)
        hu = jnp.dot(xb, wu_ref[e], preferred_element_type=F32)
        h = hg * (1.0 / (1.0 + jnp.exp(-hg))) * hu * gates[:, e:e + 1]
        contrib = jnp.dot(h.astype(BF16), wd_ref[e], preferred_element_type=F32)
        if e == 0:
            acc_ref[...] = contrib
        else:
            acc_ref[...] += contrib
    o_ref[...] = _layer_norm(alpha * x1 + acc_ref[...], g_ref[...], b_ref[...])


def _moe(x1, gates, wg, wu, wd, g, b, *, tm, alpha):
    T, D = x1.shape
    row = lambda i: (i, 0)
    const2 = lambda i: (0, 0)
    const3 = lambda i: (0, 0, 0)
    once = pl.Buffered(1)
    return pl.pallas_call(
        functools.partial(_moe_kernel, alpha=alpha),
        grid=(T // tm,),
        in_specs=[pl.BlockSpec((tm, D), row), pl.BlockSpec((tm, LANES), row),
                  pl.BlockSpec(wg.shape, const3, pipeline_mode=once),
                  pl.BlockSpec(wu.shape, const3, pipeline_mode=once),
                  pl.BlockSpec(wd.shape, const3, pipeline_mode=once),
                  pl.BlockSpec((1, D), const2), pl.BlockSpec((1, D), const2)],
        out_specs=pl.BlockSpec((tm, D), row),
        out_shape=jax.ShapeDtypeStruct((T, D), F32),
        scratch_shapes=[pltpu.VMEM((tm, D), F32)],
        compiler_params=_cparams("parallel"),
        name="moe_ln",
    )(x1, gates, wg, wu, wd, g, b)


def _rope_tables(pos):
    half = HEAD_DIM // 2
    inv = 1.0 / (ROPE_THETA ** (jnp.arange(half, dtype=F32) / half))
    ang = pos.astype(F32)[:, None] * inv[None, :]
    cos = jnp.cos(ang)
    sin = jnp.sin(ang)
    cos_t = jnp.tile(cos, (1, LANES // half))
    sin_t = jnp.tile(jnp.concatenate([-sin, sin], axis=1), (1, LANES // HEAD_DIM))
    return cos_t, sin_t


def _row_tile(n, pref):
    t = min(n, pref)
    while n % t:
        t //= 2
    return t


def kernel(x_prompt, x_sample, cache_k, cache_v, state_ssm_re, state_ssm_im, w_in, lam_q1, lam_k1, lam_q2, lam_k2, subln_g, ssm_a_re, ssm_a_im, ssm_log_dt, ssm_b_re, ssm_b_im, ssm_c_re, ssm_c_im, ssm_d, w_glu, ssm_norm_g, w_out, ln1_g, ln1_b, w_grp, b_grp, w_rt, b_rt, w_gate, w_up, w_down, ln2_g, ln2_b):
    depth = w_in.shape[0]
    assert depth == 1, "single-layer step"
    B, L, D = x_prompt.shape
    Bs, S, _ = x_sample.shape
    P = cache_k.shape[2]
    H = cache_k.shape[3]
    n_qk = H * 2 * HEAD_DIM
    G, N = ssm_a_re.shape[1], ssm_a_re.shape[2]
    alpha = (2.0 * depth) ** 0.25
    lam_init = 0.8 - 0.6 * math.exp(-0.3 * 0)
    out_scale = 1.0 - lam_init
    l = 0

    w_in_bf = w_in[l].astype(BF16)
    lam = (jnp.exp(jnp.sum(lam_q1[l].astype(F32) * lam_k1[l].astype(F32)))
           - jnp.exp(jnp.sum(lam_q2[l].astype(F32) * lam_k2[l].astype(F32))) + lam_init).reshape(1).astype(F32)
    g_sub = subln_g[l].astype(F32).reshape(1, V_DIM)
    w_glu_bf = w_glu[l].astype(BF16)
    ssm_w = D - H * V_DIM
    d_row = ssm_d[l].astype(F32).reshape(1, ssm_w)
    ng_row = ssm_norm_g[l].astype(F32).reshape(1, ssm_w)
    wa = w_out[l, :H * V_DIM].astype(BF16)
    wb = w_out[l, H * V_DIM:].astype(BF16)
    ln1g, ln1b = ln1_g[l].astype(F32).reshape(1, D), ln1_b[l].astype(F32).reshape(1, D)
    ln2g, ln2b = ln2_g[l].astype(F32).reshape(1, D), ln2_b[l].astype(F32).reshape(1, D)
    w_r = jnp.concatenate([w_rt[l].astype(F32), w_grp[l].astype(F32),
                           jnp.zeros((D, LANES - N_EXPERTS - N_EXPERT_GROUPS), F32)], axis=1)
    rhi = w_r.astype(BF16)
    rlo = (w_r - rhi.astype(F32)).astype(BF16)
    rb = jnp.concatenate([b_rt[l].astype(F32), b_grp[l].astype(F32),
                          jnp.zeros((LANES - N_EXPERTS - N_EXPERT_GROUPS,), F32)]).reshape(1, LANES)
    wg, wu, wd = w_gate[l].astype(BF16), w_up[l].astype(BF16), w_down[l].astype(BF16)

    def tail(att, ssm, x2d):
        tm = _row_tile(x2d.shape[0], 512)
        x1, gates = _mix_router(att, ssm, x2d, wa, wb, ln1g, ln1b, rhi, rlo, rb, tm=tm, alpha=alpha)
        return _moe(x1, gates, wg, wu, wd, ln2g, ln2b, tm=tm, alpha=alpha)

    cos_p, sin_p = _rope_tables(jnp.arange(L))
    xp2 = x_prompt.reshape(B * L, D)
    qp, kpf, kpb, vpf, vpb, up = _in_proj(xp2, w_in_bf, cos_p, sin_p, tm=_row_tile(L, 512))
    r3 = lambda a: a.reshape(B, L, -1)
    att_p = _attn_prompt(lam, r3(qp), r3(kpb), r3(vpb), g_sub, tq=_row_tile(L, 256), out_scale=out_scale)
    n_steps_p = _row_tile(L, 256) // SUBLANES
    tabs_p = _s5_tables(ssm_a_re[l], ssm_a_im[l], ssm_log_dt[l], ssm_b_re[l].astype(F32), ssm_b_im[l].astype(F32),
                        ssm_c_re[l], ssm_c_im[l], n_steps_p)
    ssm_p, hre_p, him_p = _s5_mixer(r3(up), tabs_p, d_row, w_glu_bf, ng_row, n_steps=n_steps_p)
    y_prompt = tail(att_p.reshape(B * L, -1), ssm_p.reshape(B * L, -1), xp2).reshape(B, L, D)

    cos_s, sin_s = _rope_tables(P + jnp.arange(S))
    cos_s, sin_s = jnp.tile(cos_s, (Bs, 1)), jnp.tile(sin_s, (Bs, 1))
    xs2 = x_sample.reshape(Bs * S, D)
    qs, ksf, ksb, vsf, vsb, us = _in_proj(xs2, w_in_bf, cos_s, sin_s, tm=_row_tile(Bs * S, 512))
    s3 = lambda a: a.reshape(Bs, S, -1)
    att_s = _attn_sample(lam, s3(qs), cache_k[l].reshape(Bs, P, n_qk), cache_v[l].reshape(Bs, P, H * V_DIM),
                         s3(ksb), s3(vsb), g_sub, out_scale=out_scale)
    tabs_s = _s5_tables(ssm_a_re[l], ssm_a_im[l], ssm_log_dt[l], ssm_b_re[l].astype(F32), ssm_b_im[l].astype(F32),
                        ssm_c_re[l], ssm_c_im[l], S)
    h0 = (state_ssm_re[l].astype(F32).reshape(Bs, G * N), state_ssm_im[l].astype(F32).reshape(Bs, G * N))
    ssm_s, hre_s, him_s = _s5_mixer(s3(us), tabs_s, d_row, w_glu_bf, ng_row, n_steps=S, h0=h0)
    y_sample = tail(att_s.reshape(Bs * S, -1), ssm_s.reshape(Bs * S, -1), xs2).reshape(Bs, S, D)

    return (y_prompt, y_sample,
            kpf.reshape(1, B, L, H, 2, HEAD_DIM), vpf.reshape(1, B, L, H, V_DIM),
            hre_p.reshape(1, B, G, N), him_p.reshape(1, B, G, N),
            ksf.reshape(1, Bs, S, H, 2, HEAD_DIM), vsf.reshape(1, Bs, S, H, V_DIM),
            hre_s.reshape(1, Bs, G, N), him_s.reshape(1, Bs, G, N))
```

```python
import functools
import math

import jax
import jax.numpy as jnp
from jax import lax
from jax.experimental import pallas as pl
from jax.experimental.pallas import tpu as pltpu

F32 = jnp.float32
BF16 = jnp.bfloat16

HEAD_DIM = 64
V_DIM = 2 * HEAD_DIM
CHUNK = 64
SSM_GROUP = 16
SSM_STATE = 64
N_EXPERT_GROUPS = 4
EXPERTS_PER_GROUP = 4
N_EXPERTS = N_EXPERT_GROUPS * EXPERTS_PER_GROUP
ROPE_THETA = 10000.0
LN_EPS = 1e-5
RMS_EPS = 1e-6
LANES = 128
SUBLANES = 8
NEG_BIG = -1e30
LOG2E = 1.4426950408889634
VMEM_LIMIT = 52 * 1024 * 1024


def _cparams(*sem):
    return pltpu.CompilerParams(dimension_semantics=sem, vmem_limit_bytes=VMEM_LIMIT)


def _store_transposed_tiles(ref, a):
    n_t, n_h, _, t = ref.shape
    at = a.T
    for n in range(n_t):
        for h in range(n_h):
            ref[n, h] = at[V_DIM * h:V_DIM * (h + 1), t * n:t * (n + 1)].astype(ref.dtype)


def _inproj_kernel(x_ref, w_ref, cos_ref, sin_ref, q_ref, kf_ref, kb_ref, vf_ref, vb_ref, u_ref,
                   *, q_scale, n_qk, transposed):
    xb = x_ref[...].astype(BF16)
    cos = cos_ref[...]
    sin = sin_ref[...]
    lane = lax.broadcasted_iota(jnp.int32, cos.shape, 1)
    first = (lane & (HEAD_DIM - 1)) < HEAD_DIM // 2

    def rope(t):
        rot = jnp.where(first, pltpu.roll(t, LANES - HEAD_DIM // 2, 1), pltpu.roll(t, HEAD_DIM // 2, 1))
        return t * cos + rot * sin

    pq = jnp.dot(xb, w_ref[:, 0:n_qk], preferred_element_type=F32)
    qr = jnp.concatenate([rope(pq[:, LANES * j:LANES * (j + 1)]) for j in range(n_qk // LANES)], axis=1) * q_scale
    if transposed:
        _store_transposed_tiles(q_ref, qr)
    else:
        q_ref[...] = qr.astype(BF16)
    pk = jnp.dot(xb, w_ref[:, n_qk:2 * n_qk], preferred_element_type=F32)
    for j in range(n_qk // LANES):
        sl = slice(LANES * j, LANES * (j + 1))
        kr = rope(pk[:, sl])
        kf_ref[:, sl] = kr
        kb_ref[:, sl] = kr.astype(BF16)
    n_v = vf_ref.shape[1]
    pv = jnp.dot(xb, w_ref[:, 2 * n_qk:2 * n_qk + n_v], preferred_element_type=F32)
    vf_ref[...] = pv
    if transposed:
        _store_transposed_tiles(vb_ref, pv)
    else:
        vb_ref[...] = pv.astype(BF16)
    u_ref[...] = jnp.dot(xb, w_ref[:, 2 * n_qk + n_v:], preferred_element_type=F32)


def _in_proj(x2d, w_bf, cos_t, sin_t, *, tm, t_attn=None):
    T, D = x2d.shape
    n_cols = w_bf.shape[1]
    n_qk = n_cols // 4
    n_tab = cos_t.shape[0] // tm
    row = lambda i: (i, 0)
    tab = lambda i: (i % n_tab, 0)
    out_sds = lambda dt: jax.ShapeDtypeStruct((T, n_qk), dt)
    spec = pl.BlockSpec((tm, n_qk), row)
    qv_spec, qv_sds = spec, out_sds(BF16)
    if t_attn is not None:
        H = n_qk // V_DIM
        qv_spec = pl.BlockSpec((tm // t_attn, H, V_DIM, t_attn), lambda i: (i, 0, 0, 0))
        qv_sds = jax.ShapeDtypeStruct((T // t_attn, H, V_DIM, t_attn), BF16)
    return pl.pallas_call(
        functools.partial(_inproj_kernel, q_scale=HEAD_DIM ** -0.5 * LOG2E, n_qk=n_qk,
                          transposed=t_attn is not None),
        grid=(T // tm,),
        in_specs=[pl.BlockSpec((tm, D), row),
                  pl.BlockSpec((D, n_cols), lambda i: (0, 0)),
                  pl.BlockSpec((tm, LANES), tab),
                  pl.BlockSpec((tm, LANES), tab)],
        out_specs=[qv_spec, spec, spec, spec, qv_spec, spec],
        out_shape=[qv_sds, out_sds(F32), out_sds(BF16), out_sds(F32), qv_sds, out_sds(F32)],
        compiler_params=_cparams("parallel"),
        name="in_proj_rope",
    )(x2d, w_bf, cos_t, sin_t)


def _split_maps(q):
    lane = lax.broadcasted_iota(jnp.int32, q.shape, 1)
    zero = jnp.zeros_like(q)
    return jnp.concatenate([jnp.where(lane < HEAD_DIM, q, zero), jnp.where(lane >= HEAD_DIM, q, zero)], axis=0)


def _qk(qs, kb):
    return lax.dot_general(qs, kb, (((1,), (1,)), ((), ())), preferred_element_type=F32)


def _subln(o, g, out_scale):
    ms = jnp.mean(o * o, axis=-1, keepdims=True)
    return o * lax.rsqrt(ms + RMS_EPS) * g * out_scale


def _attn_prompt_kernel(lam_ref, qt_ref, k_ref, vt_ref, g_ref, o_ref, qs_ref, acc_ref, s0_ref, s1_ref,
                        p0_ref, p1_ref, m_ref, l_ref, a_ref, *, tq, out_scale):
    i = pl.program_id(2)
    s_buf = (s0_ref, s1_ref)
    p_buf = (p0_ref, p1_ref)
    qt = qt_ref[...]
    row = lax.broadcasted_iota(jnp.int32, qt.shape, 0)
    zero = jnp.zeros_like(qt)
    qs_ref[:, 0:tq] = jnp.where(row < HEAD_DIM, qt, zero)
    qs_ref[:, tq:2 * tq] = jnp.where(row >= HEAD_DIM, qt, zero)
    acc_ref[...] = jnp.zeros(acc_ref.shape, F32)
    p1_ref[...] = jnp.zeros(p1_ref.shape, p1_ref.dtype)
    m_ref[...] = jnp.full(m_ref.shape, NEG_BIG, F32)
    l_ref[...] = jnp.zeros(l_ref.shape, F32)
    a_ref[...] = jnp.ones(a_ref.shape, F32)

    def stage_a(kt, slot):
        kb = k_ref[pl.ds(pl.multiple_of(kt * tq, tq), tq), :]
        s_buf[slot][...] = jnp.dot(kb, qs_ref[...], preferred_element_type=F32)

    def stage_b(slot, masked):
        st = s_buf[slot][...]
        if masked:
            c = lax.broadcasted_iota(jnp.int32, st.shape, 0)
            r = lax.broadcasted_iota(jnp.int32, st.shape, 1)
            r = jnp.where(r >= tq, r - tq, r)
            st = jnp.where((c // CHUNK) <= (r // CHUNK), st, NEG_BIG)
        m_prev = m_ref[...]
        m_new = jnp.maximum(m_prev, jnp.max(st, axis=0, keepdims=True))
        alpha = jnp.exp2(m_prev - m_new)
        pt = jnp.exp2(st - m_new)
        l_ref[...] = alpha * l_ref[...] + jnp.sum(pt, axis=0, keepdims=True)
        m_ref[...] = m_new
        a_ref[...] = alpha
        p_buf[slot][...] = pt.astype(BF16)

    def stage_c(kt, slot):
        pv = jnp.dot(vt_ref[jnp.maximum(kt, 0)], p_buf[slot][...], preferred_element_type=F32)
        acc_ref[...] = a_ref[...] * acc_ref[...] + pv

    def step(t, par):
        stage_c(t - 2, par)
        stage_a(t, par)
        stage_b(1 - par, False)

    stage_a(0, 0)

    def pair(j, carry):
        step(2 * j + 1, 1)
        step(2 * j + 2, 0)
        return carry

    lax.fori_loop(0, i // 2, pair, 0)

    @pl.when(i % 2 == 0)
    def _():
        stage_c(i - 1, 1)
        stage_b(0, True)
        stage_c(i, 0)

    @pl.when(i % 2 == 1)
    def _():
        step(i, 1)
        stage_c(i - 1, 0)
        stage_b(1, True)
        stage_c(i, 1)

    on = acc_ref[...] * (1.0 / l_ref[...])
    ot = on[:, 0:tq] - lam_ref[0] * on[:, tq:2 * tq]
    o_ref[...] = _subln(ot.T, g_ref[...], out_scale).astype(o_ref.dtype)


def _attn_prompt(lam, qt, k, vt, g, *, out_scale):
    B, L, W = k.shape
    n_t, H, _, t = qt.shape
    nq = L // t
    return pl.pallas_call(
        functools.partial(_attn_prompt_kernel, tq=t, out_scale=out_scale),
        grid=(B, H, nq),
        in_specs=[pl.BlockSpec(memory_space=pltpu.SMEM),
                  pl.BlockSpec((None, None, V_DIM, t), lambda b, h, i: (b * nq + i, h, 0, 0)),
                  pl.BlockSpec((None, L, V_DIM), lambda b, h, i: (b, 0, h)),
                  pl.BlockSpec((nq, None, V_DIM, t), lambda b, h, i: (b, h, 0, 0)),
                  pl.BlockSpec((1, V_DIM), lambda b, h, i: (0, 0))],
        out_specs=pl.BlockSpec((None, t, V_DIM), lambda b, h, i: (b, i, h)),
        out_shape=jax.ShapeDtypeStruct((B, L, W), BF16),
        scratch_shapes=[pltpu.VMEM((V_DIM, 2 * t), BF16), pltpu.VMEM((V_DIM, 2 * t), F32),
                        pltpu.VMEM((t, 2 * t), F32), pltpu.VMEM((t, 2 * t), F32),
                        pltpu.VMEM((t, 2 * t), BF16), pltpu.VMEM((t, 2 * t), BF16),
                        pltpu.VMEM((1, 2 * t), F32), pltpu.VMEM((1, 2 * t), F32), pltpu.VMEM((1, 2 * t), F32)],
        compiler_params=_cparams("parallel", "parallel", "parallel"),
        name="diff_attn_prompt",
    )(lam, qt, k, vt, g)


def _attn_sample_kernel(lam_ref, q_ref, kc_ref, vc_ref, kn_ref, vn_ref, g_ref, o_ref, *, past_len, out_scale):
    tq = q_ref.shape[0]
    qs = _split_maps(q_ref[...])
    s_c = _qk(qs, kc_ref[...].astype(BF16))
    s_n = _qk(qs, kn_ref[...])
    r = lax.broadcasted_iota(jnp.int32, s_n.shape, 0)
    c = lax.broadcasted_iota(jnp.int32, s_n.shape, 1)
    r = jnp.where(r >= tq, r - tq, r)
    s_n = jnp.where(((past_len + c) // CHUNK) <= ((past_len + r) // CHUNK), s_n, NEG_BIG)
    m = jnp.maximum(jnp.max(s_c, axis=-1, keepdims=True), jnp.max(s_n, axis=-1, keepdims=True))
    p_c = jnp.exp2(s_c - m)
    p_n = jnp.exp2(s_n - m)
    inv_l = 1.0 / (jnp.sum(p_c, axis=-1, keepdims=True) + jnp.sum(p_n, axis=-1, keepdims=True))
    lam = lam_ref[0]
    p_c = p_c * inv_l
    p_n = p_n * inv_l
    w_c = p_c[0:tq] - lam * p_c[tq:2 * tq]
    w_n = p_n[0:tq] - lam * p_n[tq:2 * tq]
    o = (jnp.dot(w_c.astype(BF16), vc_ref[...].astype(BF16), preferred_element_type=F32)
         + jnp.dot(w_n.astype(BF16), vn_ref[...], preferred_element_type=F32))
    o_ref[...] = _subln(o, g_ref[...], out_scale).astype(o_ref.dtype)


def _attn_sample(lam, q, cache_k, cache_v, k_new, v_new, g, *, out_scale):
    B, S, W = q.shape
    P = cache_k.shape[1]
    H = W // V_DIM
    blk_new = pl.BlockSpec((None, S, V_DIM), lambda b, h: (b, 0, h))
    blk_cache = pl.BlockSpec((None, P, V_DIM), lambda b, h: (b, 0, h))
    return pl.pallas_call(
        functools.partial(_attn_sample_kernel, past_len=P, out_scale=out_scale),
        grid=(B, H),
        in_specs=[pl.BlockSpec(memory_space=pltpu.SMEM), blk_new, blk_cache, blk_cache, blk_new, blk_new,
                  pl.BlockSpec((1, V_DIM), lambda b, h: (0, 0))],
        out_specs=blk_new,
        out_shape=jax.ShapeDtypeStruct((B, S, W), BF16),
        compiler_params=_cparams("parallel", "parallel"),
        name="diff_attn_sample",
    )(lam, q, cache_k, cache_v, k_new, v_new, g)


def _cmul(ar, ai, br, bi):
    return ar * br - ai * bi, ar * bi + ai * br


def _s5_kernel(*refs, n_steps, chained, lane_chunk):
    if chained:
        (u_ref, perm_ref, permt_ref, wx_ref, cb_ref, d_ref, pre_ref, pim_ref, gre_ref, gim_ref,
         wglu_ref, ng_ref, y_ref, hre_ref, him_ref, xre, xim, cre, cim) = refs
    else:
        (u_ref, perm_ref, permt_ref, wx_ref, cb_ref, d_ref, pre_ref, pim_ref, gre_ref, gim_ref,
         wglu_ref, ng_ref, h0re_ref, h0im_ref, y_ref, hre_ref, him_ref, xre, xim) = refs
    R = n_steps
    n_state = xre.shape[1]
    n_blk = wx_ref.shape[0]
    blk_in = wx_ref.shape[1]
    blk_st = wx_ref.shape[2] // 2

    u = u_ref[...]
    u_hi = u.astype(BF16)
    u_lo = (u - u_hi.astype(F32)).astype(BF16)
    perm = perm_ref[...]
    up = (jnp.dot(perm, u_hi, preferred_element_type=F32) + jnp.dot(perm, u_lo, preferred_element_type=F32))
    upb = up.astype(BF16)
    for j in range(n_blk):
        xj = jnp.dot(upb[:, blk_in * j:blk_in * (j + 1)], wx_ref[j], preferred_element_type=F32)
        xre[:, blk_st * j:blk_st * (j + 1)] = xj[:, :blk_st]
        xim[:, blk_st * j:blk_st * (j + 1)] = xj[:, blk_st:]

    n_chunks = n_state // lane_chunk
    e_re, e_im = [], []
    for c in range(n_chunks):
        cs = slice(lane_chunk * c, lane_chunk * (c + 1))
        ar = jnp.broadcast_to(pre_ref[0:1, cs], (SUBLANES, lane_chunk))
        ai = jnp.broadcast_to(pim_ref[0:1, cs], (SUBLANES, lane_chunk))

        def scan_body(r, h, cs=cs, ar=ar, ai=ai):
            hr, hi = h
            rows = pl.ds(pl.multiple_of(r * SUBLANES, SUBLANES), SUBLANES)
            nr = ar * hr - ai * hi + xre[rows, cs]
            ni = ar * hi + ai * hr + xim[rows, cs]
            xre[rows, cs] = nr
            xim[rows, cs] = ni
            return nr, ni

        z = jnp.zeros((SUBLANES, lane_chunk), F32)
        er, ei = lax.fori_loop(0, R, scan_body, (z, z), unroll=4)
        e_re.append(er)
        e_im.append(ei)
    e_re = jnp.concatenate(e_re, axis=1)
    e_im = jnp.concatenate(e_im, axis=1)

    g1r = jnp.broadcast_to(gre_ref[0:1, :], e_re.shape)
    g1i = jnp.broadcast_to(gim_ref[0:1, :], e_re.shape)
    if chained:
        t = pl.program_id(1)

        @pl.when(t == 0)
        def _():
            cre[...] = jnp.zeros(cre.shape, F32)
            cim[...] = jnp.zeros(cim.shape, F32)

        sub = lax.broadcasted_iota(jnp.int32, e_re.shape, 0)
        c_r = cre[...]
        c_i = cim[...]
        inj_r, inj_i = _cmul(g1r, g1i, c_r, c_i)
        xr = e_re + jnp.where(sub == 0, inj_r, 0.0)
        xi = e_im + jnp.where(sub == 0, inj_i, 0.0)
        for lvl, k in enumerate((1, 2, 4)):
            gr = jnp.broadcast_to(gre_ref[lvl:lvl + 1, :], e_re.shape)
            gi = jnp.broadcast_to(gim_ref[lvl:lvl + 1, :], e_re.shape)
            sr = jnp.where(sub >= k, pltpu.roll(xr, k, 0), 0.0)
            si = jnp.where(sub >= k, pltpu.roll(xi, k, 0), 0.0)
            tr, ti = _cmul(gr, gi, sr, si)
            xr = xr + tr
            xi = xi + ti
        hend_r, hend_i = xr, xi
        hc_r = jnp.where(sub == 0, c_r, pltpu.roll(hend_r, 1, 0))
        hc_i = jnp.where(sub == 0, c_i, pltpu.roll(hend_i, 1, 0))
        last_r = jnp.broadcast_to(hend_r[SUBLANES - 1:SUBLANES, :], e_re.shape)
        last_i = jnp.broadcast_to(hend_i[SUBLANES - 1:SUBLANES, :], e_re.shape)
        cre[...] = last_r
        cim[...] = last_i
        hre_ref[...] = last_r[0:1, :]
        him_ref[...] = last_i[0:1, :]
    else:
        hc_r = h0re_ref[...]
        hc_i = h0im_ref[...]
        tr, ti = _cmul(g1r, g1i, hc_r, hc_i)
        hre_ref[...] = e_re + tr
        him_ref[...] = e_im + ti

    for c in range(n_chunks):
        cs = slice(lane_chunk * c, lane_chunk * (c + 1))
        hcr = hc_r[:, cs]
        hci = hc_i[:, cs]

        def fix_body(r, carry, cs=cs, hcr=hcr, hci=hci):
            rows = pl.ds(pl.multiple_of(r * SUBLANES, SUBLANES), SUBLANES)
            pr = jnp.broadcast_to(pre_ref[pl.ds(r, 1), cs], hcr.shape)
            pi = jnp.broadcast_to(pim_ref[pl.ds(r, 1), cs], hcr.shape)
            tr, ti = _cmul(pr, pi, hcr, hci)
            xre[rows, cs] = xre[rows, cs] + tr
            xim[rows, cs] = xim[rows, cs] + ti
            return carry

        lax.fori_loop(0, R, fix_body, 0, unroll=4)

    ys = []
    for j in range(n_blk):
        st = slice(blk_st * j, blk_st * (j + 1))
        ys.append(jnp.dot(xre[:, st].astype(BF16), cb_ref[j, 0:blk_st, :], preferred_element_type=F32)
                  + jnp.dot(xim[:, st].astype(BF16), cb_ref[j, blk_st:, :], preferred_element_type=F32))
    y = jnp.concatenate(ys, axis=1) + d_ref[...] * up
    y = 0.5 * y * (1.0 + jnp.tanh(math.sqrt(2.0 / math.pi) * (y + 0.044715 * (y * y * y))))
    gl = jnp.dot(y.astype(BF16), wglu_ref[...], preferred_element_type=F32)
    half = gl.shape[1] // 2
    z = gl[:, :half] * (1.0 / (1.0 + jnp.exp(-gl[:, half:])))
    z = z * lax.rsqrt(jnp.mean(z * z, axis=-1, keepdims=True) + RMS_EPS) * ng_ref[...]
    y_ref[...] = jnp.dot(permt_ref[...], z.astype(BF16), preferred_element_type=F32).astype(y_ref.dtype)


def _s5_tables(a_re, a_im, log_dt, b_re, b_im, c_re, c_im, n_steps):
    G, N = a_re.shape
    C = b_re.shape[2]
    dt = jnp.exp(log_dt.astype(F32))[:, None]
    a_re = a_re.astype(F32)
    a_im = a_im.astype(F32)

    def power(k):
        mag = jnp.exp(k * a_re * dt)
        return mag * jnp.cos(k * a_im * dt), mag * jnp.sin(k * a_im * dt)

    ab_re, ab_im = power(1.0)
    nr, ni = ab_re - 1.0, ab_im
    den = a_re * a_re + a_im * a_im
    cf_re = (nr * a_re + ni * a_im) / den
    cf_im = (ni * a_re - nr * a_im) / den
    ks = jnp.arange(1, n_steps + 1, dtype=F32)[:, None, None]
    p_re, p_im = power(ks)
    g_re, g_im = zip(*[power(float(m * n_steps)) for m in (1, 2, 4)])
    pad = jnp.zeros((SUBLANES - 3, G * N), F32)
    g_re = jnp.concatenate([jnp.stack(g_re).reshape(3, G * N), pad])
    g_im = jnp.concatenate([jnp.stack(g_im).reshape(3, G * N), pad])
    wx_re = cf_re[..., None] * b_re - cf_im[..., None] * b_im
    wx_im = cf_re[..., None] * b_im + cf_im[..., None] * b_re
    gpb = LANES // C
    nb = G // gpb
    eye = jnp.eye(gpb, dtype=F32)

    def in_block(w):
        w = w.reshape(nb, gpb, N, C)
        return jnp.einsum('jgnc,gh->jgchn', w, eye).reshape(nb, gpb * C, gpb * N)

    def out_block(w):
        w = w.reshape(nb, gpb, C, N)
        return jnp.einsum('jgcn,gh->jgnhc', w, eye).reshape(nb, gpb * N, gpb * C)

    wx = jnp.concatenate([in_block(wx_re), in_block(wx_im)], axis=2).astype(BF16)
    cb = jnp.concatenate([out_block(c_re.astype(F32)), out_block(-c_im.astype(F32))], axis=1).astype(BF16)
    return (p_re.reshape(n_steps, G * N), p_im.reshape(n_steps, G * N), g_re, g_im, wx, cb)


def _perm_matrices(n_steps):
    tt = SUBLANES * n_steps
    dst = jnp.arange(tt)
    src = (dst % SUBLANES) * n_steps + dst // SUBLANES
    perm = (src[:, None] == jnp.arange(tt)[None, :]).astype(BF16)
    return perm, perm.T


def _s5_mixer(u, tables, d, w_glu_bf, norm_g, *, n_steps, h0=None):
    B, L, W = u.shape
    p_re, p_im, g_re, g_im, wx, cb = tables
    n_state = p_re.shape[1]
    tt = SUBLANES * n_steps
    perm, permt = _perm_matrices(n_steps)
    chained = h0 is None
    const2 = lambda *_: (0, 0)
    const3 = lambda *_: (0, 0, 0)
    common_specs = [pl.BlockSpec(perm.shape, const2), pl.BlockSpec(perm.shape, const2),
                    pl.BlockSpec(wx.shape, const3), pl.BlockSpec(cb.shape, const3),
                    pl.BlockSpec((1, W), const2),
                    pl.BlockSpec(p_re.shape, const2), pl.BlockSpec(p_im.shape, const2),
                    pl.BlockSpec(g_re.shape, const2), pl.BlockSpec(g_im.shape, const2),
                    pl.BlockSpec(w_glu_bf.shape, const2), pl.BlockSpec((1, W), const2)]
    common_args = (perm, permt, wx, cb, d, p_re, p_im, g_re, g_im, w_glu_bf, norm_g)
    state_scratch = [pltpu.VMEM((tt, n_state), F32), pltpu.VMEM((tt, n_state), F32)]
    kern = functools.partial(_s5_kernel, n_steps=n_steps, chained=chained, lane_chunk=512)
    if chained:
        y, hre, him = pl.pallas_call(
            kern,
            grid=(B, L // tt),
            in_specs=[pl.BlockSpec((None, tt, W), lambda b, t: (b, t, 0))] + common_specs,
            out_specs=[pl.BlockSpec((None, tt, W), lambda b, t: (b, t, 0)),
                       pl.BlockSpec((None, 1, n_state), lambda b, t: (b, 0, 0)),
                       pl.BlockSpec((None, 1, n_state), lambda b, t: (b, 0, 0))],
            out_shape=[jax.ShapeDtypeStruct((B, L, W), BF16),
                       jax.ShapeDtypeStruct((B, 1, n_state), F32),
                       jax.ShapeDtypeStruct((B, 1, n_state), F32)],
            scratch_shapes=state_scratch + [pltpu.VMEM((SUBLANES, n_state), F32)] * 2,
            compiler_params=_cparams("parallel", "arbitrary"),
            name="s5_mixer_chained",
        )(u, *common_args)
        return y, hre.reshape(B, n_state), him.reshape(B, n_state)
    u2 = u.reshape(B * L, W)
    h0re, h0im = h0
    y, hre, him = pl.pallas_call(
        kern,
        grid=(B // SUBLANES,),
        in_specs=[pl.BlockSpec((tt, W), lambda i: (i, 0))] + common_specs
        + [pl.BlockSpec((SUBLANES, n_state), lambda i: (i, 0))] * 2,
        out_specs=[pl.BlockSpec((tt, W), lambda i: (i, 0)),
                   pl.BlockSpec((SUBLANES, n_state), lambda i: (i, 0)),
                   pl.BlockSpec((SUBLANES, n_state), lambda i: (i, 0))],
        out_shape=[jax.ShapeDtypeStruct((B * L, W), BF16),
                   jax.ShapeDtypeStruct((B, n_state), F32),
                   jax.ShapeDtypeStruct((B, n_state), F32)],
        scratch_shapes=state_scratch,
        compiler_params=_cparams("parallel"),
        name="s5_mixer_batched",
    )(u2, *common_args, h0re, h0im)
    return y.reshape(B, L, W), hre, him


def _layer_norm(z, g, b):
    mu = jnp.mean(z, axis=-1, keepdims=True)
    zc = z - mu
    var = jnp.mean(zc * zc, axis=-1, keepdims=True)
    return zc * lax.rsqrt(var + LN_EPS) * g + b


def _mix_router_kernel(att_ref, ssm_ref, x_ref, wa_ref, wb_ref, g_ref, b_ref, rhi_ref, rlo_ref, rb_ref,
                       x1_ref, gates_ref, *, alpha):
    mix = (jnp.dot(att_ref[...], wa_ref[...], preferred_element_type=F32)
           + jnp.dot(ssm_ref[...], wb_ref[...], preferred_element_type=F32))
    x1 = _layer_norm(alpha * x_ref[...] + mix, g_ref[...], b_ref[...])
    x1_ref[...] = x1
    hi = x1.astype(BF16)
    lo = (x1 - hi.astype(F32)).astype(BF16)
    lg = (jnp.dot(hi, rhi_ref[...], preferred_element_type=F32)
          + jnp.dot(lo, rhi_ref[...], preferred_element_type=F32)
          + jnp.dot(hi, rlo_ref[...], preferred_element_type=F32)) + rb_ref[...]
    lane = lax.broadcasted_iota(jnp.int32, lg.shape, 1)
    big = jnp.int32(LANES)
    is_grp = (lane >= N_EXPERTS) & (lane < N_EXPERTS + N_EXPERT_GROUPS)
    gl = jnp.where(is_grp, lg, NEG_BIG)
    gexp = jnp.where(is_grp, jnp.exp(gl - jnp.max(gl, axis=-1, keepdims=True)), 0.0)
    gprob = gexp / jnp.sum(gexp, axis=-1, keepdims=True)
    g_w = jnp.max(gprob, axis=-1, keepdims=True)
    g_idx = jnp.min(jnp.where(is_grp & (gprob == g_w), lane - N_EXPERTS, big), axis=-1, keepdims=True)
    valid = (lane < N_EXPERTS) & ((lane // EXPERTS_PER_GROUP) == g_idx)
    el = jnp.where(valid, lg, NEG_BIG)
    eexp = jnp.where(valid, jnp.exp(el - jnp.max(el, axis=-1, keepdims=True)), 0.0)
    eprob = eexp / jnp.sum(eexp, axis=-1, keepdims=True)
    w1 = jnp.max(jnp.where(valid, eprob, -1.0), axis=-1, keepdims=True)
    i1 = jnp.min(jnp.where(valid & (eprob == w1), lane, big), axis=-1, keepdims=True)
    rest = valid & (lane != i1)
    w2 = jnp.max(jnp.where(rest, eprob, -1.0), axis=-1, keepdims=True)
    i2 = jnp.min(jnp.where(rest & (eprob == w2), lane, big), axis=-1, keepdims=True)
    den = w1 + w2
    gates_ref[...] = (jnp.where(lane == i1, g_w * (w1 / den), 0.0)
                      + jnp.where(lane == i2, g_w * (w2 / den), 0.0))


def _mix_router(att, ssm, x2d, wa, wb, g, b, rhi, rlo, rb, *, tm, alpha):
    T, D = x2d.shape
    Wh = att.shape[1]
    row = lambda i: (i, 0)
    const = lambda i: (0, 0)
    return pl.pallas_call(
        functools.partial(_mix_router_kernel, alpha=alpha),
        grid=(T // tm,),
        in_specs=[pl.BlockSpec((tm, Wh), row), pl.BlockSpec((tm, Wh), row), pl.BlockSpec((tm, D), row),
                  pl.BlockSpec(wa.shape, const), pl.BlockSpec(wb.shape, const),
                  pl.BlockSpec((1, D), const), pl.BlockSpec((1, D), const),
                  pl.BlockSpec(rhi.shape, const), pl.BlockSpec(rlo.shape, const), pl.BlockSpec((1, LANES), const)],
        out_specs=[pl.BlockSpec((tm, D), row), pl.BlockSpec((tm, LANES), row)],
        out_shape=[jax.ShapeDtypeStruct((T, D), F32), jax.ShapeDtypeStruct((T, LANES), F32)],
        compiler_params=_cparams("parallel"),
        name="out_proj_ln_router",
    )(att, ssm, x2d, wa, wb, g, b, rhi, rlo, rb)


def _moe_kernel(x1_ref, gates_ref, wg_ref, wu_ref, wd_ref, g_ref, b_ref, o_ref, acc_ref, *, alpha):
    x1 = x1_ref[...]
    xb = x1.astype(BF16)
    gates = gates_ref[...]
    for e in range(wg_ref.shape[0]):
        hg = jnp.dot(xb, wg_ref[e], preferred_element_type=F32)
        hu = jnp.dot(xb, wu_ref[e], preferred_element_type=F32)
        h = hg * (1.0 / (1.0 + jnp.exp(-hg))) * hu * gates[:, e:e + 1]
        contrib = jnp.dot(h.astype(BF16), wd_ref[e], preferred_element_type=F32)
        if e == 0:
            acc_ref[...] = contrib
        else:
            acc_ref[...] += contrib
    o_ref[...] = _layer_norm(alpha * x1 + acc_ref[...], g_ref[...], b_ref[...])


def _moe(x1, gates, wg, wu, wd, g, b, *, tm, alpha):
    T, D = x1.shape
    row = lambda i: (i, 0)
    const2 = lambda i: (0, 0)
    const3 = lambda i: (0, 0, 0)
    once = pl.Buffered(1)
    return pl.pallas_call(
        functools.partial(_moe_kernel, alpha=alpha),
        grid=(T // tm,),
        in_specs=[pl.BlockSpec((tm, D), row), pl.BlockSpec((tm, LANES), row),
                  pl.BlockSpec(wg.shape, const3, pipeline_mode=once),
                  pl.BlockSpec(wu.shape, const3, pipeline_mode=once),
                  pl.BlockSpec(wd.shape, const3, pipeline_mode=once),
                  pl.BlockSpec((1, D), const2), pl.BlockSpec((1, D), const2)],
        out_specs=pl.BlockSpec((tm, D), row),
        out_shape=jax.ShapeDtypeStruct((T, D), F32),
        scratch_shapes=[pltpu.VMEM((tm, D), F32)],
        compiler_params=_cparams("parallel"),
        name="moe_ln",
    )(x1, gates, wg, wu, wd, g, b)


def _rope_tables(pos):
    half = HEAD_DIM // 2
    inv = 1.0 / (ROPE_THETA ** (jnp.arange(half, dtype=F32) / half))
    ang = pos.astype(F32)[:, None] * inv[None, :]
    cos = jnp.cos(ang)
    sin = jnp.sin(ang)
    cos_t = jnp.tile(cos, (1, LANES // half))
    sin_t = jnp.tile(jnp.concatenate([-sin, sin], axis=1), (1, LANES // HEAD_DIM))
    return cos_t, sin_t


def _row_tile(n, pref):
    t = min(n, pref)
    while n % t:
        t //= 2
    return t


def kernel(x_prompt, x_sample, cache_k, cache_v, state_ssm_re, state_ssm_im, w_in, lam_q1, lam_k1, lam_q2, lam_k2, subln_g, ssm_a_re, ssm_a_im, ssm_log_dt, ssm_b_re, ssm_b_im, ssm_c_re, ssm_c_im, ssm_d, w_glu, ssm_norm_g, w_out, ln1_g, ln1_b, w_grp, b_grp, w_rt, b_rt, w_gate, w_up, w_down, ln2_g, ln2_b):
    depth = w_in.shape[0]
    assert depth == 1, "single-layer step"
    B, L, D = x_prompt.shape
    Bs, S, _ = x_sample.shape
    P = cache_k.shape[2]
    H = cache_k.shape[3]
    n_qk = H * 2 * HEAD_DIM
    G, N = ssm_a_re.shape[1], ssm_a_re.shape[2]
    alpha = (2.0 * depth) ** 0.25
    lam_init = 0.8 - 0.6 * math.exp(-0.3 * 0)
    out_scale = 1.0 - lam_init
    l = 0

    w_in_bf = w_in[l].astype(BF16)
    lam = (jnp.exp(jnp.sum(lam_q1[l].astype(F32) * lam_k1[l].astype(F32)))
           - jnp.exp(jnp.sum(lam_q2[l].astype(F32) * lam_k2[l].astype(F32))) + lam_init).reshape(1).astype(F32)
    g_sub = subln_g[l].astype(F32).reshape(1, V_DIM)
    w_glu_bf = w_glu[l].astype(BF16)
    ssm_w = D - H * V_DIM
    d_row = ssm_d[l].astype(F32).reshape(1, ssm_w)
    ng_row = ssm_norm_g[l].astype(F32).reshape(1, ssm_w)
    wa = w_out[l, :H * V_DIM].astype(BF16)
    wb = w_out[l, H * V_DIM:].astype(BF16)
    ln1g, ln1b = ln1_g[l].astype(F32).reshape(1, D), ln1_b[l].astype(F32).reshape(1, D)
    ln2g, ln2b = ln2_g[l].astype(F32).reshape(1, D), ln2_b[l].astype(F32).reshape(1, D)
    w_r = jnp.concatenate([w_rt[l].astype(F32), w_grp[l].astype(F32),
                           jnp.zeros((D, LANES - N_EXPERTS - N_EXPERT_GROUPS), F32)], axis=1)
    rhi = w_r.astype(BF16)
    rlo = (w_r - rhi.astype(F32)).astype(BF16)
    rb = jnp.concatenate([b_rt[l].astype(F32), b_grp[l].astype(F32),
                          jnp.zeros((LANES - N_EXPERTS - N_EXPERT_GROUPS,), F32)]).reshape(1, LANES)
    wg, wu, wd = w_gate[l].astype(BF16), w_up[l].astype(BF16), w_down[l].astype(BF16)

    def tail(att, ssm, x2d):
        tm = _row_tile(x2d.shape[0], 512)
        x1, gates = _mix_router(att, ssm, x2d, wa, wb, ln1g, ln1b, rhi, rlo, rb, tm=tm, alpha=alpha)
        return _moe(x1, gates, wg, wu, wd, ln2g, ln2b, tm=tm, alpha=alpha)

    cos_p, sin_p = _rope_tables(jnp.arange(L))
    xp2 = x_prompt.reshape(B * L, D)
    qp, kpf, kpb, vpf, vpb, up = _in_proj(xp2, w_in_bf, cos_p, sin_p, tm=_row_tile(L, 512),
                                          t_attn=_row_tile(L, 256))
    r3 = lambda a: a.reshape(B, L, -1)
    att_p = _attn_prompt(lam, qp, r3(kpb), vpb, g_sub, out_scale=out_scale)
    n_steps_p = _row_tile(L, 256) // SUBLANES
    tabs_p = _s5_tables(ssm_a_re[l], ssm_a_im[l], ssm_log_dt[l], ssm_b_re[l].astype(F32), ssm_b_im[l].astype(F32),
                        ssm_c_re[l], ssm_c_im[l], n_steps_p)
    ssm_p, hre_p, him_p = _s5_mixer(r3(up), tabs_p, d_row, w_glu_bf, ng_row, n_steps=n_steps_p)
    y_prompt = tail(att_p.reshape(B * L, -1), ssm_p.reshape(B * L, -1), xp2).reshape(B, L, D)

    cos_s, sin_s = _rope_tables(P + jnp.arange(S))
    cos_s, sin_s = jnp.tile(cos_s, (Bs, 1)), jnp.tile(sin_s, (Bs, 1))
    xs2 = x_sample.reshape(Bs * S, D)
    qs, ksf, ksb, vsf, vsb, us = _in_proj(xs2, w_in_bf, cos_s, sin_s, tm=_row_tile(Bs * S, 512))
    s3 = lambda a: a.reshape(Bs, S, -1)
    att_s = _attn_sample(lam, s3(qs), cache_k[l].reshape(Bs, P, n_qk), cache_v[l].reshape(Bs, P, H * V_DIM),
                         s3(ksb), s3(vsb), g_sub, out_scale=out_scale)
    tabs_s = _s5_tables(ssm_a_re[l], ssm_a_im[l], ssm_log_dt[l], ssm_b_re[l].astype(F32), ssm_b_im[l].astype(F32),
                        ssm_c_re[l], ssm_c_im[l], S)
    h0 = (state_ssm_re[l].astype(F32).reshape(Bs, G * N), state_ssm_im[l].astype(F32).reshape(Bs, G * N))
    ssm_s, hre_s, him_s = _s5_mixer(s3(us), tabs_s, d_row, w_glu_bf, ng_row, n_steps=S, h0=h0)
    y_sample = tail(att_s.reshape(Bs * S, -1), ssm_s.reshape(Bs * S, -1), xs2).reshape(Bs, S, D)

    return (y_prompt, y_sample,
            kpf.reshape(1, B, L, H, 2, HEAD_DIM), vpf.reshape(1, B, L, H, V_DIM),
            hre_p.reshape(1, B, G, N), him_p.reshape(1, B, G, N),
            ksf.reshape(1, Bs, S, H, 2, HEAD_DIM), vsf.reshape(1, Bs, S, H, V_DIM),
            hre_s.reshape(1, Bs, G, N), him_s.reshape(1, Bs, G, N))
```

```python
import functools
import math

import jax
import jax.numpy as jnp
from jax import lax
from jax.experimental import pallas as pl
from jax.experimental.pallas import tpu as pltpu

F32 = jnp.float32
BF16 = jnp.bfloat16

HEAD_DIM = 64
V_DIM = 2 * HEAD_DIM
CHUNK = 64
SSM_GROUP = 16
SSM_STATE = 64
N_EXPERT_GROUPS = 4
EXPERTS_PER_GROUP = 4
N_EXPERTS = N_EXPERT_GROUPS * EXPERTS_PER_GROUP
ROPE_THETA = 10000.0
LN_EPS = 1e-5
RMS_EPS = 1e-6
LANES = 128
SUBLANES = 8
MXU_N = 256
NEG_BIG = -1e30
SUM_ROWS = 16
LOG2E = 1.4426950408889634
VMEM_LIMIT = 52 * 1024 * 1024


def _cparams(*sem):
    return pltpu.CompilerParams(dimension_semantics=sem, vmem_limit_bytes=VMEM_LIMIT)


def _store_transposed_tiles(ref, a):
    n_t, n_h, _, t = ref.shape
    at = a.T
    for n in range(n_t):
        for h in range(n_h):
            ref[n, h] = at[V_DIM * h:V_DIM * (h + 1), t * n:t * (n + 1)].astype(ref.dtype)


def _inproj_kernel(x_ref, w_ref, cos_ref, sin_ref, q_ref, kf_ref, kb_ref, vf_ref, vb_ref, u_ref,
                   *, q_scale, n_qk, transposed):
    xb = x_ref[...].astype(BF16)
    cos = cos_ref[...]
    sin = sin_ref[...]
    lane = lax.broadcasted_iota(jnp.int32, cos.shape, 1)
    first = (lane & (HEAD_DIM - 1)) < HEAD_DIM // 2

    def rope(t):
        rot = jnp.where(first, pltpu.roll(t, LANES - HEAD_DIM // 2, 1), pltpu.roll(t, HEAD_DIM // 2, 1))
        return t * cos + rot * sin

    pq = jnp.dot(xb, w_ref[:, 0:n_qk], preferred_element_type=F32)
    qr = jnp.concatenate([rope(pq[:, LANES * j:LANES * (j + 1)]) for j in range(n_qk // LANES)], axis=1) * q_scale
    if transposed:
        _store_transposed_tiles(q_ref, qr)
    else:
        q_ref[...] = qr.astype(BF16)
    pk = jnp.dot(xb, w_ref[:, n_qk:2 * n_qk], preferred_element_type=F32)
    for j in range(n_qk // LANES):
        sl = slice(LANES * j, LANES * (j + 1))
        kr = rope(pk[:, sl])
        kf_ref[:, sl] = kr
        kb_ref[:, sl] = kr.astype(BF16)
    n_v = vf_ref.shape[1]
    pv = jnp.dot(xb, w_ref[:, 2 * n_qk:2 * n_qk + n_v], preferred_element_type=F32)
    vf_ref[...] = pv
    if transposed:
        _store_transposed_tiles(vb_ref, pv)
    else:
        vb_ref[...] = pv.astype(BF16)
    u_ref[...] = jnp.dot(xb, w_ref[:, 2 * n_qk + n_v:], preferred_element_type=F32)


def _in_proj(x2d, w_bf, cos_t, sin_t, *, tm, t_attn=None):
    T, D = x2d.shape
    n_cols = w_bf.shape[1]
    n_qk = n_cols // 4
    n_tab = cos_t.shape[0] // tm
    row = lambda i: (i, 0)
    tab = lambda i: (i % n_tab, 0)
    out_sds = lambda dt: jax.ShapeDtypeStruct((T, n_qk), dt)
    spec = pl.BlockSpec((tm, n_qk), row)
    qv_spec, qv_sds = spec, out_sds(BF16)
    if t_attn is not None:
        H = n_qk // V_DIM
        qv_spec = pl.BlockSpec((tm // t_attn, H, V_DIM, t_attn), lambda i: (i, 0, 0, 0))
        qv_sds = jax.ShapeDtypeStruct((T // t_attn, H, V_DIM, t_attn), BF16)
    return pl.pallas_call(
        functools.partial(_inproj_kernel, q_scale=HEAD_DIM ** -0.5 * LOG2E, n_qk=n_qk,
                          transposed=t_attn is not None),
        grid=(T // tm,),
        in_specs=[pl.BlockSpec((tm, D), row),
                  pl.BlockSpec((D, n_cols), lambda i: (0, 0)),
                  pl.BlockSpec((tm, LANES), tab),
                  pl.BlockSpec((tm, LANES), tab)],
        out_specs=[qv_spec, spec, spec, spec, qv_spec, spec],
        out_shape=[qv_sds, out_sds(F32), out_sds(BF16), out_sds(F32), qv_sds, out_sds(F32)],
        compiler_params=_cparams("parallel"),
        name="in_proj_rope",
    )(x2d, w_bf, cos_t, sin_t)


def _split_maps(q):
    lane = lax.broadcasted_iota(jnp.int32, q.shape, 1)
    zero = jnp.zeros_like(q)
    return jnp.concatenate([jnp.where(lane < HEAD_DIM, q, zero), jnp.where(lane >= HEAD_DIM, q, zero)], axis=0)


def _qk(qs, kb):
    return lax.dot_general(qs, kb, (((1,), (1,)), ((), ())), preferred_element_type=F32)


def _subln(o, g, out_scale):
    ms = jnp.mean(o * o, axis=-1, keepdims=True)
    return o * lax.rsqrt(ms + RMS_EPS) * g * out_scale


def _attn_prompt_kernel(lam_ref, qt_ref, k_ref, vt_ref, g_ref, o_ref, qs_ref, acc_ref, s0_ref, s1_ref, m_ref,
                        *, tq, out_scale):
    i = pl.program_id(2)
    s_buf = (s0_ref, s1_ref)
    qt = qt_ref[...]
    row = lax.broadcasted_iota(jnp.int32, qt.shape, 0)
    zero = jnp.zeros_like(qt)
    qs_ref[:, 0:tq] = jnp.where(row < HEAD_DIM, qt, zero)
    qs_ref[:, tq:2 * tq] = jnp.where(row >= HEAD_DIM, qt, zero)
    acc_ref[...] = jnp.zeros(acc_ref.shape, F32)
    m_ref[...] = jnp.full(m_ref.shape, NEG_BIG, F32)
    ones_rows = jnp.ones((SUM_ROWS, tq), BF16)

    def scores(kt, slot):
        kb = k_ref[pl.ds(pl.multiple_of(kt * tq, tq), tq), :]
        s_buf[slot][...] = jnp.dot(kb, qs_ref[...], preferred_element_type=F32)

    def softmax_pv(kt, slot, masked):
        lhs = jnp.concatenate([vt_ref[kt], ones_rows], axis=0)
        for cb in range(2 * tq // MXU_N):
            cs = slice(MXU_N * cb, MXU_N * (cb + 1))
            st = s_buf[slot][:, cs]
            if masked:
                c = lax.broadcasted_iota(jnp.int32, st.shape, 0)
                r = (lax.broadcasted_iota(jnp.int32, st.shape, 1) + MXU_N * cb) % tq
                st = jnp.where((c // CHUNK) <= (r // CHUNK), st, NEG_BIG)
            m_prev = m_ref[:, cs]
            m_new = jnp.maximum(m_prev, jnp.max(st, axis=0, keepdims=True))
            alpha = jnp.exp2(m_prev - m_new)
            pt = jnp.exp2(st - m_new).astype(BF16)
            m_ref[:, cs] = m_new
            acc_ref[:, cs] = alpha * acc_ref[:, cs] + jnp.dot(lhs, pt, preferred_element_type=F32)

    def step(t, par):
        scores(t, par)
        softmax_pv(t - 1, 1 - par, False)

    scores(0, 0)

    def pair(j, carry):
        step(2 * j + 1, 1)
        step(2 * j + 2, 0)
        return carry

    lax.fori_loop(0, i // 2, pair, 0)

    @pl.when(i % 2 == 0)
    def _():
        softmax_pv(i, 0, True)

    @pl.when(i % 2 == 1)
    def _():
        step(i, 1)
        softmax_pv(i, 1, True)

    on = acc_ref[0:V_DIM, :] * (1.0 / acc_ref[V_DIM:V_DIM + 1, :])
    ot = on[:, 0:tq] - lam_ref[0] * on[:, tq:2 * tq]
    o_ref[...] = _subln(ot.T, g_ref[...], out_scale).astype(o_ref.dtype)


def _attn_prompt(lam, qt, k, vt, g, *, out_scale):
    B, L, W = k.shape
    n_t, H, _, t = qt.shape
    nq = L // t
    return pl.pallas_call(
        functools.partial(_attn_prompt_kernel, tq=t, out_scale=out_scale),
        grid=(B, H, nq),
        in_specs=[pl.BlockSpec(memory_space=pltpu.SMEM),
                  pl.BlockSpec((None, None, V_DIM, t), lambda b, h, i: (b * nq + i, h, 0, 0)),
                  pl.BlockSpec((None, L, V_DIM), lambda b, h, i: (b, 0, h)),
                  pl.BlockSpec((nq, None, V_DIM, t), lambda b, h, i: (b, h, 0, 0)),
                  pl.BlockSpec((1, V_DIM), lambda b, h, i: (0, 0))],
        out_specs=pl.BlockSpec((None, t, V_DIM), lambda b, h, i: (b, i, h)),
        out_shape=jax.ShapeDtypeStruct((B, L, W), BF16),
        scratch_shapes=[pltpu.VMEM((V_DIM, 2 * t), BF16), pltpu.VMEM((V_DIM + SUM_ROWS, 2 * t), F32),
                        pltpu.VMEM((t, 2 * t), F32), pltpu.VMEM((t, 2 * t), F32),
                        pltpu.VMEM((1, 2 * t), F32)],
        compiler_params=_cparams("parallel", "parallel", "parallel"),
        name="diff_attn_prompt",
    )(lam, qt, k, vt, g)


def _attn_sample_kernel(lam_ref, q_ref, kc_ref, vc_ref, kn_ref, vn_ref, g_ref, o_ref, *, past_len, out_scale):
    tq = q_ref.shape[0]
    qs = _split_maps(q_ref[...])
    s_c = _qk(qs, kc_ref[...].astype(BF16))
    s_n = _qk(qs, kn_ref[...])
    r = lax.broadcasted_iota(jnp.int32, s_n.shape, 0)
    c = lax.broadcasted_iota(jnp.int32, s_n.shape, 1)
    r = jnp.where(r >= tq, r - tq, r)
    s_n = jnp.where(((past_len + c) // CHUNK) <= ((past_len + r) // CHUNK), s_n, NEG_BIG)
    m = jnp.maximum(jnp.max(s_c, axis=-1, keepdims=True), jnp.max(s_n, axis=-1, keepdims=True))
    p_c = jnp.exp2(s_c - m)
    p_n = jnp.exp2(s_n - m)
    inv_l = 1.0 / (jnp.sum(p_c, axis=-1, keepdims=True) + jnp.sum(p_n, axis=-1, keepdims=True))
    lam = lam_ref[0]
    p_c = p_c * inv_l
    p_n = p_n * inv_l
    w_c = p_c[0:tq] - lam * p_c[tq:2 * tq]
    w_n = p_n[0:tq] - lam * p_n[tq:2 * tq]
    o = (jnp.dot(w_c.astype(BF16), vc_ref[...].astype(BF16), preferred_element_type=F32)
         + jnp.dot(w_n.astype(BF16), vn_ref[...], preferred_element_type=F32))
    o_ref[...] = _subln(o, g_ref[...], out_scale).astype(o_ref.dtype)


def _attn_sample(lam, q, cache_k, cache_v, k_new, v_new, g, *, out_scale):
    B, S, W = q.shape
    P = cache_k.shape[1]
    H = W // V_DIM
    blk_new = pl.BlockSpec((None, S, V_DIM), lambda b, h: (b, 0, h))
    blk_cache = pl.BlockSpec((None, P, V_DIM), lambda b, h: (b, 0, h))
    return pl.pallas_call(
        functools.partial(_attn_sample_kernel, past_len=P, out_scale=out_scale),
        grid=(B, H),
        in_specs=[pl.BlockSpec(memory_space=pltpu.SMEM), blk_new, blk_cache, blk_cache, blk_new, blk_new,
                  pl.BlockSpec((1, V_DIM), lambda b, h: (0, 0))],
        out_specs=blk_new,
        out_shape=jax.ShapeDtypeStruct((B, S, W), BF16),
        compiler_params=_cparams("parallel", "parallel"),
        name="diff_attn_sample",
    )(lam, q, cache_k, cache_v, k_new, v_new, g)


def _cmul(ar, ai, br, bi):
    return ar * br - ai * bi, ar * bi + ai * br


def _s5_kernel(*refs, n_steps, chained, lane_chunk):
    if chained:
        (u_ref, perm_ref, permt_ref, wx_ref, cb_ref, d_ref, pre_ref, pim_ref, gre_ref, gim_ref,
         wglu_ref, ng_ref, y_ref, hre_ref, him_ref, xre, xim, cre, cim) = refs
    else:
        (u_ref, perm_ref, permt_ref, wx_ref, cb_ref, d_ref, pre_ref, pim_ref, gre_ref, gim_ref,
         wglu_ref, ng_ref, h0re_ref, h0im_ref, y_ref, hre_ref, him_ref, xre, xim) = refs
    R = n_steps
    n_state = xre.shape[1]
    n_blk = wx_ref.shape[0]
    blk_in = wx_ref.shape[1]
    blk_st = wx_ref.shape[2] // 2

    u = u_ref[...]
    u_hi = u.astype(BF16)
    u_lo = (u - u_hi.astype(F32)).astype(BF16)
    perm = perm_ref[...]
    up = (jnp.dot(perm, u_hi, preferred_element_type=F32) + jnp.dot(perm, u_lo, preferred_element_type=F32))
    upb = up.astype(BF16)
    for j in range(n_blk):
        xj = jnp.dot(upb[:, blk_in * j:blk_in * (j + 1)], wx_ref[j], preferred_element_type=F32)
        xre[:, blk_st * j:blk_st * (j + 1)] = xj[:, :blk_st]
        xim[:, blk_st * j:blk_st * (j + 1)] = xj[:, blk_st:]

    n_chunks = n_state // lane_chunk
    e_re, e_im = [], []
    for c in range(n_chunks):
        cs = slice(lane_chunk * c, lane_chunk * (c + 1))
        ar = jnp.broadcast_to(pre_ref[0:1, cs], (SUBLANES, lane_chunk))
        ai = jnp.broadcast_to(pim_ref[0:1, cs], (SUBLANES, lane_chunk))

        def scan_body(r, h, cs=cs, ar=ar, ai=ai):
            hr, hi = h
            rows = pl.ds(pl.multiple_of(r * SUBLANES, SUBLANES), SUBLANES)
            nr = ar * hr - ai * hi + xre[rows, cs]
            ni = ar * hi + ai * hr + xim[rows, cs]
            xre[rows, cs] = nr
            xim[rows, cs] = ni
            return nr, ni

        z = jnp.zeros((SUBLANES, lane_chunk), F32)
        er, ei = lax.fori_loop(0, R, scan_body, (z, z), unroll=4)
        e_re.append(er)
        e_im.append(ei)
    e_re = jnp.concatenate(e_re, axis=1)
    e_im = jnp.concatenate(e_im, axis=1)

    g1r = jnp.broadcast_to(gre_ref[0:1, :], e_re.shape)
    g1i = jnp.broadcast_to(gim_ref[0:1, :], e_re.shape)
    if chained:
        t = pl.program_id(1)

        @pl.when(t == 0)
        def _():
            cre[...] = jnp.zeros(cre.shape, F32)
            cim[...] = jnp.zeros(cim.shape, F32)

        sub = lax.broadcasted_iota(jnp.int32, e_re.shape, 0)
        c_r = cre[...]
        c_i = cim[...]
        inj_r, inj_i = _cmul(g1r, g1i, c_r, c_i)
        xr = e_re + jnp.where(sub == 0, inj_r, 0.0)
        xi = e_im + jnp.where(sub == 0, inj_i, 0.0)
        for lvl, k in enumerate((1, 2, 4)):
            gr = jnp.broadcast_to(gre_ref[lvl:lvl + 1, :], e_re.shape)
            gi = jnp.broadcast_to(gim_ref[lvl:lvl + 1, :], e_re.shape)
            sr = jnp.where(sub >= k, pltpu.roll(xr, k, 0), 0.0)
            si = jnp.where(sub >= k, pltpu.roll(xi, k, 0), 0.0)
            tr, ti = _cmul(gr, gi, sr, si)
            xr = xr + tr
            xi = xi + ti
        hend_r, hend_i = xr, xi
        hc_r = jnp.where(sub == 0, c_r, pltpu.roll(hend_r, 1, 0))
        hc_i = jnp.where(sub == 0, c_i, pltpu.roll(hend_i, 1, 0))
        last_r = jnp.broadcast_to(hend_r[SUBLANES - 1:SUBLANES, :], e_re.shape)
        last_i = jnp.broadcast_to(hend_i[SUBLANES - 1:SUBLANES, :], e_re.shape)
        cre[...] = last_r
        cim[...] = last_i
        hre_ref[...] = last_r[0:1, :]
        him_ref[...] = last_i[0:1, :]
    else:
        hc_r = h0re_ref[...]
        hc_i = h0im_ref[...]
        tr, ti = _cmul(g1r, g1i, hc_r, hc_i)
        hre_ref[...] = e_re + tr
        him_ref[...] = e_im + ti

    for c in range(n_chunks):
        cs = slice(lane_chunk * c, lane_chunk * (c + 1))
        hcr = hc_r[:, cs]
        hci = hc_i[:, cs]

        def fix_body(r, carry, cs=cs, hcr=hcr, hci=hci):
            rows = pl.ds(pl.multiple_of(r * SUBLANES, SUBLANES), SUBLANES)
            pr = jnp.broadcast_to(pre_ref[pl.ds(r, 1), cs], hcr.shape)
            pi = jnp.broadcast_to(pim_ref[pl.ds(r, 1), cs], hcr.shape)
            tr, ti = _cmul(pr, pi, hcr, hci)
            xre[rows, cs] = xre[rows, cs] + tr
            xim[rows, cs] = xim[rows, cs] + ti
            return carry

        lax.fori_loop(0, R, fix_body, 0, unroll=4)

    ys = []
    for j in range(n_blk):
        st = slice(blk_st * j, blk_st * (j + 1))
        ys.append(jnp.dot(xre[:, st].astype(BF16), cb_ref[j, 0:blk_st, :], preferred_element_type=F32)
                  + jnp.dot(xim[:, st].astype(BF16), cb_ref[j, blk_st:, :], preferred_element_type=F32))
    y = jnp.concatenate(ys, axis=1) + d_ref[...] * up
    y = 0.5 * y * (1.0 + jnp.tanh(math.sqrt(2.0 / math.pi) * (y + 0.044715 * (y * y * y))))
    gl = jnp.dot(y.astype(BF16), wglu_ref[...], preferred_element_type=F32)
    half = gl.shape[1] // 2
    z = gl[:, :half] * (1.0 / (1.0 + jnp.exp(-gl[:, half:])))
    z = z * lax.rsqrt(jnp.mean(z * z, axis=-1, keepdims=True) + RMS_EPS) * ng_ref[...]
    y_ref[...] = jnp.dot(permt_ref[...], z.astype(BF16), preferred_element_type=F32).astype(y_ref.dtype)


def _s5_tables(a_re, a_im, log_dt, b_re, b_im, c_re, c_im, n_steps):
    G, N = a_re.shape
    C = b_re.shape[2]
    dt = jnp.exp(log_dt.astype(F32))[:, None]
    a_re = a_re.astype(F32)
    a_im = a_im.astype(F32)

    def power(k):
        mag = jnp.exp(k * a_re * dt)
        return mag * jnp.cos(k * a_im * dt), mag * jnp.sin(k * a_im * dt)

    ab_re, ab_im = power(1.0)
    nr, ni = ab_re - 1.0, ab_im
    den = a_re * a_re + a_im * a_im
    cf_re = (nr * a_re + ni * a_im) / den
    cf_im = (ni * a_re - nr * a_im) / den
    ks = jnp.arange(1, n_steps + 1, dtype=F32)[:, None, None]
    p_re, p_im = power(ks)
    g_re, g_im = zip(*[power(float(m * n_steps)) for m in (1, 2, 4)])
    pad = jnp.zeros((SUBLANES - 3, G * N), F32)
    g_re = jnp.concatenate([jnp.stack(g_re).reshape(3, G * N), pad])
    g_im = jnp.concatenate([jnp.stack(g_im).reshape(3, G * N), pad])
    wx_re = cf_re[..., None] * b_re - cf_im[..., None] * b_im
    wx_im = cf_re[..., None] * b_im + cf_im[..., None] * b_re
    gpb = LANES // C
    nb = G // gpb
    eye = jnp.eye(gpb, dtype=F32)

    def in_block(w):
        w = w.reshape(nb, gpb, N, C)
        return jnp.einsum('jgnc,gh->jgchn', w, eye).reshape(nb, gpb * C, gpb * N)

    def out_block(w):
        w = w.reshape(nb, gpb, C, N)
        return jnp.einsum('jgcn,gh->jgnhc', w, eye).reshape(nb, gpb * N, gpb * C)

    wx = jnp.concatenate([in_block(wx_re), in_block(wx_im)], axis=2).astype(BF16)
    cb = jnp.concatenate([out_block(c_re.astype(F32)), out_block(-c_im.astype(F32))], axis=1).astype(BF16)
    return (p_re.reshape(n_steps, G * N), p_im.reshape(n_steps, G * N), g_re, g_im, wx, cb)


def _perm_matrices(n_steps):
    tt = SUBLANES * n_steps
    dst = jnp.arange(tt)
    src = (dst % SUBLANES) * n_steps + dst // SUBLANES
    perm = (src[:, None] == jnp.arange(tt)[None, :]).astype(BF16)
    return perm, perm.T


def _s5_mixer(u, tables, d, w_glu_bf, norm_g, *, n_steps, h0=None):
    B, L, W = u.shape
    p_re, p_im, g_re, g_im, wx, cb = tables
    n_state = p_re.shape[1]
    tt = SUBLANES * n_steps
    perm, permt = _perm_matrices(n_steps)
    chained = h0 is None
    const2 = lambda *_: (0, 0)
    const3 = lambda *_: (0, 0, 0)
    common_specs = [pl.BlockSpec(perm.shape, const2), pl.BlockSpec(perm.shape, const2),
                    pl.BlockSpec(wx.shape, const3), pl.BlockSpec(cb.shape, const3),
                    pl.BlockSpec((1, W), const2),
                    pl.BlockSpec(p_re.shape, const2), pl.BlockSpec(p_im.shape, const2),
                    pl.BlockSpec(g_re.shape, const2), pl.BlockSpec(g_im.shape, const2),
                    pl.BlockSpec(w_glu_bf.shape, const2), pl.BlockSpec((1, W), const2)]
    common_args = (perm, permt, wx, cb, d, p_re, p_im, g_re, g_im, w_glu_bf, norm_g)
    state_scratch = [pltpu.VMEM((tt, n_state), F32), pltpu.VMEM((tt, n_state), F32)]
    kern = functools.partial(_s5_kernel, n_steps=n_steps, chained=chained, lane_chunk=512)
    if chained:
        y, hre, him = pl.pallas_call(
            kern,
            grid=(B, L // tt),
            in_specs=[pl.BlockSpec((None, tt, W), lambda b, t: (b, t, 0))] + common_specs,
            out_specs=[pl.BlockSpec((None, tt, W), lambda b, t: (b, t, 0)),
                       pl.BlockSpec((None, 1, n_state), lambda b, t: (b, 0, 0)),
                       pl.BlockSpec((None, 1, n_state), lambda b, t: (b, 0, 0))],
            out_shape=[jax.ShapeDtypeStruct((B, L, W), BF16),
                       jax.ShapeDtypeStruct((B, 1, n_state), F32),
                       jax.ShapeDtypeStruct((B, 1, n_state), F32)],
            scratch_shapes=state_scratch + [pltpu.VMEM((SUBLANES, n_state), F32)] * 2,
            compiler_params=_cparams("parallel", "arbitrary"),
            name="s5_mixer_chained",
        )(u, *common_args)
        return y, hre.reshape(B, n_state), him.reshape(B, n_state)
    u2 = u.reshape(B * L, W)
    h0re, h0im = h0
    y, hre, him = pl.pallas_call(
        kern,
        grid=(B // SUBLANES,),
        in_specs=[pl.BlockSpec((tt, W), lambda i: (i, 0))] + common_specs
        + [pl.BlockSpec((SUBLANES, n_state), lambda i: (i, 0))] * 2,
        out_specs=[pl.BlockSpec((tt, W), lambda i: (i, 0)),
                   pl.BlockSpec((SUBLANES, n_state), lambda i: (i, 0)),
                   pl.BlockSpec((SUBLANES, n_state), lambda i: (i, 0))],
        out_shape=[jax.ShapeDtypeStruct((B * L, W), BF16),
                   jax.ShapeDtypeStruct((B, n_state), F32),
                   jax.ShapeDtypeStruct((B, n_state), F32)],
        scratch_shapes=state_scratch,
        compiler_params=_cparams("parallel"),
        name="s5_mixer_batched",
    )(u2, *common_args, h0re, h0im)
    return y.reshape(B, L, W), hre, him


def _layer_norm(z, g, b):
    mu = jnp.mean(z, axis=-1, keepdims=True)
    zc = z - mu
    var = jnp.mean(zc * zc, axis=-1, keepdims=True)
    return zc * lax.rsqrt(var + LN_EPS) * g + b


def _mix_router_kernel(att_ref, ssm_ref, x_ref, wa_ref, wb_ref, g_ref, b_ref, rhi_ref, rlo_ref, rb_ref,
                       x1_ref, gates_ref, *, alpha):
    mix = (jnp.dot(att_ref[...], wa_ref[...], preferred_element_type=F32)
           + jnp.dot(ssm_ref[...], wb_ref[...], preferred_element_type=F32))
    x1 = _layer_norm(alpha * x_ref[...] + mix, g_ref[...], b_ref[...])
    x1_ref[...] = x1
    hi = x1.astype(BF16)
    lo = (x1 - hi.astype(F32)).astype(BF16)
    lg = (jnp.dot(hi, rhi_ref[...], preferred_element_type=F32)
          + jnp.dot(lo, rhi_ref[...], preferred_element_type=F32)
          + jnp.dot(hi, rlo_ref[...], preferred_element_type=F32)) + rb_ref[...]
    lane = lax.broadcasted_iota(jnp.int32, lg.shape, 1)
    big = jnp.int32(LANES)
    is_grp = (lane >= N_EXPERTS) & (lane < N_EXPERTS + N_EXPERT_GROUPS)
    gl = jnp.where(is_grp, lg, NEG_BIG)
    gexp = jnp.where(is_grp, jnp.exp(gl - jnp.max(gl, axis=-1, keepdims=True)), 0.0)
    gprob = gexp / jnp.sum(gexp, axis=-1, keepdims=True)
    g_w = jnp.max(gprob, axis=-1, keepdims=True)
    g_idx = jnp.min(jnp.where(is_grp & (gprob == g_w), lane - N_EXPERTS, big), axis=-1, keepdims=True)
    valid = (lane < N_EXPERTS) & ((lane // EXPERTS_PER_GROUP) == g_idx)
    el = jnp.where(valid, lg, NEG_BIG)
    eexp = jnp.where(valid, jnp.exp(el - jnp.max(el, axis=-1, keepdims=True)), 0.0)
    eprob = eexp / jnp.sum(eexp, axis=-1, keepdims=True)
    w1 = jnp.max(jnp.where(valid, eprob, -1.0), axis=-1, keepdims=True)
    i1 = jnp.min(jnp.where(valid & (eprob == w1), lane, big), axis=-1, keepdims=True)
    rest = valid & (lane != i1)
    w2 = jnp.max(jnp.where(rest, eprob, -1.0), axis=-1, keepdims=True)
    i2 = jnp.min(jnp.where(rest & (eprob == w2), lane, big), axis=-1, keepdims=True)
    den = w1 + w2
    gates_ref[...] = (jnp.where(lane == i1, g_w * (w1 / den), 0.0)
                      + jnp.where(lane == i2, g_w * (w2 / den), 0.0))


def _mix_router(att, ssm, x2d, wa, wb, g, b, rhi, rlo, rb, *, tm, alpha):
    T, D = x2d.shape
    Wh = att.shape[1]
    row = lambda i: (i, 0)
    const = lambda i: (0, 0)
    return pl.pallas_call(
        functools.partial(_mix_router_kernel, alpha=alpha),
        grid=(T // tm,),
        in_specs=[pl.BlockSpec((tm, Wh), row), pl.BlockSpec((tm, Wh), row), pl.BlockSpec((tm, D), row),
                  pl.BlockSpec(wa.shape, const), pl.BlockSpec(wb.shape, const),
                  pl.BlockSpec((1, D), const), pl.BlockSpec((1, D), const),
                  pl.BlockSpec(rhi.shape, const), pl.BlockSpec(rlo.shape, const), pl.BlockSpec((1, LANES), const)],
        out_specs=[pl.BlockSpec((tm, D), row), pl.BlockSpec((tm, LANES), row)],
        out_shape=[jax.ShapeDtypeStruct((T, D), F32), jax.ShapeDtypeStruct((T, LANES), F32)],
        compiler_params=_cparams("parallel"),
        name="out_proj_ln_router",
    )(att, ssm, x2d, wa, wb, g, b, rhi, rlo, rb)


def _moe_kernel(x1_ref, gates_ref, wg_ref, wu_ref, wd_ref, g_ref, b_ref, o_ref, acc_ref, *, alpha):
    x1 = x1_ref[...]
    xb = x1.astype(BF16)
    gates = gates_ref[...]
    for e in range(wg_ref.shape[0]):
        hg = jnp.dot(xb, wg_ref[e], preferred_element_type=F32)
        hu = jnp.dot(xb, wu_ref[e], preferred_element_type=F32)
        h = hg * (1.0 / (1.0 + jnp.exp(-hg))) * hu * gates[:, e:e + 1]
        contrib = jnp.dot(h.astype(BF16), wd_ref[e], preferred_element_type=F32)
        if e == 0:
            acc_ref[...] = contrib
        else:
            acc_ref[...] += contrib
    o_ref[...] = _layer_norm(alpha * x1 + acc_ref[...], g_ref[...], b_ref[...])


def _moe(x1, gates, wg, wu, wd, g, b, *, tm, alpha):
    T, D = x1.shape
    row = lambda i: (i, 0)
    const2 = lambda i: (0, 0)
    const3 = lambda i: (0, 0, 0)
    once = pl.Buffered(1)
    return pl.pallas_call(
        functools.partial(_moe_kernel, alpha=alpha),
        grid=(T // tm,),
        in_specs=[pl.BlockSpec((tm, D), row), pl.BlockSpec((tm, LANES), row),
                  pl.BlockSpec(wg.shape, const3, pipeline_mode=once),
                  pl.BlockSpec(wu.shape, const3, pipeline_mode=once),
                  pl.BlockSpec(wd.shape, const3, pipeline_mode=once),
                  pl.BlockSpec((1, D), const2), pl.BlockSpec((1, D), const2)],
        out_specs=pl.BlockSpec((tm, D), row),
        out_shape=jax.ShapeDtypeStruct((T, D), F32),
        scratch_shapes=[pltpu.VMEM((tm, D), F32)],
        compiler_params=_cparams("parallel"),
        name="moe_ln",
    )(x1, gates, wg, wu, wd, g, b)


def _rope_tables(pos):
    half = HEAD_DIM // 2
    inv = 1.0 / (ROPE_THETA ** (jnp.arange(half, dtype=F32) / half))
    ang = pos.astype(F32)[:, None] * inv[None, :]
    cos = jnp.cos(ang)
    sin = jnp.sin(ang)
    cos_t = jnp.tile(cos, (1, LANES // half))
    sin_t = jnp.tile(jnp.concatenate([-sin, sin], axis=1), (1, LANES // HEAD_DIM))
    return cos_t, sin_t


def _row_tile(n, pref):
    t = min(n, pref)
    while n % t:
        t //= 2
    return t


def kernel(x_prompt, x_sample, cache_k, cache_v, state_ssm_re, state_ssm_im, w_in, lam_q1, lam_k1, lam_q2, lam_k2, subln_g, ssm_a_re, ssm_a_im, ssm_log_dt, ssm_b_re, ssm_b_im, ssm_c_re, ssm_c_im, ssm_d, w_glu, ssm_norm_g, w_out, ln1_g, ln1_b, w_grp, b_grp, w_rt, b_rt, w_gate, w_up, w_down, ln2_g, ln2_b):
    depth = w_in.shape[0]
    assert depth == 1, "single-layer step"
    B, L, D = x_prompt.shape
    Bs, S, _ = x_sample.shape
    P = cache_k.shape[2]
    H = cache_k.shape[3]
    n_qk = H * 2 * HEAD_DIM
    G, N = ssm_a_re.shape[1], ssm_a_re.shape[2]
    alpha = (2.0 * depth) ** 0.25
    lam_init = 0.8 - 0.6 * math.exp(-0.3 * 0)
    out_scale = 1.0 - lam_init
    l = 0

    w_in_bf = w_in[l].astype(BF16)
    lam = (jnp.exp(jnp.sum(lam_q1[l].astype(F32) * lam_k1[l].astype(F32)))
           - jnp.exp(jnp.sum(lam_q2[l].astype(F32) * lam_k2[l].astype(F32))) + lam_init).reshape(1).astype(F32)
    g_sub = subln_g[l].astype(F32).reshape(1, V_DIM)
    w_glu_bf = w_glu[l].astype(BF16)
    ssm_w = D - H * V_DIM
    d_row = ssm_d[l].astype(F32).reshape(1, ssm_w)
    ng_row = ssm_norm_g[l].astype(F32).reshape(1, ssm_w)
    wa = w_out[l, :H * V_DIM].astype(BF16)
    wb = w_out[l, H * V_DIM:].astype(BF16)
    ln1g, ln1b = ln1_g[l].astype(F32).reshape(1, D), ln1_b[l].astype(F32).reshape(1, D)
    ln2g, ln2b = ln2_g[l].astype(F32).reshape(1, D), ln2_b[l].astype(F32).reshape(1, D)
    w_r = jnp.concatenate([w_rt[l].astype(F32), w_grp[l].astype(F32),
                           jnp.zeros((D, LANES - N_EXPERTS - N_EXPERT_GROUPS), F32)], axis=1)
    rhi = w_r.astype(BF16)
    rlo = (w_r - rhi.astype(F32)).astype(BF16)
    rb = jnp.concatenate([b_rt[l].astype(F32), b_grp[l].astype(F32),
                          jnp.zeros((LANES - N_EXPERTS - N_EXPERT_GROUPS,), F32)]).reshape(1, LANES)
    wg, wu, wd = w_gate[l].astype(BF16), w_up[l].astype(BF16), w_down[l].astype(BF16)

    def tail(att, ssm, x2d):
        tm = _row_tile(x2d.shape[0], 512)
        x1, gates = _mix_router(att, ssm, x2d, wa, wb, ln1g, ln1b, rhi, rlo, rb, tm=tm, alpha=alpha)
        return _moe(x1, gates, wg, wu, wd, ln2g, ln2b, tm=tm, alpha=alpha)

    cos_p, sin_p = _rope_tables(jnp.arange(L))
    xp2 = x_prompt.reshape(B * L, D)
    qp, kpf, kpb, vpf, vpb, up = _in_proj(xp2, w_in_bf, cos_p, sin_p, tm=_row_tile(L, 512),
                                          t_attn=_row_tile(L, 512))
    r3 = lambda a: a.reshape(B, L, -1)
    att_p = _attn_prompt(lam, qp, r3(kpb), vpb, g_sub, out_scale=out_scale)
    n_steps_p = _row_tile(L, 256) // SUBLANES
    tabs_p = _s5_tables(ssm_a_re[l], ssm_a_im[l], ssm_log_dt[l], ssm_b_re[l].astype(F32), ssm_b_im[l].astype(F32),
                        ssm_c_re[l], ssm_c_im[l], n_steps_p)
    ssm_p, hre_p, him_p = _s5_mixer(r3(up), tabs_p, d_row, w_glu_bf, ng_row, n_steps=n_steps_p)
    y_prompt = tail(att_p.reshape(B * L, -1), ssm_p.reshape(B * L, -1), xp2).reshape(B, L, D)

    cos_s, sin_s = _rope_tables(P + jnp.arange(S))
    cos_s, sin_s = jnp.tile(cos_s, (Bs, 1)), jnp.tile(sin_s, (Bs, 1))
    xs2 = x_sample.reshape(Bs * S, D)
    qs, ksf, ksb, vsf, vsb, us = _in_proj(xs2, w_in_bf, cos_s, sin_s, tm=_row_tile(Bs * S, 512))
    s3 = lambda a: a.reshape(Bs, S, -1)
    att_s = _attn_sample(lam, s3(qs), cache_k[l].reshape(Bs, P, n_qk), cache_v[l].reshape(Bs, P, H * V_DIM),
                         s3(ksb), s3(vsb), g_sub, out_scale=out_scale)
    tabs_s = _s5_tables(ssm_a_re[l], ssm_a_im[l], ssm_log_dt[l], ssm_b_re[l].astype(F32), ssm_b_im[l].astype(F32),
                        ssm_c_re[l], ssm_c_im[l], S)
    h0 = (state_ssm_re[l].astype(F32).reshape(Bs, G * N), state_ssm_im[l].astype(F32).reshape(Bs, G * N))
    ssm_s, hre_s, him_s = _s5_mixer(s3(us), tabs_s, d_row, w_glu_bf, ng_row, n_steps=S, h0=h0)
    y_sample = tail(att_s.reshape(Bs * S, -1), ssm_s.reshape(Bs * S, -1), xs2).reshape(Bs, S, D)

    return (y_prompt, y_sample,
            kpf.reshape(1, B, L, H, 2, HEAD_DIM), vpf.reshape(1, B, L, H, V_DIM),
            hre_p.reshape(1, B, G, N), him_p.reshape(1, B, G, N),
            ksf.reshape(1, Bs, S, H, 2, HEAD_DIM), vsf.reshape(1, Bs, S, H, V_DIM),
            hre_s.reshape(1, Bs, G, N), him_s.reshape(1, Bs, G, N))
```

```python
import functools
import math

import jax
import jax.numpy as jnp
from jax import lax
from jax.experimental import pallas as pl
from jax.experimental.pallas import tpu as pltpu

F32 = jnp.float32
BF16 = jnp.bfloat16

HEAD_DIM = 64
V_DIM = 2 * HEAD_DIM
CHUNK = 64
SSM_GROUP = 16
SSM_STATE = 64
N_EXPERT_GROUPS = 4
EXPERTS_PER_GROUP = 4
N_EXPERTS = N_EXPERT_GROUPS * EXPERTS_PER_GROUP
ROPE_THETA = 10000.0
LN_EPS = 1e-5
RMS_EPS = 1e-6
LANES = 128
SUBLANES = 8
MXU_N = 256
NEG_BIG = -1e30
SUM_ROWS = 16
LOG2E = 1.4426950408889634
VMEM_LIMIT = 52 * 1024 * 1024


def _cparams(*sem):
    return pltpu.CompilerParams(dimension_semantics=sem, vmem_limit_bytes=VMEM_LIMIT)


def _store_transposed_tiles(ref, a):
    n_t, n_h, _, t = ref.shape
    at = a.T
    for n in range(n_t):
        for h in range(n_h):
            ref[n, h] = at[V_DIM * h:V_DIM * (h + 1), t * n:t * (n + 1)].astype(ref.dtype)


def _inproj_kernel(x_ref, w_ref, cos_ref, sin_ref, q_ref, kf_ref, kb_ref, vf_ref, vb_ref, u_ref,
                   *, q_scale, n_qk, transposed):
    xb = x_ref[...].astype(BF16)
    cos = cos_ref[...]
    sin = sin_ref[...]
    lane = lax.broadcasted_iota(jnp.int32, cos.shape, 1)
    first = (lane & (HEAD_DIM - 1)) < HEAD_DIM // 2

    def rope(t):
        rot = jnp.where(first, pltpu.roll(t, LANES - HEAD_DIM // 2, 1), pltpu.roll(t, HEAD_DIM // 2, 1))
        return t * cos + rot * sin

    pq = jnp.dot(xb, w_ref[:, 0:n_qk], preferred_element_type=F32)
    qr = jnp.concatenate([rope(pq[:, LANES * j:LANES * (j + 1)]) for j in range(n_qk // LANES)], axis=1) * q_scale
    if transposed:
        _store_transposed_tiles(q_ref, qr)
    else:
        q_ref[...] = qr.astype(BF16)
    pk = jnp.dot(xb, w_ref[:, n_qk:2 * n_qk], preferred_element_type=F32)
    kr = jnp.concatenate([rope(pk[:, LANES * j:LANES * (j + 1)]) for j in range(n_qk // LANES)], axis=1)
    kb_ref[...] = kr.astype(BF16)
    n_v = u_ref.shape[1]
    pv = jnp.dot(xb, w_ref[:, 2 * n_qk:2 * n_qk + n_v], preferred_element_type=F32)
    if transposed:
        kf_ref[...] = kr.T
        n_h = n_v // V_DIM
        for h in range(n_h):
            vf_ref[pl.ds(h, pv.shape[0], stride=n_h), :] = pv[:, V_DIM * h:V_DIM * (h + 1)]
        _store_transposed_tiles(vb_ref, pv)
    else:
        kf_ref[...] = kr
        vf_ref[...] = pv
        vb_ref[...] = pv.astype(BF16)
    u_ref[...] = jnp.dot(xb, w_ref[:, 2 * n_qk + n_v:], preferred_element_type=F32)


def _in_proj(x2d, w_bf, cos_t, sin_t, *, tm, t_attn=None):
    T, D = x2d.shape
    n_cols = w_bf.shape[1]
    n_qk = n_cols // 4
    n_tab = cos_t.shape[0] // tm
    row = lambda i: (i, 0)
    tab = lambda i: (i % n_tab, 0)
    out_sds = lambda dt: jax.ShapeDtypeStruct((T, n_qk), dt)
    spec = pl.BlockSpec((tm, n_qk), row)
    qv_spec, qv_sds = spec, out_sds(BF16)
    kf_spec, kf_sds, vf_spec, vf_sds = spec, out_sds(F32), spec, out_sds(F32)
    if t_attn is not None:
        H = n_qk // V_DIM
        qv_spec = pl.BlockSpec((tm // t_attn, H, V_DIM, t_attn), lambda i: (i, 0, 0, 0))
        qv_sds = jax.ShapeDtypeStruct((T // t_attn, H, V_DIM, t_attn), BF16)
        kf_spec = pl.BlockSpec((None, n_qk, tm), lambda i: (i // n_tab, 0, i % n_tab))
        kf_sds = jax.ShapeDtypeStruct((T // cos_t.shape[0], n_qk, cos_t.shape[0]), F32)
        vf_spec = pl.BlockSpec((tm * H, V_DIM), row)
        vf_sds = jax.ShapeDtypeStruct((T * H, V_DIM), F32)
    return pl.pallas_call(
        functools.partial(_inproj_kernel, q_scale=HEAD_DIM ** -0.5 * LOG2E, n_qk=n_qk,
                          transposed=t_attn is not None),
        grid=(T // tm,),
        in_specs=[pl.BlockSpec((tm, D), row),
                  pl.BlockSpec((D, n_cols), lambda i: (0, 0)),
                  pl.BlockSpec((tm, LANES), tab),
                  pl.BlockSpec((tm, LANES), tab)],
        out_specs=[qv_spec, kf_spec, spec, vf_spec, qv_spec, spec],
        out_shape=[qv_sds, kf_sds, out_sds(BF16), vf_sds, qv_sds, out_sds(F32)],
        compiler_params=_cparams("parallel"),
        name="in_proj_rope",
    )(x2d, w_bf, cos_t, sin_t)


def _split_maps(q):
    lane = lax.broadcasted_iota(jnp.int32, q.shape, 1)
    zero = jnp.zeros_like(q)
    return jnp.concatenate([jnp.where(lane < HEAD_DIM, q, zero), jnp.where(lane >= HEAD_DIM, q, zero)], axis=0)


def _qk(qs, kb):
    return lax.dot_general(qs, kb, (((1,), (1,)), ((), ())), preferred_element_type=F32)


def _subln(o, g, out_scale):
    ms = jnp.mean(o * o, axis=-1, keepdims=True)
    return o * lax.rsqrt(ms + RMS_EPS) * g * out_scale


def _attn_prompt_kernel(lam_ref, qt_ref, k_ref, vt_ref, g_ref, o_ref, qs_ref, acc_ref, s0_ref, s1_ref, m_ref,
                        *, tq, out_scale):
    i = pl.program_id(2)
    s_buf = (s0_ref, s1_ref)
    qt = qt_ref[...]
    row = lax.broadcasted_iota(jnp.int32, qt.shape, 0)
    zero = jnp.zeros_like(qt)
    qs_ref[:, 0:tq] = jnp.where(row < HEAD_DIM, qt, zero)
    qs_ref[:, tq:2 * tq] = jnp.where(row >= HEAD_DIM, qt, zero)
    acc_ref[...] = jnp.zeros(acc_ref.shape, F32)
    m_ref[...] = jnp.full(m_ref.shape, NEG_BIG, F32)
    ones_rows = jnp.ones((SUM_ROWS, tq), BF16)

    def scores(kt, slot):
        kb = k_ref[pl.ds(pl.multiple_of(kt * tq, tq), tq), :]
        s_buf[slot][...] = jnp.dot(kb, qs_ref[...], preferred_element_type=F32)

    def softmax_pv(kt, slot, masked):
        lhs = jnp.concatenate([vt_ref[kt], ones_rows], axis=0)
        for cb in range(2 * tq // MXU_N):
            cs = slice(MXU_N * cb, MXU_N * (cb + 1))
            st = s_buf[slot][:, cs]
            if masked:
                c = lax.broadcasted_iota(jnp.int32, st.shape, 0)
                r = (lax.broadcasted_iota(jnp.int32, st.shape, 1) + MXU_N * cb) % tq
                st = jnp.where((c // CHUNK) <= (r // CHUNK), st, NEG_BIG)
            m_prev = m_ref[:, cs]
            m_new = jnp.maximum(m_prev, jnp.max(st, axis=0, keepdims=True))
            alpha = jnp.exp2(m_prev - m_new)
            pt = jnp.exp2(st - m_new).astype(BF16)
            m_ref[:, cs] = m_new
            acc_ref[:, cs] = alpha * acc_ref[:, cs] + jnp.dot(lhs, pt, preferred_element_type=F32)

    def step(t, par):
        scores(t, par)
        softmax_pv(t - 1, 1 - par, False)

    scores(0, 0)

    def pair(j, carry):
        step(2 * j + 1, 1)
        step(2 * j + 2, 0)
        return carry

    lax.fori_loop(0, i // 2, pair, 0)

    @pl.when(i % 2 == 0)
    def _():
        softmax_pv(i, 0, True)

    @pl.when(i % 2 == 1)
    def _():
        step(i, 1)
        softmax_pv(i, 1, True)

    on = acc_ref[0:V_DIM, :] * (1.0 / acc_ref[V_DIM:V_DIM + 1, :])
    ot = on[:, 0:tq] - lam_ref[0] * on[:, tq:2 * tq]
    o_ref[...] = _subln(ot.T, g_ref[...], out_scale).astype(o_ref.dtype)


def _attn_prompt(lam, qt, k, vt, g, *, out_scale):
    B, L, W = k.shape
    n_t, H, _, t = qt.shape
    nq = L // t
    return pl.pallas_call(
        functools.partial(_attn_prompt_kernel, tq=t, out_scale=out_scale),
        grid=(B, H, nq),
        in_specs=[pl.BlockSpec(memory_space=pltpu.SMEM),
                  pl.BlockSpec((None, None, V_DIM, t), lambda b, h, i: (b * nq + i, h, 0, 0)),
                  pl.BlockSpec((None, L, V_DIM), lambda b, h, i: (b, 0, h)),
                  pl.BlockSpec((nq, None, V_DIM, t), lambda b, h, i: (b, h, 0, 0)),
                  pl.BlockSpec((1, V_DIM), lambda b, h, i: (0, 0))],
        out_specs=pl.BlockSpec((None, t, V_DIM), lambda b, h, i: (b, i, h)),
        out_shape=jax.ShapeDtypeStruct((B, L, W), BF16),
        scratch_shapes=[pltpu.VMEM((V_DIM, 2 * t), BF16), pltpu.VMEM((V_DIM + SUM_ROWS, 2 * t), F32),
                        pltpu.VMEM((t, 2 * t), F32), pltpu.VMEM((t, 2 * t), F32),
                        pltpu.VMEM((1, 2 * t), F32)],
        compiler_params=_cparams("parallel", "parallel", "parallel"),
        name="diff_attn_prompt",
    )(lam, qt, k, vt, g)


def _attn_sample_kernel(lam_ref, q_ref, kct_ref, vc_ref, kn_ref, vn_ref, g_ref, o_ref, *, past_len, n_heads,
                        out_scale):
    tq = q_ref.shape[0]
    h = pl.program_id(1)
    qs = _split_maps(q_ref[...])
    s_c = jnp.dot(qs, kct_ref[...].astype(BF16), preferred_element_type=F32)
    vc = vc_ref[pl.ds(h, past_len, stride=n_heads), :]
    s_n = _qk(qs, kn_ref[...])
    r = lax.broadcasted_iota(jnp.int32, s_n.shape, 0)
    c = lax.broadcasted_iota(jnp.int32, s_n.shape, 1)
    r = jnp.where(r >= tq, r - tq, r)
    s_n = jnp.where(((past_len + c) // CHUNK) <= ((past_len + r) // CHUNK), s_n, NEG_BIG)
    m = jnp.maximum(jnp.max(s_c, axis=-1, keepdims=True), jnp.max(s_n, axis=-1, keepdims=True))
    p_c = jnp.exp2(s_c - m)
    p_n = jnp.exp2(s_n - m)
    inv_l = 1.0 / (jnp.sum(p_c, axis=-1, keepdims=True) + jnp.sum(p_n, axis=-1, keepdims=True))
    lam = lam_ref[0]
    p_c = p_c * inv_l
    p_n = p_n * inv_l
    w_c = p_c[0:tq] - lam * p_c[tq:2 * tq]
    w_n = p_n[0:tq] - lam * p_n[tq:2 * tq]
    o = (jnp.dot(w_c.astype(BF16), vc.astype(BF16), preferred_element_type=F32)
         + jnp.dot(w_n.astype(BF16), vn_ref[...], preferred_element_type=F32))
    o_ref[...] = _subln(o, g_ref[...], out_scale).astype(o_ref.dtype)


def _attn_sample(lam, q, cache_kt, cache_v, k_new, v_new, g, *, out_scale):
    B, S, W = q.shape
    P = cache_kt.shape[1]
    H = W // V_DIM
    blk_new = pl.BlockSpec((None, S, V_DIM), lambda b, h: (b, 0, h))
    return pl.pallas_call(
        functools.partial(_attn_sample_kernel, past_len=P, n_heads=H, out_scale=out_scale),
        grid=(B, H),
        in_specs=[pl.BlockSpec(memory_space=pltpu.SMEM), blk_new,
                  pl.BlockSpec((V_DIM, P), lambda b, h: (b * H + h, 0)),
                  pl.BlockSpec((P * H, V_DIM), lambda b, h: (b, 0)),
                  blk_new, blk_new, pl.BlockSpec((1, V_DIM), lambda b, h: (0, 0))],
        out_specs=blk_new,
        out_shape=jax.ShapeDtypeStruct((B, S, W), BF16),
        compiler_params=_cparams("parallel", "arbitrary"),
        name="diff_attn_sample",
    )(lam, q, cache_kt, cache_v, k_new, v_new, g)


def _cmul(ar, ai, br, bi):
    return ar * br - ai * bi, ar * bi + ai * br


def _s5_kernel(*refs, n_steps, chained, lane_chunk):
    if chained:
        (u_ref, perm_ref, permt_ref, wx_ref, cb_ref, d_ref, pre_ref, pim_ref, gre_ref, gim_ref,
         wglu_ref, ng_ref, y_ref, hre_ref, him_ref, xre, xim, cre, cim) = refs
    else:
        (u_ref, perm_ref, permt_ref, wx_ref, cb_ref, d_ref, pre_ref, pim_ref, gre_ref, gim_ref,
         wglu_ref, ng_ref, h0re_ref, h0im_ref, y_ref, hre_ref, him_ref, xre, xim) = refs
    R = n_steps
    n_state = xre.shape[1]
    n_blk = wx_ref.shape[0]
    blk_in = wx_ref.shape[1]
    blk_st = wx_ref.shape[2] // 2

    u = u_ref[...]
    u_hi = u.astype(BF16)
    u_lo = (u - u_hi.astype(F32)).astype(BF16)
    perm = perm_ref[...]
    up = (jnp.dot(perm, u_hi, preferred_element_type=F32) + jnp.dot(perm, u_lo, preferred_element_type=F32))
    upb = up.astype(BF16)
    for j in range(n_blk):
        xj = jnp.dot(upb[:, blk_in * j:blk_in * (j + 1)], wx_ref[j], preferred_element_type=F32)
        xre[:, blk_st * j:blk_st * (j + 1)] = xj[:, :blk_st]
        xim[:, blk_st * j:blk_st * (j + 1)] = xj[:, blk_st:]

    n_chunks = n_state // lane_chunk
    e_re, e_im = [], []
    for c in range(n_chunks):
        cs = slice(lane_chunk * c, lane_chunk * (c + 1))
        ar = jnp.broadcast_to(pre_ref[0:1, cs], (SUBLANES, lane_chunk))
        ai = jnp.broadcast_to(pim_ref[0:1, cs], (SUBLANES, lane_chunk))

        def scan_body(r, h, cs=cs, ar=ar, ai=ai):
            hr, hi = h
            rows = pl.ds(pl.multiple_of(r * SUBLANES, SUBLANES), SUBLANES)
            nr = ar * hr - ai * hi + xre[rows, cs]
            ni = ar * hi + ai * hr + xim[rows, cs]
            xre[rows, cs] = nr
            xim[rows, cs] = ni
            return nr, ni

        z = jnp.zeros((SUBLANES, lane_chunk), F32)
        er, ei = lax.fori_loop(0, R, scan_body, (z, z), unroll=4)
        e_re.append(er)
        e_im.append(ei)
    e_re = jnp.concatenate(e_re, axis=1)
    e_im = jnp.concatenate(e_im, axis=1)

    g1r = jnp.broadcast_to(gre_ref[0:1, :], e_re.shape)
    g1i = jnp.broadcast_to(gim_ref[0:1, :], e_re.shape)
    if chained:
        t = pl.program_id(1)

        @pl.when(t == 0)
        def _():
            cre[...] = jnp.zeros(cre.shape, F32)
            cim[...] = jnp.zeros(cim.shape, F32)

        sub = lax.broadcasted_iota(jnp.int32, e_re.shape, 0)
        c_r = cre[...]
        c_i = cim[...]
        inj_r, inj_i = _cmul(g1r, g1i, c_r, c_i)
        xr = e_re + jnp.where(sub == 0, inj_r, 0.0)
        xi = e_im + jnp.where(sub == 0, inj_i, 0.0)
        for lvl, k in enumerate((1, 2, 4)):
            gr = jnp.broadcast_to(gre_ref[lvl:lvl + 1, :], e_re.shape)
            gi = jnp.broadcast_to(gim_ref[lvl:lvl + 1, :], e_re.shape)
            sr = jnp.where(sub >= k, pltpu.roll(xr, k, 0), 0.0)
            si = jnp.where(sub >= k, pltpu.roll(xi, k, 0), 0.0)
            tr, ti = _cmul(gr, gi, sr, si)
            xr = xr + tr
            xi = xi + ti
        hend_r, hend_i = xr, xi
        hc_r = jnp.where(sub == 0, c_r, pltpu.roll(hend_r, 1, 0))
        hc_i = jnp.where(sub == 0, c_i, pltpu.roll(hend_i, 1, 0))
        last_r = jnp.broadcast_to(hend_r[SUBLANES - 1:SUBLANES, :], e_re.shape)
        last_i = jnp.broadcast_to(hend_i[SUBLANES - 1:SUBLANES, :], e_re.shape)
        cre[...] = last_r
        cim[...] = last_i
        hre_ref[...] = last_r[0:1, :]
        him_ref[...] = last_i[0:1, :]
    else:
        hc_r = h0re_ref[...]
        hc_i = h0im_ref[...]
        tr, ti = _cmul(g1r, g1i, hc_r, hc_i)
        hre_ref[...] = e_re + tr
        him_ref[...] = e_im + ti

    for c in range(n_chunks):
        cs = slice(lane_chunk * c, lane_chunk * (c + 1))
        hcr = hc_r[:, cs]
        hci = hc_i[:, cs]

        def fix_body(r, carry, cs=cs, hcr=hcr, hci=hci):
            rows = pl.ds(pl.multiple_of(r * SUBLANES, SUBLANES), SUBLANES)
            pr = jnp.broadcast_to(pre_ref[pl.ds(r, 1), cs], hcr.shape)
            pi = jnp.broadcast_to(pim_ref[pl.ds(r, 1), cs], hcr.shape)
            tr, ti = _cmul(pr, pi, hcr, hci)
            xre[rows, cs] = xre[rows, cs] + tr
            xim[rows, cs] = xim[rows, cs] + ti
            return carry

        lax.fori_loop(0, R, fix_body, 0, unroll=4)

    ys = []
    for j in range(n_blk):
        st = slice(blk_st * j, blk_st * (j + 1))
        ys.append(jnp.dot(xre[:, st].astype(BF16), cb_ref[j, 0:blk_st, :], preferred_element_type=F32)
                  + jnp.dot(xim[:, st].astype(BF16), cb_ref[j, blk_st:, :], preferred_element_type=F32))
    y = jnp.concatenate(ys, axis=1) + d_ref[...] * up
    y = 0.5 * y * (1.0 + jnp.tanh(math.sqrt(2.0 / math.pi) * (y + 0.044715 * (y * y * y))))
    gl = jnp.dot(y.astype(BF16), wglu_ref[...], preferred_element_type=F32)
    half = gl.shape[1] // 2
    z = gl[:, :half] * (1.0 / (1.0 + jnp.exp(-gl[:, half:])))
    z = z * lax.rsqrt(jnp.mean(z * z, axis=-1, keepdims=True) + RMS_EPS) * ng_ref[...]
    y_ref[...] = jnp.dot(permt_ref[...], z.astype(BF16), preferred_element_type=F32).astype(y_ref.dtype)


def _s5_tables(a_re, a_im, log_dt, b_re, b_im, c_re, c_im, n_steps):
    G, N = a_re.shape
    C = b_re.shape[2]
    dt = jnp.exp(log_dt.astype(F32))[:, None]
    a_re = a_re.astype(F32)
    a_im = a_im.astype(F32)

    def power(k):
        mag = jnp.exp(k * a_re * dt)
        return mag * jnp.cos(k * a_im * dt), mag * jnp.sin(k * a_im * dt)

    ab_re, ab_im = power(1.0)
    nr, ni = ab_re - 1.0, ab_im
    den = a_re * a_re + a_im * a_im
    cf_re = (nr * a_re + ni * a_im) / den
    cf_im = (ni * a_re - nr * a_im) / den
    ks = jnp.arange(1, n_steps + 1, dtype=F32)[:, None, None]
    p_re, p_im = power(ks)
    g_re, g_im = zip(*[power(float(m * n_steps)) for m in (1, 2, 4)])
    pad = jnp.zeros((SUBLANES - 3, G * N), F32)
    g_re = jnp.concatenate([jnp.stack(g_re).reshape(3, G * N), pad])
    g_im = jnp.concatenate([jnp.stack(g_im).reshape(3, G * N), pad])
    wx_re = cf_re[..., None] * b_re - cf_im[..., None] * b_im
    wx_im = cf_re[..., None] * b_im + cf_im[..., None] * b_re
    gpb = LANES // C
    nb = G // gpb
    eye = jnp.eye(gpb, dtype=F32)

    def in_block(w):
        w = w.reshape(nb, gpb, N, C)
        return jnp.einsum('jgnc,gh->jgchn', w, eye).reshape(nb, gpb * C, gpb * N)

    def out_block(w):
        w = w.reshape(nb, gpb, C, N)
        return jnp.einsum('jgcn,gh->jgnhc', w, eye).reshape(nb, gpb * N, gpb * C)

    wx = jnp.concatenate([in_block(wx_re), in_block(wx_im)], axis=2).astype(BF16)
    cb = jnp.concatenate([out_block(c_re.astype(F32)), out_block(-c_im.astype(F32))], axis=1).astype(BF16)
    return (p_re.reshape(n_steps, G * N), p_im.reshape(n_steps, G * N), g_re, g_im, wx, cb)


def _perm_matrices(n_steps):
    tt = SUBLANES * n_steps
    dst = jnp.arange(tt)
    src = (dst % SUBLANES) * n_steps + dst // SUBLANES
    perm = (src[:, None] == jnp.arange(tt)[None, :]).astype(BF16)
    return perm, perm.T


def _s5_mixer(u, tables, d, w_glu_bf, norm_g, *, n_steps, h0=None):
    B, L, W = u.shape
    p_re, p_im, g_re, g_im, wx, cb = tables
    n_state = p_re.shape[1]
    tt = SUBLANES * n_steps
    perm, permt = _perm_matrices(n_steps)
    chained = h0 is None
    const2 = lambda *_: (0, 0)
    const3 = lambda *_: (0, 0, 0)
    common_specs = [pl.BlockSpec(perm.shape, const2), pl.BlockSpec(perm.shape, const2),
                    pl.BlockSpec(wx.shape, const3), pl.BlockSpec(cb.shape, const3),
                    pl.BlockSpec((1, W), const2),
                    pl.BlockSpec(p_re.shape, const2), pl.BlockSpec(p_im.shape, const2),
                    pl.BlockSpec(g_re.shape, const2), pl.BlockSpec(g_im.shape, const2),
                    pl.BlockSpec(w_glu_bf.shape, const2), pl.BlockSpec((1, W), const2)]
    common_args = (perm, permt, wx, cb, d, p_re, p_im, g_re, g_im, w_glu_bf, norm_g)
    state_scratch = [pltpu.VMEM((tt, n_state), F32), pltpu.VMEM((tt, n_state), F32)]
    kern = functools.partial(_s5_kernel, n_steps=n_steps, chained=chained, lane_chunk=512)
    if chained:
        y, hre, him = pl.pallas_call(
            kern,
            grid=(B, L // tt),
            in_specs=[pl.BlockSpec((None, tt, W), lambda b, t: (b, t, 0))] + common_specs,
            out_specs=[pl.BlockSpec((None, tt, W), lambda b, t: (b, t, 0)),
                       pl.BlockSpec((None, 1, n_state), lambda b, t: (b, 0, 0)),
                       pl.BlockSpec((None, 1, n_state), lambda b, t: (b, 0, 0))],
            out_shape=[jax.ShapeDtypeStruct((B, L, W), BF16),
                       jax.ShapeDtypeStruct((B, 1, n_state), F32),
                       jax.ShapeDtypeStruct((B, 1, n_state), F32)],
            scratch_shapes=state_scratch + [pltpu.VMEM((SUBLANES, n_state), F32)] * 2,
            compiler_params=_cparams("parallel", "arbitrary"),
            name="s5_mixer_chained",
        )(u, *common_args)
        return y, hre.reshape(B, n_state), him.reshape(B, n_state)
    u2 = u.reshape(B * L, W)
    h0re, h0im = h0
    y, hre, him = pl.pallas_call(
        kern,
        grid=(B // SUBLANES,),
        in_specs=[pl.BlockSpec((tt, W), lambda i: (i, 0))] + common_specs
        + [pl.BlockSpec((SUBLANES, n_state), lambda i: (i, 0))] * 2,
        out_specs=[pl.BlockSpec((tt, W), lambda i: (i, 0)),
                   pl.BlockSpec((SUBLANES, n_state), lambda i: (i, 0)),
                   pl.BlockSpec((SUBLANES, n_state), lambda i: (i, 0))],
        out_shape=[jax.ShapeDtypeStruct((B * L, W), BF16),
                   jax.ShapeDtypeStruct((B, n_state), F32),
                   jax.ShapeDtypeStruct((B, n_state), F32)],
        scratch_shapes=state_scratch,
        compiler_params=_cparams("parallel"),
        name="s5_mixer_batched",
    )(u2, *common_args, h0re, h0im)
    return y.reshape(B, L, W), hre, him


def _layer_norm(z, g, b):
    mu = jnp.mean(z, axis=-1, keepdims=True)
    zc = z - mu
    var = jnp.mean(zc * zc, axis=-1, keepdims=True)
    return zc * lax.rsqrt(var + LN_EPS) * g + b


def _mix_router_kernel(att_ref, ssm_ref, x_ref, wa_ref, wb_ref, g_ref, b_ref, rhi_ref, rlo_ref, rb_ref,
                       x1_ref, gates_ref, *, alpha):
    mix = (jnp.dot(att_ref[...], wa_ref[...], preferred_element_type=F32)
           + jnp.dot(ssm_ref[...], wb_ref[...], preferred_element_type=F32))
    x1 = _layer_norm(alpha * x_ref[...] + mix, g_ref[...], b_ref[...])
    x1_ref[...] = x1
    hi = x1.astype(BF16)
    lo = (x1 - hi.astype(F32)).astype(BF16)
    lg = (jnp.dot(hi, rhi_ref[...], preferred_element_type=F32)
          + jnp.dot(lo, rhi_ref[...], preferred_element_type=F32)
          + jnp.dot(hi, rlo_ref[...], preferred_element_type=F32)) + rb_ref[...]
    lane = lax.broadcasted_iota(jnp.int32, lg.shape, 1)
    big = jnp.int32(LANES)
    is_grp = (lane >= N_EXPERTS) & (lane < N_EXPERTS + N_EXPERT_GROUPS)
    gl = jnp.where(is_grp, lg, NEG_BIG)
    gexp = jnp.where(is_grp, jnp.exp(gl - jnp.max(gl, axis=-1, keepdims=True)), 0.0)
    gprob = gexp / jnp.sum(gexp, axis=-1, keepdims=True)
    g_w = jnp.max(gprob, axis=-1, keepdims=True)
    g_idx = jnp.min(jnp.where(is_grp & (gprob == g_w), lane - N_EXPERTS, big), axis=-1, keepdims=True)
    valid = (lane < N_EXPERTS) & ((lane // EXPERTS_PER_GROUP) == g_idx)
    el = jnp.where(valid, lg, NEG_BIG)
    eexp = jnp.where(valid, jnp.exp(el - jnp.max(el, axis=-1, keepdims=True)), 0.0)
    eprob = eexp / jnp.sum(eexp, axis=-1, keepdims=True)
    w1 = jnp.max(jnp.where(valid, eprob, -1.0), axis=-1, keepdims=True)
    i1 = jnp.min(jnp.where(valid & (eprob == w1), lane, big), axis=-1, keepdims=True)
    rest = valid & (lane != i1)
    w2 = jnp.max(jnp.where(rest, eprob, -1.0), axis=-1, keepdims=True)
    i2 = jnp.min(jnp.where(rest & (eprob == w2), lane, big), axis=-1, keepdims=True)
    den = w1 + w2
    gates_ref[...] = (jnp.where(lane == i1, g_w * (w1 / den), 0.0)
                      + jnp.where(lane == i2, g_w * (w2 / den), 0.0))


def _mix_router(att, ssm, x2d, wa, wb, g, b, rhi, rlo, rb, *, tm, alpha):
    T, D = x2d.shape
    Wh = att.shape[1]
    row = lambda i: (i, 0)
    const = lambda i: (0, 0)
    return pl.pallas_call(
        functools.partial(_mix_router_kernel, alpha=alpha),
        grid=(T // tm,),
        in_specs=[pl.BlockSpec((tm, Wh), row), pl.BlockSpec((tm, Wh), row), pl.BlockSpec((tm, D), row),
                  pl.BlockSpec(wa.shape, const), pl.BlockSpec(wb.shape, const),
                  pl.BlockSpec((1, D), const), pl.BlockSpec((1, D), const),
                  pl.BlockSpec(rhi.shape, const), pl.BlockSpec(rlo.shape, const), pl.BlockSpec((1, LANES), const)],
        out_specs=[pl.BlockSpec((tm, D), row), pl.BlockSpec((tm, LANES), row)],
        out_shape=[jax.ShapeDtypeStruct((T, D), F32), jax.ShapeDtypeStruct((T, LANES), F32)],
        compiler_params=_cparams("parallel"),
        name="out_proj_ln_router",
    )(att, ssm, x2d, wa, wb, g, b, rhi, rlo, rb)


def _moe_kernel(x1_ref, gates_ref, wg_ref, wu_ref, wd_ref, g_ref, b_ref, o_ref, acc_ref, *, alpha):
    x1 = x1_ref[...]
    xb = x1.astype(BF16)
    gates = gates_ref[...]
    for e in range(wg_ref.shape[0]):
        hg = jnp.dot(xb, wg_ref[e], preferred_element_type=F32)
        hu = jnp.dot(xb, wu_ref[e], preferred_element_type=F32)
        h = hg * (1.0 / (1.0 + jnp.exp(-hg))) * hu * gates[:, e:e + 1]
        contrib = jnp.dot(h.astype(BF16), wd_ref[e], preferred_element_type=F32)
        if e == 0:
            acc_ref[...] = contrib
        else:
            acc_ref[...] += contrib
    o_ref[...] = _layer_norm(alpha * x1 + acc_ref[...], g_ref[...], b_ref[...])


def _moe(x1, gates, wg, wu, wd, g, b, *, tm, alpha):
    T, D = x1.shape
    row = lambda i: (i, 0)
    const2 = lambda i: (0, 0)
    const3 = lambda i: (0, 0, 0)
    once = pl.Buffered(1)
    return pl.pallas_call(
        functools.partial(_moe_kernel, alpha=alpha),
        grid=(T // tm,),
        in_specs=[pl.BlockSpec((tm, D), row), pl.BlockSpec((tm, LANES), row),
                  pl.BlockSpec(wg.shape, const3, pipeline_mode=once),
                  pl.BlockSpec(wu.shape, const3, pipeline_mode=once),
                  pl.BlockSpec(wd.shape, const3, pipeline_mode=once),
                  pl.BlockSpec((1, D), const2), pl.BlockSpec((1, D), const2)],
        out_specs=pl.BlockSpec((tm, D), row),
        out_shape=jax.ShapeDtypeStruct((T, D), F32),
        scratch_shapes=[pltpu.VMEM((tm, D), F32)],
        compiler_params=_cparams("parallel"),
        name="moe_ln",
    )(x1, gates, wg, wu, wd, g, b)


def _rope_tables(pos):
    half = HEAD_DIM // 2
    inv = 1.0 / (ROPE_THETA ** (jnp.arange(half, dtype=F32) / half))
    ang = pos.astype(F32)[:, None] * inv[None, :]
    cos = jnp.cos(ang)
    sin = jnp.sin(ang)
    cos_t = jnp.tile(cos, (1, LANES // half))
    sin_t = jnp.tile(jnp.concatenate([-sin, sin], axis=1), (1, LANES // HEAD_DIM))
    return cos_t, sin_t


def _row_tile(n, pref):
    t = min(n, pref)
    while n % t:
        t //= 2
    return t


def kernel(x_prompt, x_sample, cache_k, cache_v, state_ssm_re, state_ssm_im, w_in, lam_q1, lam_k1, lam_q2, lam_k2, subln_g, ssm_a_re, ssm_a_im, ssm_log_dt, ssm_b_re, ssm_b_im, ssm_c_re, ssm_c_im, ssm_d, w_glu, ssm_norm_g, w_out, ln1_g, ln1_b, w_grp, b_grp, w_rt, b_rt, w_gate, w_up, w_down, ln2_g, ln2_b):
    depth = w_in.shape[0]
    assert depth == 1, "single-layer step"
    B, L, D = x_prompt.shape
    Bs, S, _ = x_sample.shape
    P = cache_k.shape[2]
    H = cache_k.shape[3]
    n_qk = H * 2 * HEAD_DIM
    G, N = ssm_a_re.shape[1], ssm_a_re.shape[2]
    alpha = (2.0 * depth) ** 0.25
    lam_init = 0.8 - 0.6 * math.exp(-0.3 * 0)
    out_scale = 1.0 - lam_init
    l = 0

    w_in_bf = w_in[l].astype(BF16)
    lam = (jnp.exp(jnp.sum(lam_q1[l].astype(F32) * lam_k1[l].astype(F32)))
           - jnp.exp(jnp.sum(lam_q2[l].astype(F32) * lam_k2[l].astype(F32))) + lam_init).reshape(1).astype(F32)
    g_sub = subln_g[l].astype(F32).reshape(1, V_DIM)
    w_glu_bf = w_glu[l].astype(BF16)
    ssm_w = D - H * V_DIM
    d_row = ssm_d[l].astype(F32).reshape(1, ssm_w)
    ng_row = ssm_norm_g[l].astype(F32).reshape(1, ssm_w)
    wa = w_out[l, :H * V_DIM].astype(BF16)
    wb = w_out[l, H * V_DIM:].astype(BF16)
    ln1g, ln1b = ln1_g[l].astype(F32).reshape(1, D), ln1_b[l].astype(F32).reshape(1, D)
    ln2g, ln2b = ln2_g[l].astype(F32).reshape(1, D), ln2_b[l].astype(F32).reshape(1, D)
    w_r = jnp.concatenate([w_rt[l].astype(F32), w_grp[l].astype(F32),
                           jnp.zeros((D, LANES - N_EXPERTS - N_EXPERT_GROUPS), F32)], axis=1)
    rhi = w_r.astype(BF16)
    rlo = (w_r - rhi.astype(F32)).astype(BF16)
    rb = jnp.concatenate([b_rt[l].astype(F32), b_grp[l].astype(F32),
                          jnp.zeros((LANES - N_EXPERTS - N_EXPERT_GROUPS,), F32)]).reshape(1, LANES)
    wg, wu, wd = w_gate[l].astype(BF16), w_up[l].astype(BF16), w_down[l].astype(BF16)

    def tail(att, ssm, x2d):
        tm = _row_tile(x2d.shape[0], 512)
        x1, gates = _mix_router(att, ssm, x2d, wa, wb, ln1g, ln1b, rhi, rlo, rb, tm=tm, alpha=alpha)
        return _moe(x1, gates, wg, wu, wd, ln2g, ln2b, tm=tm, alpha=alpha)

    cos_p, sin_p = _rope_tables(jnp.arange(L))
    xp2 = x_prompt.reshape(B * L, D)
    qp, kpf, kpb, vpf, vpb, up = _in_proj(xp2, w_in_bf, cos_p, sin_p, tm=_row_tile(L, 512),
                                          t_attn=_row_tile(L, 512))
    r3 = lambda a: a.reshape(B, L, -1)
    att_p = _attn_prompt(lam, qp, r3(kpb), vpb, g_sub, out_scale=out_scale)
    n_steps_p = _row_tile(L, 256) // SUBLANES
    tabs_p = _s5_tables(ssm_a_re[l], ssm_a_im[l], ssm_log_dt[l], ssm_b_re[l].astype(F32), ssm_b_im[l].astype(F32),
                        ssm_c_re[l], ssm_c_im[l], n_steps_p)
    ssm_p, hre_p, him_p = _s5_mixer(r3(up), tabs_p, d_row, w_glu_bf, ng_row, n_steps=n_steps_p)
    y_prompt = tail(att_p.reshape(B * L, -1), ssm_p.reshape(B * L, -1), xp2).reshape(B, L, D)

    cos_s, sin_s = _rope_tables(P + jnp.arange(S))
    cos_s, sin_s = jnp.tile(cos_s, (Bs, 1)), jnp.tile(sin_s, (Bs, 1))
    xs2 = x_sample.reshape(Bs * S, D)
    qs, ksf, ksb, vsf, vsb, us = _in_proj(xs2, w_in_bf, cos_s, sin_s, tm=_row_tile(Bs * S, 512))
    s3 = lambda a: a.reshape(Bs, S, -1)
    cache_kt = jnp.transpose(cache_k[l], (0, 2, 3, 4, 1)).reshape(Bs * n_qk, P)
    att_s = _attn_sample(lam, s3(qs), cache_kt, cache_v[l].reshape(Bs * P * H, V_DIM),
                         s3(ksb), s3(vsb), g_sub, out_scale=out_scale)
    tabs_s = _s5_tables(ssm_a_re[l], ssm_a_im[l], ssm_log_dt[l], ssm_b_re[l].astype(F32), ssm_b_im[l].astype(F32),
                        ssm_c_re[l], ssm_c_im[l], S)
    h0 = (state_ssm_re[l].astype(F32).reshape(Bs, G * N), state_ssm_im[l].astype(F32).reshape(Bs, G * N))
    ssm_s, hre_s, him_s = _s5_mixer(s3(us), tabs_s, d_row, w_glu_bf, ng_row, n_steps=S, h0=h0)
    y_sample = tail(att_s.reshape(Bs * S, -1), ssm_s.reshape(Bs * S, -1), xs2).reshape(Bs, S, D)

    return (y_prompt, y_sample,
            jnp.transpose(kpf.reshape(1, B, H, 2, HEAD_DIM, L), (0, 1, 5, 2, 3, 4)), vpf.reshape(1, B, L, H, V_DIM),
            hre_p.reshape(1, B, G, N), him_p.reshape(1, B, G, N),
            ksf.reshape(1, Bs, S, H, 2, HEAD_DIM), vsf.reshape(1, Bs, S, H, V_DIM),
            hre_s.reshape(1, Bs, G, N), him_s.reshape(1, Bs, G, N))
```

```python
import functools
import math

import jax
import jax.numpy as jnp
from jax import lax
from jax.experimental import pallas as pl
from jax.experimental.pallas import tpu as pltpu

F32 = jnp.float32
BF16 = jnp.bfloat16

HEAD_DIM = 64
V_DIM = 2 * HEAD_DIM
CHUNK = 64
SSM_GROUP = 16
SSM_STATE = 64
N_EXPERT_GROUPS = 4
EXPERTS_PER_GROUP = 4
N_EXPERTS = N_EXPERT_GROUPS * EXPERTS_PER_GROUP
ROPE_THETA = 10000.0
LN_EPS = 1e-5
RMS_EPS = 1e-6
LANES = 128
SUBLANES = 8
MXU_N = 256
NEG_BIG = -1e30
SUM_ROWS = 16
LOG2E = 1.4426950408889634
VMEM_LIMIT = 52 * 1024 * 1024


def _cparams(*sem):
    return pltpu.CompilerParams(dimension_semantics=sem, vmem_limit_bytes=VMEM_LIMIT)


def _store_transposed_tiles(ref, a):
    n_t, n_h, _, t = ref.shape
    at = a.T
    for n in range(n_t):
        for h in range(n_h):
            ref[n, h] = at[V_DIM * h:V_DIM * (h + 1), t * n:t * (n + 1)].astype(ref.dtype)


def _inproj_kernel(x_ref, w_ref, cos_ref, sin_ref, q_ref, kf_ref, kb_ref, vf_ref, vb_ref, u_ref,
                   *, q_scale, n_qk, transposed):
    xb = x_ref[...].astype(BF16)
    cos = cos_ref[...]
    sin = sin_ref[...]
    lane = lax.broadcasted_iota(jnp.int32, cos.shape, 1)
    first = (lane & (HEAD_DIM - 1)) < HEAD_DIM // 2

    def rope(t):
        rot = jnp.where(first, pltpu.roll(t, LANES - HEAD_DIM // 2, 1), pltpu.roll(t, HEAD_DIM // 2, 1))
        return t * cos + rot * sin

    pq = jnp.dot(xb, w_ref[:, 0:n_qk], preferred_element_type=F32)
    qr = jnp.concatenate([rope(pq[:, LANES * j:LANES * (j + 1)]) for j in range(n_qk // LANES)], axis=1) * q_scale
    if transposed:
        _store_transposed_tiles(q_ref, qr)
    else:
        q_ref[...] = qr.astype(BF16)
    pk = jnp.dot(xb, w_ref[:, n_qk:2 * n_qk], preferred_element_type=F32)
    kr = jnp.concatenate([rope(pk[:, LANES * j:LANES * (j + 1)]) for j in range(n_qk // LANES)], axis=1)
    kb_ref[...] = kr.astype(BF16)
    n_v = u_ref.shape[1]
    pv = jnp.dot(xb, w_ref[:, 2 * n_qk:2 * n_qk + n_v], preferred_element_type=F32)
    if transposed:
        kf_ref[...] = kr.T
        n_h = n_v // V_DIM
        for h in range(n_h):
            vf_ref[pl.ds(h, pv.shape[0], stride=n_h), :] = pv[:, V_DIM * h:V_DIM * (h + 1)]
        _store_transposed_tiles(vb_ref, pv)
    else:
        kf_ref[...] = kr
        vf_ref[...] = pv
        vb_ref[...] = pv.astype(BF16)
    u_ref[...] = jnp.dot(xb, w_ref[:, 2 * n_qk + n_v:], preferred_element_type=F32)


def _in_proj(x2d, w_bf, cos_t, sin_t, *, tm, t_attn=None):
    T, D = x2d.shape
    n_cols = w_bf.shape[1]
    n_qk = n_cols // 4
    n_tab = cos_t.shape[0] // tm
    row = lambda i: (i, 0)
    tab = lambda i: (i % n_tab, 0)
    out_sds = lambda dt: jax.ShapeDtypeStruct((T, n_qk), dt)
    spec = pl.BlockSpec((tm, n_qk), row)
    qv_spec, qv_sds = spec, out_sds(BF16)
    kf_spec, kf_sds, vf_spec, vf_sds = spec, out_sds(F32), spec, out_sds(F32)
    if t_attn is not None:
        H = n_qk // V_DIM
        qv_spec = pl.BlockSpec((tm // t_attn, H, V_DIM, t_attn), lambda i: (i, 0, 0, 0))
        qv_sds = jax.ShapeDtypeStruct((T // t_attn, H, V_DIM, t_attn), BF16)
        kf_spec = pl.BlockSpec((None, n_qk, tm), lambda i: (i // n_tab, 0, i % n_tab))
        kf_sds = jax.ShapeDtypeStruct((T // cos_t.shape[0], n_qk, cos_t.shape[0]), F32)
        vf_spec = pl.BlockSpec((tm * H, V_DIM), row)
        vf_sds = jax.ShapeDtypeStruct((T * H, V_DIM), F32)
    return pl.pallas_call(
        functools.partial(_inproj_kernel, q_scale=HEAD_DIM ** -0.5 * LOG2E, n_qk=n_qk,
                          transposed=t_attn is not None),
        grid=(T // tm,),
        in_specs=[pl.BlockSpec((tm, D), row),
                  pl.BlockSpec((D, n_cols), lambda i: (0, 0)),
                  pl.BlockSpec((tm, LANES), tab),
                  pl.BlockSpec((tm, LANES), tab)],
        out_specs=[qv_spec, kf_spec, spec, vf_spec, qv_spec, spec],
        out_shape=[qv_sds, kf_sds, out_sds(BF16), vf_sds, qv_sds, out_sds(F32)],
        compiler_params=_cparams("parallel"),
        name="in_proj_rope",
    )(x2d, w_bf, cos_t, sin_t)


def _split_maps(q):
    lane = lax.broadcasted_iota(jnp.int32, q.shape, 1)
    zero = jnp.zeros_like(q)
    return jnp.concatenate([jnp.where(lane < HEAD_DIM, q, zero), jnp.where(lane >= HEAD_DIM, q, zero)], axis=0)


def _qk(qs, kb):
    return lax.dot_general(qs, kb, (((1,), (1,)), ((), ())), preferred_element_type=F32)


def _subln(o, g, out_scale):
    ms = jnp.mean(o * o, axis=-1, keepdims=True)
    return o * lax.rsqrt(ms + RMS_EPS) * g * out_scale


def _attn_prompt_kernel(lam_ref, qt_ref, k_ref, vt_ref, g_ref, o_ref, qs_ref, acc_ref, s0_ref, s1_ref, m_ref,
                        *, tq, out_scale):
    i = pl.program_id(2)
    s_buf = (s0_ref, s1_ref)
    qt = qt_ref[...]
    row = lax.broadcasted_iota(jnp.int32, qt.shape, 0)
    zero = jnp.zeros_like(qt)
    qs_ref[:, 0:tq] = jnp.where(row < HEAD_DIM, qt, zero)
    qs_ref[:, tq:2 * tq] = jnp.where(row >= HEAD_DIM, qt, zero)
    acc_ref[...] = jnp.zeros(acc_ref.shape, F32)
    m_ref[...] = jnp.full(m_ref.shape, NEG_BIG, F32)
    ones_rows = jnp.ones((SUM_ROWS, tq), BF16)

    def scores(kt, slot):
        kb = k_ref[pl.ds(pl.multiple_of(kt * tq, tq), tq), :]
        s_buf[slot][...] = jnp.dot(kb, qs_ref[...], preferred_element_type=F32)

    def softmax_pv(kt, slot, masked):
        lhs = jnp.concatenate([vt_ref[kt], ones_rows], axis=0)
        for cb in range(2 * tq // MXU_N):
            cs = slice(MXU_N * cb, MXU_N * (cb + 1))
            st = s_buf[slot][:, cs]
            if masked:
                c = lax.broadcasted_iota(jnp.int32, st.shape, 0)
                r = (lax.broadcasted_iota(jnp.int32, st.shape, 1) + MXU_N * cb) % tq
                st = jnp.where((c // CHUNK) <= (r // CHUNK), st, NEG_BIG)
            m_prev = m_ref[:, cs]
            m_new = jnp.maximum(m_prev, jnp.max(st, axis=0, keepdims=True))
            alpha = jnp.exp2(m_prev - m_new)
            pt = jnp.exp2(st - m_new).astype(BF16)
            m_ref[:, cs] = m_new
            acc_ref[:, cs] = alpha * acc_ref[:, cs] + jnp.dot(lhs, pt, preferred_element_type=F32)

    def step(t, par):
        scores(t, par)
        softmax_pv(t - 1, 1 - par, False)

    scores(0, 0)

    def pair(j, carry):
        step(2 * j + 1, 1)
        step(2 * j + 2, 0)
        return carry

    lax.fori_loop(0, i // 2, pair, 0)

    @pl.when(i % 2 == 0)
    def _():
        softmax_pv(i, 0, True)

    @pl.when(i % 2 == 1)
    def _():
        step(i, 1)
        softmax_pv(i, 1, True)

    on = acc_ref[0:V_DIM, :] * (1.0 / acc_ref[V_DIM:V_DIM + 1, :])
    ot = on[:, 0:tq] - lam_ref[0] * on[:, tq:2 * tq]
    o_ref[...] = _subln(ot.T, g_ref[...], out_scale).astype(o_ref.dtype)


def _attn_prompt(lam, qt, k, vt, g, *, out_scale):
    B, L, W = k.shape
    n_t, H, _, t = qt.shape
    nq = L // t
    return pl.pallas_call(
        functools.partial(_attn_prompt_kernel, tq=t, out_scale=out_scale),
        grid=(B, H, nq),
        in_specs=[pl.BlockSpec(memory_space=pltpu.SMEM),
                  pl.BlockSpec((None, None, V_DIM, t), lambda b, h, i: (b * nq + i, h, 0, 0)),
                  pl.BlockSpec((None, L, V_DIM), lambda b, h, i: (b, 0, h)),
                  pl.BlockSpec((nq, None, V_DIM, t), lambda b, h, i: (b, h, 0, 0)),
                  pl.BlockSpec((1, V_DIM), lambda b, h, i: (0, 0))],
        out_specs=pl.BlockSpec((None, t, V_DIM), lambda b, h, i: (b, i, h)),
        out_shape=jax.ShapeDtypeStruct((B, L, W), BF16),
        scratch_shapes=[pltpu.VMEM((V_DIM, 2 * t), BF16), pltpu.VMEM((V_DIM + SUM_ROWS, 2 * t), F32),
                        pltpu.VMEM((t, 2 * t), F32), pltpu.VMEM((t, 2 * t), F32),
                        pltpu.VMEM((1, 2 * t), F32)],
        compiler_params=_cparams("parallel", "parallel", "parallel"),
        name="diff_attn_prompt",
    )(lam, qt, k, vt, g)


def _attn_sample_kernel(lam_ref, q_ref, kct_ref, vc_ref, kn_ref, vn_ref, g_ref, o_ref, *, past_len, n_heads,
                        out_scale):
    for h in range(n_heads):
        hs = slice(V_DIM * h, V_DIM * (h + 1))
        vc = vc_ref[pl.ds(h, past_len, stride=n_heads), :]
        o_ref[:, hs] = _attn_sample_head(lam_ref[0], q_ref[:, hs], kct_ref[hs, :], vc, kn_ref[:, hs], vn_ref[:, hs],
                                         g_ref[...], past_len, out_scale).astype(o_ref.dtype)


def _attn_sample_head(lam, q, kct, vc, kn, vn, g, past_len, out_scale):
    tq = q.shape[0]
    qs = _split_maps(q)
    s_c = jnp.dot(qs, kct.astype(BF16), preferred_element_type=F32)
    s_n = _qk(qs, kn)
    r = lax.broadcasted_iota(jnp.int32, s_n.shape, 0)
    c = lax.broadcasted_iota(jnp.int32, s_n.shape, 1)
    r = jnp.where(r >= tq, r - tq, r)
    s_n = jnp.where(((past_len + c) // CHUNK) <= ((past_len + r) // CHUNK), s_n, NEG_BIG)
    m = jnp.maximum(jnp.max(s_c, axis=-1, keepdims=True), jnp.max(s_n, axis=-1, keepdims=True))
    p_c = jnp.exp2(s_c - m)
    p_n = jnp.exp2(s_n - m)
    inv_l = 1.0 / (jnp.sum(p_c, axis=-1, keepdims=True) + jnp.sum(p_n, axis=-1, keepdims=True))
    p_c = p_c * inv_l
    p_n = p_n * inv_l
    w_c = p_c[0:tq] - lam * p_c[tq:2 * tq]
    w_n = p_n[0:tq] - lam * p_n[tq:2 * tq]
    o = (jnp.dot(w_c.astype(BF16), vc.astype(BF16), preferred_element_type=F32)
         + jnp.dot(w_n.astype(BF16), vn, preferred_element_type=F32))
    return _subln(o, g, out_scale)


def _attn_sample(lam, q, cache_kt, cache_v, k_new, v_new, g, *, out_scale):
    B, S, W = q.shape
    P = cache_kt.shape[1]
    H = W // V_DIM
    blk_new = pl.BlockSpec((None, S, W), lambda b: (b, 0, 0))
    return pl.pallas_call(
        functools.partial(_attn_sample_kernel, past_len=P, n_heads=H, out_scale=out_scale),
        grid=(B,),
        in_specs=[pl.BlockSpec(memory_space=pltpu.SMEM), blk_new,
                  pl.BlockSpec((W, P), lambda b: (b, 0)),
                  pl.BlockSpec((P * H, V_DIM), lambda b: (b, 0)),
                  blk_new, blk_new, pl.BlockSpec((1, V_DIM), lambda b: (0, 0))],
        out_specs=blk_new,
        out_shape=jax.ShapeDtypeStruct((B, S, W), BF16),
        compiler_params=_cparams("parallel"),
        name="diff_attn_sample",
    )(lam, q, cache_kt, cache_v, k_new, v_new, g)


def _cmul(ar, ai, br, bi):
    return ar * br - ai * bi, ar * bi + ai * br


def _s5_kernel(*refs, n_steps, chained, lane_chunk):
    if chained:
        (u_ref, perm_ref, permt_ref, wx_ref, cb_ref, d_ref, pre_ref, pim_ref, gre_ref, gim_ref,
         wglu_ref, ng_ref, y_ref, hre_ref, him_ref, xre, xim, cre, cim) = refs
    else:
        (u_ref, perm_ref, permt_ref, wx_ref, cb_ref, d_ref, pre_ref, pim_ref, gre_ref, gim_ref,
         wglu_ref, ng_ref, h0re_ref, h0im_ref, y_ref, hre_ref, him_ref, xre, xim) = refs
    R = n_steps
    n_state = xre.shape[1]
    n_blk = wx_ref.shape[0]
    blk_in = wx_ref.shape[1]
    blk_st = wx_ref.shape[2] // 2

    u = u_ref[...]
    u_hi = u.astype(BF16)
    u_lo = (u - u_hi.astype(F32)).astype(BF16)
    perm = perm_ref[...]
    up = (jnp.dot(perm, u_hi, preferred_element_type=F32) + jnp.dot(perm, u_lo, preferred_element_type=F32))
    upb = up.astype(BF16)
    for j in range(n_blk):
        xj = jnp.dot(upb[:, blk_in * j:blk_in * (j + 1)], wx_ref[j], preferred_element_type=F32)
        xre[:, blk_st * j:blk_st * (j + 1)] = xj[:, :blk_st]
        xim[:, blk_st * j:blk_st * (j + 1)] = xj[:, blk_st:]

    n_chunks = n_state // lane_chunk
    e_re, e_im = [], []
    for c in range(n_chunks):
        cs = slice(lane_chunk * c, lane_chunk * (c + 1))
        ar = jnp.broadcast_to(pre_ref[0:1, cs], (SUBLANES, lane_chunk))
        ai = jnp.broadcast_to(pim_ref[0:1, cs], (SUBLANES, lane_chunk))

        def scan_body(r, h, cs=cs, ar=ar, ai=ai):
            hr, hi = h
            rows = pl.ds(pl.multiple_of(r * SUBLANES, SUBLANES), SUBLANES)
            nr = ar * hr - ai * hi + xre[rows, cs]
            ni = ar * hi + ai * hr + xim[rows, cs]
            xre[rows, cs] = nr
            xim[rows, cs] = ni
            return nr, ni

        z = jnp.zeros((SUBLANES, lane_chunk), F32)
        er, ei = lax.fori_loop(0, R, scan_body, (z, z), unroll=4)
        e_re.append(er)
        e_im.append(ei)
    e_re = jnp.concatenate(e_re, axis=1)
    e_im = jnp.concatenate(e_im, axis=1)

    g1r = jnp.broadcast_to(gre_ref[0:1, :], e_re.shape)
    g1i = jnp.broadcast_to(gim_ref[0:1, :], e_re.shape)
    if chained:
        t = pl.program_id(1)

        @pl.when(t == 0)
        def _():
            cre[...] = jnp.zeros(cre.shape, F32)
            cim[...] = jnp.zeros(cim.shape, F32)

        sub = lax.broadcasted_iota(jnp.int32, e_re.shape, 0)
        c_r = cre[...]
        c_i = cim[...]
        inj_r, inj_i = _cmul(g1r, g1i, c_r, c_i)
        xr = e_re + jnp.where(sub == 0, inj_r, 0.0)
        xi = e_im + jnp.where(sub == 0, inj_i, 0.0)
        for lvl, k in enumerate((1, 2, 4)):
            gr = jnp.broadcast_to(gre_ref[lvl:lvl + 1, :], e_re.shape)
            gi = jnp.broadcast_to(gim_ref[lvl:lvl + 1, :], e_re.shape)
            sr = jnp.where(sub >= k, pltpu.roll(xr, k, 0), 0.0)
            si = jnp.where(sub >= k, pltpu.roll(xi, k, 0), 0.0)
            tr, ti = _cmul(gr, gi, sr, si)
            xr = xr + tr
            xi = xi + ti
        hend_r, hend_i = xr, xi
        hc_r = jnp.where(sub == 0, c_r, pltpu.roll(hend_r, 1, 0))
        hc_i = jnp.where(sub == 0, c_i, pltpu.roll(hend_i, 1, 0))
        last_r = jnp.broadcast_to(hend_r[SUBLANES - 1:SUBLANES, :], e_re.shape)
        last_i = jnp.broadcast_to(hend_i[SUBLANES - 1:SUBLANES, :], e_re.shape)
        cre[...] = last_r
        cim[...] = last_i
        hre_ref[...] = last_r[0:1, :]
        him_ref[...] = last_i[0:1, :]
    else:
        hc_r = h0re_ref[...]
        hc_i = h0im_ref[...]
        tr, ti = _cmul(g1r, g1i, hc_r, hc_i)
        hre_ref[...] = e_re + tr
        him_ref[...] = e_im + ti

    for c in range(n_chunks):
        cs = slice(lane_chunk * c, lane_chunk * (c + 1))
        hcr = hc_r[:, cs]
        hci = hc_i[:, cs]

        def fix_body(r, carry, cs=cs, hcr=hcr, hci=hci):
            rows = pl.ds(pl.multiple_of(r * SUBLANES, SUBLANES), SUBLANES)
            pr = jnp.broadcast_to(pre_ref[pl.ds(r, 1), cs], hcr.shape)
            pi = jnp.broadcast_to(pim_ref[pl.ds(r, 1), cs], hcr.shape)
            tr, ti = _cmul(pr, pi, hcr, hci)
            xre[rows, cs] = xre[rows, cs] + tr
            xim[rows, cs] = xim[rows, cs] + ti
            return carry

        lax.fori_loop(0, R, fix_body, 0, unroll=4)

    ys = []
    for j in range(n_blk):
        st = slice(blk_st * j, blk_st * (j + 1))
        ys.append(jnp.dot(xre[:, st].astype(BF16), cb_ref[j, 0:blk_st, :], preferred_element_type=F32)
                  + jnp.dot(xim[:, st].astype(BF16), cb_ref[j, blk_st:, :], preferred_element_type=F32))
    y = jnp.concatenate(ys, axis=1) + d_ref[...] * up
    y = 0.5 * y * (1.0 + jnp.tanh(math.sqrt(2.0 / math.pi) * (y + 0.044715 * (y * y * y))))
    gl = jnp.dot(y.astype(BF16), wglu_ref[...], preferred_element_type=F32)
    half = gl.shape[1] // 2
    z = gl[:, :half] * (1.0 / (1.0 + jnp.exp(-gl[:, half:])))
    z = z * lax.rsqrt(jnp.mean(z * z, axis=-1, keepdims=True) + RMS_EPS) * ng_ref[...]
    y_ref[...] = jnp.dot(permt_ref[...], z.astype(BF16), preferred_element_type=F32).astype(y_ref.dtype)


def _s5_tables(a_re, a_im, log_dt, b_re, b_im, c_re, c_im, n_steps):
    G, N = a_re.shape
    C = b_re.shape[2]
    dt = jnp.exp(log_dt.astype(F32))[:, None]
    a_re = a_re.astype(F32)
    a_im = a_im.astype(F32)

    def power(k):
        mag = jnp.exp(k * a_re * dt)
        return mag * jnp.cos(k * a_im * dt), mag * jnp.sin(k * a_im * dt)

    ab_re, ab_im = power(1.0)
    nr, ni = ab_re - 1.0, ab_im
    den = a_re * a_re + a_im * a_im
    cf_re = (nr * a_re + ni * a_im) / den
    cf_im = (ni * a_re - nr * a_im) / den
    ks = jnp.arange(1, n_steps + 1, dtype=F32)[:, None, None]
    p_re, p_im = power(ks)
    g_re, g_im = zip(*[power(float(m * n_steps)) for m in (1, 2, 4)])
    pad = jnp.zeros((SUBLANES - 3, G * N), F32)
    g_re = jnp.concatenate([jnp.stack(g_re).reshape(3, G * N), pad])
    g_im = jnp.concatenate([jnp.stack(g_im).reshape(3, G * N), pad])
    wx_re = cf_re[..., None] * b_re - cf_im[..., None] * b_im
    wx_im = cf_re[..., None] * b_im + cf_im[..., None] * b_re
    gpb = LANES // C
    nb = G // gpb
    eye = jnp.eye(gpb, dtype=F32)

    def in_block(w):
        w = w.reshape(nb, gpb, N, C)
        return jnp.einsum('jgnc,gh->jgchn', w, eye).reshape(nb, gpb * C, gpb * N)

    def out_block(w):
        w = w.reshape(nb, gpb, C, N)
        return jnp.einsum('jgcn,gh->jgnhc', w, eye).reshape(nb, gpb * N, gpb * C)

    wx = jnp.concatenate([in_block(wx_re), in_block(wx_im)], axis=2).astype(BF16)
    cb = jnp.concatenate([out_block(c_re.astype(F32)), out_block(-c_im.astype(F32))], axis=1).astype(BF16)
    return (p_re.reshape(n_steps, G * N), p_im.reshape(n_steps, G * N), g_re, g_im, wx, cb)


def _perm_matrices(n_steps):
    tt = SUBLANES * n_steps
    dst = jnp.arange(tt)
    src = (dst % SUBLANES) * n_steps + dst // SUBLANES
    perm = (src[:, None] == jnp.arange(tt)[None, :]).astype(BF16)
    return perm, perm.T


def _s5_mixer(u, tables, d, w_glu_bf, norm_g, *, n_steps, h0=None):
    B, L, W = u.shape
    p_re, p_im, g_re, g_im, wx, cb = tables
    n_state = p_re.shape[1]
    tt = SUBLANES * n_steps
    perm, permt = _perm_matrices(n_steps)
    chained = h0 is None
    const2 = lambda *_: (0, 0)
    const3 = lambda *_: (0, 0, 0)
    common_specs = [pl.BlockSpec(perm.shape, const2), pl.BlockSpec(perm.shape, const2),
                    pl.BlockSpec(wx.shape, const3), pl.BlockSpec(cb.shape, const3),
                    pl.BlockSpec((1, W), const2),
                    pl.BlockSpec(p_re.shape, const2), pl.BlockSpec(p_im.shape, const2),
                    pl.BlockSpec(g_re.shape, const2), pl.BlockSpec(g_im.shape, const2),
                    pl.BlockSpec(w_glu_bf.shape, const2), pl.BlockSpec((1, W), const2)]
    common_args = (perm, permt, wx, cb, d, p_re, p_im, g_re, g_im, w_glu_bf, norm_g)
    state_scratch = [pltpu.VMEM((tt, n_state), F32), pltpu.VMEM((tt, n_state), F32)]
    kern = functools.partial(_s5_kernel, n_steps=n_steps, chained=chained, lane_chunk=512)
    if chained:
        y, hre, him = pl.pallas_call(
            kern,
            grid=(B, L // tt),
            in_specs=[pl.BlockSpec((None, tt, W), lambda b, t: (b, t, 0))] + common_specs,
            out_specs=[pl.BlockSpec((None, tt, W), lambda b, t: (b, t, 0)),
                       pl.BlockSpec((None, 1, n_state), lambda b, t: (b, 0, 0)),
                       pl.BlockSpec((None, 1, n_state), lambda b, t: (b, 0, 0))],
            out_shape=[jax.ShapeDtypeStruct((B, L, W), BF16),
                       jax.ShapeDtypeStruct((B, 1, n_state), F32),
                       jax.ShapeDtypeStruct((B, 1, n_state), F32)],
            scratch_shapes=state_scratch + [pltpu.VMEM((SUBLANES, n_state), F32)] * 2,
            compiler_params=_cparams("parallel", "arbitrary"),
            name="s5_mixer_chained",
        )(u, *common_args)
        return y, hre.reshape(B, n_state), him.reshape(B, n_state)
    u2 = u.reshape(B * L, W)
    h0re, h0im = h0
    y, hre, him = pl.pallas_call(
        kern,
        grid=(B // SUBLANES,),
        in_specs=[pl.BlockSpec((tt, W), lambda i: (i, 0))] + common_specs
        + [pl.BlockSpec((SUBLANES, n_state), lambda i: (i, 0))] * 2,
        out_specs=[pl.BlockSpec((tt, W), lambda i: (i, 0)),
                   pl.BlockSpec((SUBLANES, n_state), lambda i: (i, 0)),
                   pl.BlockSpec((SUBLANES, n_state), lambda i: (i, 0))],
        out_shape=[jax.ShapeDtypeStruct((B * L, W), BF16),
                   jax.ShapeDtypeStruct((B, n_state), F32),
                   jax.ShapeDtypeStruct((B, n_state), F32)],
        scratch_shapes=state_scratch,
        compiler_params=_cparams("parallel"),
        name="s5_mixer_batched",
    )(u2, *common_args, h0re, h0im)
    return y.reshape(B, L, W), hre, him


def _layer_norm(z, g, b):
    mu = jnp.mean(z, axis=-1, keepdims=True)
    zc = z - mu
    var = jnp.mean(zc * zc, axis=-1, keepdims=True)
    return zc * lax.rsqrt(var + LN_EPS) * g + b


def _router_gates(x1, hi, rw_ref, rb_ref):
    lo = (x1 - hi.astype(F32)).astype(BF16)
    both = jnp.dot(hi, rw_ref[...], preferred_element_type=F32)
    lg = (both[:, :LANES] + both[:, LANES:]
          + jnp.dot(lo, rw_ref[:, :LANES], preferred_element_type=F32)) + rb_ref[...]
    lane = lax.broadcasted_iota(jnp.int32, lg.shape, 1)
    big = jnp.int32(LANES)
    is_grp = (lane >= N_EXPERTS) & (lane < N_EXPERTS + N_EXPERT_GROUPS)
    gl = jnp.where(is_grp, lg, NEG_BIG)
    gexp = jnp.where(is_grp, jnp.exp(gl - jnp.max(gl, axis=-1, keepdims=True)), 0.0)
    gprob = gexp / jnp.sum(gexp, axis=-1, keepdims=True)
    g_w = jnp.max(gprob, axis=-1, keepdims=True)
    g_idx = jnp.min(jnp.where(is_grp & (gprob == g_w), lane - N_EXPERTS, big), axis=-1, keepdims=True)
    valid = (lane < N_EXPERTS) & ((lane // EXPERTS_PER_GROUP) == g_idx)
    el = jnp.where(valid, lg, NEG_BIG)
    eexp = jnp.where(valid, jnp.exp(el - jnp.max(el, axis=-1, keepdims=True)), 0.0)
    eprob = eexp / jnp.sum(eexp, axis=-1, keepdims=True)
    w1 = jnp.max(jnp.where(valid, eprob, -1.0), axis=-1, keepdims=True)
    i1 = jnp.min(jnp.where(valid & (eprob == w1), lane, big), axis=-1, keepdims=True)
    rest = valid & (lane != i1)
    w2 = jnp.max(jnp.where(rest, eprob, -1.0), axis=-1, keepdims=True)
    i2 = jnp.min(jnp.where(rest & (eprob == w2), lane, big), axis=-1, keepdims=True)
    den = w1 + w2
    return (jnp.where(lane == i1, g_w * (w1 / den), 0.0)
            + jnp.where(lane == i2, g_w * (w2 / den), 0.0))


def _tail_kernel(att_ref, ssm_ref, x_ref, wa_ref, wb_ref, g1_ref, b1_ref, rw_ref, rb_ref,
                 wg_ref, wu_ref, wd_ref, g2_ref, b2_ref, o_ref, acc_ref, *, alpha):
    mix = (jnp.dot(att_ref[...], wa_ref[...], preferred_element_type=F32)
           + jnp.dot(ssm_ref[...], wb_ref[...], preferred_element_type=F32))
    x1 = _layer_norm(alpha * x_ref[...] + mix, g1_ref[...], b1_ref[...])
    xb = x1.astype(BF16)
    gates = _router_gates(x1, xb, rw_ref, rb_ref)
    for e in range(wg_ref.shape[0]):
        hg = jnp.dot(xb, wg_ref[e], preferred_element_type=F32)
        hu = jnp.dot(xb, wu_ref[e], preferred_element_type=F32)
        h = hg * (1.0 / (1.0 + jnp.exp(-hg))) * hu * gates[:, e:e + 1]
        contrib = jnp.dot(h.astype(BF16), wd_ref[e], preferred_element_type=F32)
        if e == 0:
            acc_ref[...] = contrib
        else:
            acc_ref[...] += contrib
    o_ref[...] = _layer_norm(alpha * x1 + acc_ref[...], g2_ref[...], b2_ref[...])


def _tail(att, ssm, x2d, wa, wb, g1, b1, rw, rb, wg, wu, wd, g2, b2, *, tm, alpha):
    T, D = x2d.shape
    Wh = att.shape[1]
    row = lambda i: (i, 0)
    const2 = lambda i: (0, 0)
    const3 = lambda i: (0, 0, 0)
    once = pl.Buffered(1)
    vec = pl.BlockSpec((1, D), const2)
    return pl.pallas_call(
        functools.partial(_tail_kernel, alpha=alpha),
        grid=(T // tm,),
        in_specs=[pl.BlockSpec((tm, Wh), row), pl.BlockSpec((tm, Wh), row), pl.BlockSpec((tm, D), row),
                  pl.BlockSpec(wa.shape, const2, pipeline_mode=once),
                  pl.BlockSpec(wb.shape, const2, pipeline_mode=once), vec, vec,
                  pl.BlockSpec(rw.shape, const2, pipeline_mode=once), pl.BlockSpec((1, LANES), const2),
                  pl.BlockSpec(wg.shape, const3, pipeline_mode=once),
                  pl.BlockSpec(wu.shape, const3, pipeline_mode=once),
                  pl.BlockSpec(wd.shape, const3, pipeline_mode=once), vec, vec],
        out_specs=pl.BlockSpec((tm, D), row),
        out_shape=jax.ShapeDtypeStruct((T, D), F32),
        scratch_shapes=[pltpu.VMEM((tm, D), F32)],
        compiler_params=_cparams("parallel"),
        name="out_proj_router_moe",
    )(att, ssm, x2d, wa, wb, g1, b1, rw, rb, wg, wu, wd, g2, b2)


def _rope_tables(pos):
    half = HEAD_DIM // 2
    inv = 1.0 / (ROPE_THETA ** (jnp.arange(half, dtype=F32) / half))
    ang = pos.astype(F32)[:, None] * inv[None, :]
    cos = jnp.cos(ang)
    sin = jnp.sin(ang)
    cos_t = jnp.tile(cos, (1, LANES // half))
    sin_t = jnp.tile(jnp.concatenate([-sin, sin], axis=1), (1, LANES // HEAD_DIM))
    return cos_t, sin_t


def _row_tile(n, pref):
    t = min(n, pref)
    while n % t:
        t //= 2
    return t


def kernel(x_prompt, x_sample, cache_k, cache_v, state_ssm_re, state_ssm_im, w_in, lam_q1, lam_k1, lam_q2, lam_k2, subln_g, ssm_a_re, ssm_a_im, ssm_log_dt, ssm_b_re, ssm_b_im, ssm_c_re, ssm_c_im, ssm_d, w_glu, ssm_norm_g, w_out, ln1_g, ln1_b, w_grp, b_grp, w_rt, b_rt, w_gate, w_up, w_down, ln2_g, ln2_b):
    depth = w_in.shape[0]
    assert depth == 1, "single-layer step"
    B, L, D = x_prompt.shape
    Bs, S, _ = x_sample.shape
    P = cache_k.shape[2]
    H = cache_k.shape[3]
    n_qk = H * 2 * HEAD_DIM
    G, N = ssm_a_re.shape[1], ssm_a_re.shape[2]
    alpha = (2.0 * depth) ** 0.25
    lam_init = 0.8 - 0.6 * math.exp(-0.3 * 0)
    out_scale = 1.0 - lam_init
    l = 0

    w_in_bf = w_in[l].astype(BF16)
    lam = (jnp.exp(jnp.sum(lam_q1[l].astype(F32) * lam_k1[l].astype(F32)))
           - jnp.exp(jnp.sum(lam_q2[l].astype(F32) * lam_k2[l].astype(F32))) + lam_init).reshape(1).astype(F32)
    g_sub = subln_g[l].astype(F32).reshape(1, V_DIM)
    w_glu_bf = w_glu[l].astype(BF16)
    ssm_w = D - H * V_DIM
    d_row = ssm_d[l].astype(F32).reshape(1, ssm_w)
    ng_row = ssm_norm_g[l].astype(F32).reshape(1, ssm_w)
    wa = w_out[l, :H * V_DIM].astype(BF16)
    wb = w_out[l, H * V_DIM:].astype(BF16)
    ln1g, ln1b = ln1_g[l].astype(F32).reshape(1, D), ln1_b[l].astype(F32).reshape(1, D)
    ln2g, ln2b = ln2_g[l].astype(F32).reshape(1, D), ln2_b[l].astype(F32).reshape(1, D)
    w_r = jnp.concatenate([w_rt[l].astype(F32), w_grp[l].astype(F32),
                           jnp.zeros((D, LANES - N_EXPERTS - N_EXPERT_GROUPS), F32)], axis=1)
    rhi = w_r.astype(BF16)
    rw = jnp.concatenate([rhi, (w_r - rhi.astype(F32)).astype(BF16)], axis=1)
    rb =jnp.concatenate([b_rt[l].astype(F32), b_grp[l].astype(F32),
                          jnp.zeros((LANES - N_EXPERTS - N_EXPERT_GROUPS,), F32)]).reshape(1, LANES)
    wg, wu, wd = w_gate[l].astype(BF16), w_up[l].astype(BF16), w_down[l].astype(BF16)

    def tail(att, ssm, x2d):
        return _tail(att, ssm, x2d, wa, wb, ln1g, ln1b, rw, rb, wg, wu, wd, ln2g, ln2b,
                     tm=_row_tile(x2d.shape[0], 512), alpha=alpha)

    cos_p, sin_p = _rope_tables(jnp.arange(L))
    xp2 = x_prompt.reshape(B * L, D)
    qp, kpf, kpb, vpf, vpb, up = _in_proj(xp2, w_in_bf, cos_p, sin_p, tm=_row_tile(L, 512),
                                          t_attn=_row_tile(L, 512))
    r3 = lambda a: a.reshape(B, L, -1)
    att_p = _attn_prompt(lam, qp, r3(kpb), vpb, g_sub, out_scale=out_scale)
    n_steps_p = _row_tile(L, 256) // SUBLANES
    tabs_p = _s5_tables(ssm_a_re[l], ssm_a_im[l], ssm_log_dt[l], ssm_b_re[l].astype(F32), ssm_b_im[l].astype(F32),
                        ssm_c_re[l], ssm_c_im[l], n_steps_p)
    ssm_p, hre_p, him_p = _s5_mixer(r3(up), tabs_p, d_row, w_glu_bf, ng_row, n_steps=n_steps_p)
    y_prompt = tail(att_p.reshape(B * L, -1), ssm_p.reshape(B * L, -1), xp2).reshape(B, L, D)

    cos_s, sin_s = _rope_tables(P + jnp.arange(S))
    cos_s, sin_s = jnp.tile(cos_s, (Bs, 1)), jnp.tile(sin_s, (Bs, 1))
    xs2 = x_sample.reshape(Bs * S, D)
    qs, ksf, ksb, vsf, vsb, us = _in_proj(xs2, w_in_bf, cos_s, sin_s, tm=_row_tile(Bs * S, 512))
    s3 = lambda a: a.reshape(Bs, S, -1)
    cache_kt = jnp.transpose(cache_k[l], (0, 2, 3, 4, 1)).reshape(Bs * n_qk, P)
    att_s = _attn_sample(lam, s3(qs), cache_kt, cache_v[l].reshape(Bs * P * H, V_DIM),
                         s3(ksb), s3(vsb), g_sub, out_scale=out_scale)
    tabs_s = _s5_tables(ssm_a_re[l], ssm_a_im[l], ssm_log_dt[l], ssm_b_re[l].astype(F32), ssm_b_im[l].astype(F32),
                        ssm_c_re[l], ssm_c_im[l], S)
    h0 = (state_ssm_re[l].astype(F32).reshape(Bs, G * N), state_ssm_im[l].astype(F32).reshape(Bs, G * N))
    ssm_s, hre_s, him_s = _s5_mixer(s3(us), tabs_s, d_row, w_glu_bf, ng_row, n_steps=S, h0=h0)
    y_sample = tail(att_s.reshape(Bs * S, -1), ssm_s.reshape(Bs * S, -1), xs2).reshape(Bs, S, D)

    return (y_prompt, y_sample,
            jnp.transpose(kpf.reshape(1, B, H, 2, HEAD_DIM, L), (0, 1, 5, 2, 3, 4)), vpf.reshape(1, B, L, H, V_DIM),
            hre_p.reshape(1, B, G, N), him_p.reshape(1, B, G, N),
            ksf.reshape(1, Bs, S, H, 2, HEAD_DIM), vsf.reshape(1, Bs, S, H, V_DIM),
            hre_s.reshape(1, Bs, G, N), him_s.reshape(1, Bs, G, N))
```

```python
import functools
import math

import jax
import jax.numpy as jnp
from jax import lax
from jax.experimental import pallas as pl
from jax.experimental.pallas import tpu as pltpu

F32 = jnp.float32
BF16 = jnp.bfloat16

HEAD_DIM = 64
V_DIM = 2 * HEAD_DIM
CHUNK = 64
SSM_GROUP = 16
SSM_STATE = 64
N_EXPERT_GROUPS = 4
EXPERTS_PER_GROUP = 4
N_EXPERTS = N_EXPERT_GROUPS * EXPERTS_PER_GROUP
ROPE_THETA = 10000.0
LN_EPS = 1e-5
RMS_EPS = 1e-6
LANES = 128
SUBLANES = 8
MXU_N = 256
NEG_BIG = -1e30
SUM_ROWS = 16
LOG2E = 1.4426950408889634
VMEM_LIMIT = 52 * 1024 * 1024


def _cparams(*sem):
    return pltpu.CompilerParams(dimension_semantics=sem, vmem_limit_bytes=VMEM_LIMIT)


def _store_transposed_tiles(ref, a):
    n_t, n_h, _, t = ref.shape
    at = a.T
    for n in range(n_t):
        for h in range(n_h):
            ref[n, h] = at[V_DIM * h:V_DIM * (h + 1), t * n:t * (n + 1)].astype(ref.dtype)


def _inproj_kernel(x_ref, w_ref, cos_ref, sin_ref, q_ref, kf_ref, kb_ref, vf_ref, vb_ref, u_ref,
                   *, q_scale, n_qk, transposed):
    xb = x_ref[...].astype(BF16)
    cos = cos_ref[...]
    sin = sin_ref[...]
    lane = lax.broadcasted_iota(jnp.int32, cos.shape, 1)
    first = (lane & (HEAD_DIM - 1)) < HEAD_DIM // 2

    def rope(t):
        rot = jnp.where(first, pltpu.roll(t, LANES - HEAD_DIM // 2, 1), pltpu.roll(t, HEAD_DIM // 2, 1))
        return t * cos + rot * sin

    pq = jnp.dot(xb, w_ref[:, 0:n_qk], preferred_element_type=F32)
    qr = jnp.concatenate([rope(pq[:, LANES * j:LANES * (j + 1)]) for j in range(n_qk // LANES)], axis=1) * q_scale
    if transposed:
        _store_transposed_tiles(q_ref, qr)
    else:
        q_ref[...] = qr.astype(BF16)
    pk = jnp.dot(xb, w_ref[:, n_qk:2 * n_qk], preferred_element_type=F32)
    kr = jnp.concatenate([rope(pk[:, LANES * j:LANES * (j + 1)]) for j in range(n_qk // LANES)], axis=1)
    kb_ref[...] = kr.astype(BF16)
    n_v = u_ref.shape[1]
    pv = jnp.dot(xb, w_ref[:, 2 * n_qk:2 * n_qk + n_v], preferred_element_type=F32)
    if transposed:
        kf_ref[...] = kr.T
        n_h = n_v // V_DIM
        for h in range(n_h):
            vf_ref[pl.ds(h, pv.shape[0], stride=n_h), :] = pv[:, V_DIM * h:V_DIM * (h + 1)]
        _store_transposed_tiles(vb_ref, pv)
    else:
        kf_ref[...] = kr
        vf_ref[...] = pv
        vb_ref[...] = pv.astype(BF16)
    u_ref[...] = jnp.dot(xb, w_ref[:, 2 * n_qk + n_v:], preferred_element_type=F32)


def _in_proj(x2d, w_bf, cos_t, sin_t, *, tm, t_attn=None):
    T, D = x2d.shape
    n_cols = w_bf.shape[1]
    n_qk = n_cols // 4
    n_tab = cos_t.shape[0] // tm
    row = lambda i: (i, 0)
    tab = lambda i: (i % n_tab, 0)
    out_sds = lambda dt: jax.ShapeDtypeStruct((T, n_qk), dt)
    spec = pl.BlockSpec((tm, n_qk), row)
    qv_spec, qv_sds = spec, out_sds(BF16)
    kf_spec, kf_sds, vf_spec, vf_sds = spec, out_sds(F32), spec, out_sds(F32)
    if t_attn is not None:
        H = n_qk // V_DIM
        qv_spec = pl.BlockSpec((tm // t_attn, H, V_DIM, t_attn), lambda i: (i, 0, 0, 0))
        qv_sds = jax.ShapeDtypeStruct((T // t_attn, H, V_DIM, t_attn), BF16)
        kf_spec = pl.BlockSpec((None, n_qk, tm), lambda i: (i // n_tab, 0, i % n_tab))
        kf_sds = jax.ShapeDtypeStruct((T // cos_t.shape[0], n_qk, cos_t.shape[0]), F32)
        vf_spec = pl.BlockSpec((tm * H, V_DIM), row)
        vf_sds = jax.ShapeDtypeStruct((T * H, V_DIM), F32)
    return pl.pallas_call(
        functools.partial(_inproj_kernel, q_scale=HEAD_DIM ** -0.5 * LOG2E, n_qk=n_qk,
                          transposed=t_attn is not None),
        grid=(T // tm,),
        in_specs=[pl.BlockSpec((tm, D), row),
                  pl.BlockSpec((D, n_cols), lambda i: (0, 0)),
                  pl.BlockSpec((tm, LANES), tab),
                  pl.BlockSpec((tm, LANES), tab)],
        out_specs=[qv_spec, kf_spec, spec, vf_spec, qv_spec, spec],
        out_shape=[qv_sds, kf_sds, out_sds(BF16), vf_sds, qv_sds, out_sds(F32)],
        compiler_params=_cparams("parallel"),
        name="in_proj_rope",
    )(x2d, w_bf, cos_t, sin_t)


def _split_maps(q):
    lane = lax.broadcasted_iota(jnp.int32, q.shape, 1)
    zero = jnp.zeros_like(q)
    return jnp.concatenate([jnp.where(lane < HEAD_DIM, q, zero), jnp.where(lane >= HEAD_DIM, q, zero)], axis=0)


def _qk(qs, kb):
    return lax.dot_general(qs, kb, (((1,), (1,)), ((), ())), preferred_element_type=F32)


def _subln(o, g, out_scale):
    ms = jnp.mean(o * o, axis=-1, keepdims=True)
    return o * lax.rsqrt(ms + RMS_EPS) * g * out_scale


def _attn_prompt_kernel(lam_ref, qt_ref, k_ref, vt_ref, g_ref, o_ref, qs_ref, acc_ref, s0_ref, s1_ref, m_ref,
                        *, tq, out_scale):
    i = pl.program_id(2)
    s_buf = (s0_ref, s1_ref)
    n_cb = 2 * tq // MXU_N
    qt = qt_ref[...]
    row = lax.broadcasted_iota(jnp.int32, qt.shape, 0)
    zero = jnp.zeros_like(qt)
    q_maps = (jnp.where(row < HEAD_DIM, qt, zero), jnp.where(row >= HEAD_DIM, qt, zero))
    for cb in range(n_cb):
        j = cb % (n_cb // 2)
        qs_ref[cb] = q_maps[cb // (n_cb // 2)][:, MXU_N * j:MXU_N * (j + 1)]
    acc_ref[...] = jnp.zeros(acc_ref.shape, F32)
    m_ref[...] = jnp.full(m_ref.shape, NEG_BIG, F32)
    ones_rows = jnp.ones((SUM_ROWS, tq), BF16)

    def scores(kt, slot):
        kb = k_ref[pl.ds(pl.multiple_of(kt * tq, tq), tq), :]
        for cb in range(n_cb):
            s_buf[slot][cb] = jnp.dot(kb, qs_ref[cb], preferred_element_type=F32)

    def softmax_pv(kt, slot, masked):
        lhs = jnp.concatenate([vt_ref[kt], ones_rows], axis=0)
        for cb in range(n_cb):
            st = s_buf[slot][cb]
            if masked:
                c = lax.broadcasted_iota(jnp.int32, st.shape, 0)
                r = (lax.broadcasted_iota(jnp.int32, st.shape, 1) + MXU_N * cb) % tq
                st = jnp.where((c // CHUNK) <= (r // CHUNK), st, NEG_BIG)
            m_prev = m_ref[cb]
            m_new = jnp.maximum(m_prev, jnp.max(st, axis=0, keepdims=True))
            alpha = jnp.exp2(m_prev - m_new)
            pt = jnp.exp2(st - m_new).astype(BF16)
            m_ref[cb] = m_new
            acc_ref[cb] = alpha * acc_ref[cb] + jnp.dot(lhs, pt, preferred_element_type=F32)

    def step(t, par):
        scores(t, par)
        softmax_pv(t - 1, 1 - par, False)

    scores(0, 0)

    def pair(j, carry):
        step(2 * j + 1, 1)
        step(2 * j + 2, 0)
        return carry

    lax.fori_loop(0, i // 2, pair, 0)

    @pl.when(i % 2 == 0)
    def _():
        softmax_pv(i, 0, True)

    @pl.when(i % 2 == 1)
    def _():
        step(i, 1)
        softmax_pv(i, 1, True)

    def normalized(cb):
        return acc_ref[cb, 0:V_DIM, :] * (1.0 / acc_ref[cb, V_DIM:V_DIM + 1, :])

    half = n_cb // 2
    ot = jnp.concatenate([normalized(j) - lam_ref[0] * normalized(j + half) for j in range(half)], axis=1)
    o_ref[...] = _subln(ot.T, g_ref[...], out_scale).astype(o_ref.dtype)


def _attn_prompt(lam, qt, k, vt, g, *, out_scale):
    B, L, W = k.shape
    n_t, H, _, t = qt.shape
    nq = L // t
    return pl.pallas_call(
        functools.partial(_attn_prompt_kernel, tq=t, out_scale=out_scale),
        grid=(B, H, nq),
        in_specs=[pl.BlockSpec(memory_space=pltpu.SMEM),
                  pl.BlockSpec((None, None, V_DIM, t), lambda b, h, i: (b * nq + i, h, 0, 0)),
                  pl.BlockSpec((None, L, V_DIM), lambda b, h, i: (b, 0, h)),
                  pl.BlockSpec((nq, None, V_DIM, t), lambda b, h, i: (b, h, 0, 0)),
                  pl.BlockSpec((1, V_DIM), lambda b, h, i: (0, 0))],
        out_specs=pl.BlockSpec((None, t, V_DIM), lambda b, h, i: (b, i, h)),
        out_shape=jax.ShapeDtypeStruct((B, L, W), BF16),
        scratch_shapes=[pltpu.VMEM((2 * t // MXU_N, V_DIM, MXU_N), BF16),
                        pltpu.VMEM((2 * t // MXU_N, V_DIM + SUM_ROWS, MXU_N), F32),
                        pltpu.VMEM((2 * t // MXU_N, t, MXU_N), F32), pltpu.VMEM((2 * t // MXU_N, t, MXU_N), F32),
                        pltpu.VMEM((2 * t // MXU_N, 1, MXU_N), F32)],
        compiler_params=_cparams("parallel", "parallel", "parallel"),
        name="diff_attn_prompt",
    )(lam, qt, k, vt, g)


def _attn_sample_kernel(lam_ref, q_ref, kct_ref, vc_ref, kn_ref, vn_ref, g_ref, o_ref, *, past_len, n_heads,
                        out_scale):
    for h in range(n_heads):
        hs = slice(V_DIM * h, V_DIM * (h + 1))
        vc = vc_ref[pl.ds(h, past_len, stride=n_heads), :]
        o_ref[:, hs] = _attn_sample_head(lam_ref[0], q_ref[:, hs], kct_ref[hs, :], vc, kn_ref[:, hs], vn_ref[:, hs],
                                         g_ref[...], past_len, out_scale).astype(o_ref.dtype)


def _attn_sample_head(lam, q, kct, vc, kn, vn, g, past_len, out_scale):
    tq = q.shape[0]
    qs = _split_maps(q)
    s_c = jnp.dot(qs, kct.astype(BF16), preferred_element_type=F32)
    s_n = _qk(qs, kn)
    r = lax.broadcasted_iota(jnp.int32, s_n.shape, 0)
    c = lax.broadcasted_iota(jnp.int32, s_n.shape, 1)
    r = jnp.where(r >= tq, r - tq, r)
    s_n = jnp.where(((past_len + c) // CHUNK) <= ((past_len + r) // CHUNK), s_n, NEG_BIG)
    m = jnp.maximum(jnp.max(s_c, axis=-1, keepdims=True), jnp.max(s_n, axis=-1, keepdims=True))
    p_c = jnp.exp2(s_c - m)
    p_n = jnp.exp2(s_n - m)
    inv_l = 1.0 / (jnp.sum(p_c, axis=-1, keepdims=True) + jnp.sum(p_n, axis=-1, keepdims=True))
    p_c = p_c * inv_l
    p_n = p_n * inv_l
    w_c = p_c[0:tq] - lam * p_c[tq:2 * tq]
    w_n = p_n[0:tq] - lam * p_n[tq:2 * tq]
    o = (jnp.dot(w_c.astype(BF16), vc.astype(BF16), preferred_element_type=F32)
         + jnp.dot(w_n.astype(BF16), vn, preferred_element_type=F32))
    return _subln(o, g, out_scale)


def _attn_sample(lam, q, cache_kt, cache_v, k_new, v_new, g, *, out_scale):
    B, S, W = q.shape
    P = cache_kt.shape[1]
    H = W // V_DIM
    blk_new = pl.BlockSpec((None, S, W), lambda b: (b, 0, 0))
    return pl.pallas_call(
        functools.partial(_attn_sample_kernel, past_len=P, n_heads=H, out_scale=out_scale),
        grid=(B,),
        in_specs=[pl.BlockSpec(memory_space=pltpu.SMEM), blk_new,
                  pl.BlockSpec((W, P), lambda b: (b, 0)),
                  pl.BlockSpec((P * H, V_DIM), lambda b: (b, 0)),
                  blk_new, blk_new, pl.BlockSpec((1, V_DIM), lambda b: (0, 0))],
        out_specs=blk_new,
        out_shape=jax.ShapeDtypeStruct((B, S, W), BF16),
        compiler_params=_cparams("parallel"),
        name="diff_attn_sample",
    )(lam, q, cache_kt, cache_v, k_new, v_new, g)


def _cmul(ar, ai, br, bi):
    return ar * br - ai * bi, ar * bi + ai * br


def _s5_kernel(*refs, n_steps, chained, lane_chunk):
    if chained:
        (u_ref, perm_ref, permt_ref, wx_ref, cb_ref, d_ref, pre_ref, pim_ref, gre_ref, gim_ref,
         wglu_ref, ng_ref, y_ref, hre_ref, him_ref, xre, xim, cre, cim) = refs
    else:
        (u_ref, perm_ref, permt_ref, wx_ref, cb_ref, d_ref, pre_ref, pim_ref, gre_ref, gim_ref,
         wglu_ref, ng_ref, h0re_ref, h0im_ref, y_ref, hre_ref, him_ref, xre, xim) = refs
    R = n_steps
    n_state = xre.shape[1]
    n_blk = wx_ref.shape[0]
    blk_in = wx_ref.shape[1]
    blk_st = wx_ref.shape[2] // 2

    u = u_ref[...]
    u_hi = u.astype(BF16)
    u_lo = (u - u_hi.astype(F32)).astype(BF16)
    perm = perm_ref[...]
    up = (jnp.dot(perm, u_hi, preferred_element_type=F32) + jnp.dot(perm, u_lo, preferred_element_type=F32))
    upb = up.astype(BF16)
    for j in range(n_blk):
        xj = jnp.dot(upb[:, blk_in * j:blk_in * (j + 1)], wx_ref[j], preferred_element_type=F32)
        xre[:, blk_st * j:blk_st * (j + 1)] = xj[:, :blk_st]
        xim[:, blk_st * j:blk_st * (j + 1)] = xj[:, blk_st:]

    n_chunks = n_state // lane_chunk
    e_re, e_im = [], []
    for c in range(n_chunks):
        cs = slice(lane_chunk * c, lane_chunk * (c + 1))
        ar = jnp.broadcast_to(pre_ref[0:1, cs], (SUBLANES, lane_chunk))
        ai = jnp.broadcast_to(pim_ref[0:1, cs], (SUBLANES, lane_chunk))

        def scan_body(r, h, cs=cs, ar=ar, ai=ai):
            hr, hi = h
            rows = pl.ds(pl.multiple_of(r * SUBLANES, SUBLANES), SUBLANES)
            nr = ar * hr - ai * hi + xre[rows, cs]
            ni = ar * hi + ai * hr + xim[rows, cs]
            xre[rows, cs] = nr
            xim[rows, cs] = ni
            return nr, ni

        z = jnp.zeros((SUBLANES, lane_chunk), F32)
        er, ei = lax.fori_loop(0, R, scan_body, (z, z), unroll=4)
        e_re.append(er)
        e_im.append(ei)
    e_re = jnp.concatenate(e_re, axis=1)
    e_im = jnp.concatenate(e_im, axis=1)

    g1r = jnp.broadcast_to(gre_ref[0:1, :], e_re.shape)
    g1i = jnp.broadcast_to(gim_ref[0:1, :], e_re.shape)
    if chained:
        t = pl.program_id(1)

        @pl.when(t == 0)
        def _():
            cre[...] = jnp.zeros(cre.shape, F32)
            cim[...] = jnp.zeros(cim.shape, F32)

        sub = lax.broadcasted_iota(jnp.int32, e_re.shape, 0)
        c_r = cre[...]
        c_i = cim[...]
        inj_r, inj_i = _cmul(g1r, g1i, c_r, c_i)
        xr = e_re + jnp.where(sub == 0, inj_r, 0.0)
        xi = e_im + jnp.where(sub == 0, inj_i, 0.0)
        for lvl, k in enumerate((1, 2, 4)):
            gr = jnp.broadcast_to(gre_ref[lvl:lvl + 1, :], e_re.shape)
            gi = jnp.broadcast_to(gim_ref[lvl:lvl + 1, :], e_re.shape)
            sr = jnp.where(sub >= k, pltpu.roll(xr, k, 0), 0.0)
            si = jnp.where(sub >= k, pltpu.roll(xi, k, 0), 0.0)
            tr, ti = _cmul(gr, gi, sr, si)
            xr = xr + tr
            xi = xi + ti
        hend_r, hend_i = xr, xi
        hc_r = jnp.where(sub == 0, c_r, pltpu.roll(hend_r, 1, 0))
        hc_i = jnp.where(sub == 0, c_i, pltpu.roll(hend_i, 1, 0))
        last_r = jnp.broadcast_to(hend_r[SUBLANES - 1:SUBLANES, :], e_re.shape)
        last_i = jnp.broadcast_to(hend_i[SUBLANES - 1:SUBLANES, :], e_re.shape)
        cre[...] = last_r
        cim[...] = last_i
        hre_ref[...] = last_r[0:1, :]
        him_ref[...] = last_i[0:1, :]
    else:
        hc_r = h0re_ref[...]
        hc_i = h0im_ref[...]
        tr, ti = _cmul(g1r, g1i, hc_r, hc_i)
        hre_ref[...] = e_re + tr
        him_ref[...] = e_im + ti

    for c in range(n_chunks):
        cs = slice(lane_chunk * c, lane_chunk * (c + 1))
        hcr = hc_r[:, cs]
        hci = hc_i[:, cs]

        def fix_body(r, carry, cs=cs, hcr=hcr, hci=hci):
            rows = pl.ds(pl.multiple_of(r * SUBLANES, SUBLANES), SUBLANES)
            pr = jnp.broadcast_to(pre_ref[pl.ds(r, 1), cs], hcr.shape)
            pi = jnp.broadcast_to(pim_ref[pl.ds(r, 1), cs], hcr.shape)
            tr, ti = _cmul(pr, pi, hcr, hci)
            xre[rows, cs] = xre[rows, cs] + tr
            xim[rows, cs] = xim[rows, cs] + ti
            return carry

        lax.fori_loop(0, R, fix_body, 0, unroll=4)

    ys = []
    for j in range(n_blk):
        st = slice(blk_st * j, blk_st * (j + 1))
        ys.append(jnp.dot(xre[:, st].astype(BF16), cb_ref[j, 0:blk_st, :], preferred_element_type=F32)
                  + jnp.dot(xim[:, st].astype(BF16), cb_ref[j, blk_st:, :], preferred_element_type=F32))
    y = jnp.concatenate(ys, axis=1) + d_ref[...] * up
    y = 0.5 * y * (1.0 + jnp.tanh(math.sqrt(2.0 / math.pi) * (y + 0.044715 * (y * y * y))))
    gl = jnp.dot(y.astype(BF16), wglu_ref[...], preferred_element_type=F32)
    half = gl.shape[1] // 2
    z = gl[:, :half] * (1.0 / (1.0 + jnp.exp(-gl[:, half:])))
    z = z * lax.rsqrt(jnp.mean(z * z, axis=-1, keepdims=True) + RMS_EPS) * ng_ref[...]
    y_ref[...] = jnp.dot(permt_ref[...], z.astype(BF16), preferred_element_type=F32).astype(y_ref.dtype)


def _s5_tables(a_re, a_im, log_dt, b_re, b_im, c_re, c_im, n_steps):
    G, N = a_re.shape
    C = b_re.shape[2]
    dt = jnp.exp(log_dt.astype(F32))[:, None]
    a_re = a_re.astype(F32)
    a_im = a_im.astype(F32)

    def power(k):
        mag = jnp.exp(k * a_re * dt)
        return mag * jnp.cos(k * a_im * dt), mag * jnp.sin(k * a_im * dt)

    ab_re, ab_im = power(1.0)
    nr, ni = ab_re - 1.0, ab_im
    den = a_re * a_re + a_im * a_im
    cf_re = (nr * a_re + ni * a_im) / den
    cf_im = (ni * a_re - nr * a_im) / den
    ks = jnp.arange(1, n_steps + 1, dtype=F32)[:, None, None]
    p_re, p_im = power(ks)
    g_re, g_im = zip(*[power(float(m * n_steps)) for m in (1, 2, 4)])
    pad = jnp.zeros((SUBLANES - 3, G * N), F32)
    g_re = jnp.concatenate([jnp.stack(g_re).reshape(3, G * N), pad])
    g_im = jnp.concatenate([jnp.stack(g_im).reshape(3, G * N), pad])
    wx_re = cf_re[..., None] * b_re - cf_im[..., None] * b_im
    wx_im = cf_re[..., None] * b_im + cf_im[..., None] * b_re
    gpb = LANES // C
    nb = G // gpb
    eye = jnp.eye(gpb, dtype=F32)

    def in_block(w):
        w = w.reshape(nb, gpb, N, C)
        return jnp.einsum('jgnc,gh->jgchn', w, eye).reshape(nb, gpb * C, gpb * N)

    def out_block(w):
        w = w.reshape(nb, gpb, C, N)
        return jnp.einsum('jgcn,gh->jgnhc', w, eye).reshape(nb, gpb * N, gpb * C)

    wx = jnp.concatenate([in_block(wx_re), in_block(wx_im)], axis=2).astype(BF16)
    cb = jnp.concatenate([out_block(c_re.astype(F32)), out_block(-c_im.astype(F32))], axis=1).astype(BF16)
    return (p_re.reshape(n_steps, G * N), p_im.reshape(n_steps, G * N), g_re, g_im, wx, cb)


def _perm_matrices(n_steps):
    tt = SUBLANES * n_steps
    dst = jnp.arange(tt)
    src = (dst % SUBLANES) * n_steps + dst // SUBLANES
    perm = (src[:, None] == jnp.arange(tt)[None, :]).astype(BF16)
    return perm, perm.T


def _s5_mixer(u, tables, d, w_glu_bf, norm_g, *, n_steps, h0=None):
    B, L, W = u.shape
    p_re, p_im, g_re, g_im, wx, cb = tables
    n_state = p_re.shape[1]
    tt = SUBLANES * n_steps
    perm, permt = _perm_matrices(n_steps)
    chained = h0 is None
    const2 = lambda *_: (0, 0)
    const3 = lambda *_: (0, 0, 0)
    common_specs = [pl.BlockSpec(perm.shape, const2), pl.BlockSpec(perm.shape, const2),
                    pl.BlockSpec(wx.shape, const3), pl.BlockSpec(cb.shape, const3),
                    pl.BlockSpec((1, W), const2),
                    pl.BlockSpec(p_re.shape, const2), pl.BlockSpec(p_im.shape, const2),
                    pl.BlockSpec(g_re.shape, const2), pl.BlockSpec(g_im.shape, const2),
                    pl.BlockSpec(w_glu_bf.shape, const2), pl.BlockSpec((1, W), const2)]
    common_args = (perm, permt, wx, cb, d, p_re, p_im, g_re, g_im, w_glu_bf, norm_g)
    state_scratch = [pltpu.VMEM((tt, n_state), F32), pltpu.VMEM((tt, n_state), F32)]
    kern = functools.partial(_s5_kernel, n_steps=n_steps, chained=chained, lane_chunk=512)
    if chained:
        y, hre, him = pl.pallas_call(
            kern,
            grid=(B, L // tt),
            in_specs=[pl.BlockSpec((None, tt, W), lambda b, t: (b, t, 0))] + common_specs,
            out_specs=[pl.BlockSpec((None, tt, W), lambda b, t: (b, t, 0)),
                       pl.BlockSpec((None, 1, n_state), lambda b, t: (b, 0, 0)),
                       pl.BlockSpec((None, 1, n_state), lambda b, t: (b, 0, 0))],
            out_shape=[jax.ShapeDtypeStruct((B, L, W), BF16),
                       jax.ShapeDtypeStruct((B, 1, n_state), F32),
                       jax.ShapeDtypeStruct((B, 1, n_state), F32)],
            scratch_shapes=state_scratch + [pltpu.VMEM((SUBLANES, n_state), F32)] * 2,
            compiler_params=_cparams("parallel", "arbitrary"),
            name="s5_mixer_chained",
        )(u, *common_args)
        return y, hre.reshape(B, n_state), him.reshape(B, n_state)
    u2 = u.reshape(B * L, W)
    h0re, h0im = h0
    y, hre, him = pl.pallas_call(
        kern,
        grid=(B // SUBLANES,),
        in_specs=[pl.BlockSpec((tt, W), lambda i: (i, 0))] + common_specs
        + [pl.BlockSpec((SUBLANES, n_state), lambda i: (i, 0))] * 2,
        out_specs=[pl.BlockSpec((tt, W), lambda i: (i, 0)),
                   pl.BlockSpec((SUBLANES, n_state), lambda i: (i, 0)),
                   pl.BlockSpec((SUBLANES, n_state), lambda i: (i, 0))],
        out_shape=[jax.ShapeDtypeStruct((B * L, W), BF16),
                   jax.ShapeDtypeStruct((B, n_state), F32),
                   jax.ShapeDtypeStruct((B, n_state), F32)],
        scratch_shapes=state_scratch,
        compiler_params=_cparams("parallel"),
        name="s5_mixer_batched",
    )(u2, *common_args, h0re, h0im)
    return y.reshape(B, L, W), hre, him


def _layer_norm(z, g, b):
    mu = jnp.mean(z, axis=-1, keepdims=True)
    zc = z - mu
    var = jnp.mean(zc * zc, axis=-1, keepdims=True)
    return zc * lax.rsqrt(var + LN_EPS) * g + b


def _router_gates(x1, hi, rw_ref, rb_ref):
    lo = (x1 - hi.astype(F32)).astype(BF16)
    both = jnp.dot(hi, rw_ref[...], preferred_element_type=F32)
    lg = (both[:, :LANES] + both[:, LANES:]
          + jnp.dot(lo, rw_ref[:, :LANES], preferred_element_type=F32)) + rb_ref[...]
    lane = lax.broadcasted_iota(jnp.int32, lg.shape, 1)
    big = jnp.int32(LANES)
    is_grp = (lane >= N_EXPERTS) & (lane < N_EXPERTS + N_EXPERT_GROUPS)
    gl = jnp.where(is_grp, lg, NEG_BIG)
    gexp = jnp.where(is_grp, jnp.exp(gl - jnp.max(gl, axis=-1, keepdims=True)), 0.0)
    gprob = gexp / jnp.sum(gexp, axis=-1, keepdims=True)
    g_w = jnp.max(gprob, axis=-1, keepdims=True)
    g_idx = jnp.min(jnp.where(is_grp & (gprob == g_w), lane - N_EXPERTS, big), axis=-1, keepdims=True)
    valid = (lane < N_EXPERTS) & ((lane // EXPERTS_PER_GROUP) == g_idx)
    el = jnp.where(valid, lg, NEG_BIG)
    eexp = jnp.where(valid, jnp.exp(el - jnp.max(el, axis=-1, keepdims=True)), 0.0)
    eprob = eexp / jnp.sum(eexp, axis=-1, keepdims=True)
    w1 = jnp.max(jnp.where(valid, eprob, -1.0), axis=-1, keepdims=True)
    i1 = jnp.min(jnp.where(valid & (eprob == w1), lane, big), axis=-1, keepdims=True)
    rest = valid & (lane != i1)
    w2 = jnp.max(jnp.where(rest, eprob, -1.0), axis=-1, keepdims=True)
    i2 = jnp.min(jnp.where(rest & (eprob == w2), lane, big), axis=-1, keepdims=True)
    den = w1 + w2
    return (jnp.where(lane == i1, g_w * (w1 / den), 0.0)
            + jnp.where(lane == i2, g_w * (w2 / den), 0.0))


def _tail_kernel(att_ref, ssm_ref, x_ref, wa_ref, wb_ref, g1_ref, b1_ref, rw_ref, rb_ref,
                 wg_ref, wu_ref, wd_ref, g2_ref, b2_ref, o_ref, acc_ref, *, alpha):
    mix = (jnp.dot(att_ref[...], wa_ref[...], preferred_element_type=F32)
           + jnp.dot(ssm_ref[...], wb_ref[...], preferred_element_type=F32))
    x1 = _layer_norm(alpha * x_ref[...] + mix, g1_ref[...], b1_ref[...])
    xb = x1.astype(BF16)
    gates = _router_gates(x1, xb, rw_ref, rb_ref)
    for e in range(wg_ref.shape[0]):
        hg = jnp.dot(xb, wg_ref[e], preferred_element_type=F32)
        hu = jnp.dot(xb, wu_ref[e], preferred_element_type=F32)
        h = hg * (1.0 / (1.0 + jnp.exp(-hg))) * hu * gates[:, e:e + 1]
        contrib = jnp.dot(h.astype(BF16), wd_ref[e], preferred_element_type=F32)
        if e == 0:
            acc_ref[...] = contrib
        else:
            acc_ref[...] += contrib
    o_ref[...] = _layer_norm(alpha * x1 + acc_ref[...], g2_ref[...], b2_ref[...])


def _tail(att, ssm, x2d, wa, wb, g1, b1, rw, rb, wg, wu, wd, g2, b2, *, tm, alpha):
    T, D = x2d.shape
    Wh = att.shape[1]
    row = lambda i: (i, 0)
    const2 = lambda i: (0, 0)
    const3 = lambda i: (0, 0, 0)
    once = pl.Buffered(1)
    vec = pl.BlockSpec((1, D), const2)
    return pl.pallas_call(
        functools.partial(_tail_kernel, alpha=alpha),
        grid=(T // tm,),
        in_specs=[pl.BlockSpec((tm, Wh), row), pl.BlockSpec((tm, Wh), row), pl.BlockSpec((tm, D), row),
                  pl.BlockSpec(wa.shape, const2, pipeline_mode=once),
                  pl.BlockSpec(wb.shape, const2, pipeline_mode=once), vec, vec,
                  pl.BlockSpec(rw.shape, const2, pipeline_mode=once), pl.BlockSpec((1, LANES), const2),
                  pl.BlockSpec(wg.shape, const3, pipeline_mode=once),
                  pl.BlockSpec(wu.shape, const3, pipeline_mode=once),
                  pl.BlockSpec(wd.shape, const3, pipeline_mode=once), vec, vec],
        out_specs=pl.BlockSpec((tm, D), row),
        out_shape=jax.ShapeDtypeStruct((T, D), F32),
        scratch_shapes=[pltpu.VMEM((tm, D), F32)],
        compiler_params=_cparams("parallel"),
        name="out_proj_router_moe",
    )(att, ssm, x2d, wa, wb, g1, b1, rw, rb, wg, wu, wd, g2, b2)


def _rope_tables(pos):
    half = HEAD_DIM // 2
    inv = 1.0 / (ROPE_THETA ** (jnp.arange(half, dtype=F32) / half))
    ang = pos.astype(F32)[:, None] * inv[None, :]
    cos = jnp.cos(ang)
    sin = jnp.sin(ang)
    cos_t = jnp.tile(cos, (1, LANES // half))
    sin_t = jnp.tile(jnp.concatenate([-sin, sin], axis=1), (1, LANES // HEAD_DIM))
    return cos_t, sin_t


def _row_tile(n, pref):
    t = min(n, pref)
    while n % t:
        t //= 2
    return t


def kernel(x_prompt, x_sample, cache_k, cache_v, state_ssm_re, state_ssm_im, w_in, lam_q1, lam_k1, lam_q2, lam_k2, subln_g, ssm_a_re, ssm_a_im, ssm_log_dt, ssm_b_re, ssm_b_im, ssm_c_re, ssm_c_im, ssm_d, w_glu, ssm_norm_g, w_out, ln1_g, ln1_b, w_grp, b_grp, w_rt, b_rt, w_gate, w_up, w_down, ln2_g, ln2_b):
    depth = w_in.shape[0]
    assert depth == 1, "single-layer step"
    B, L, D = x_prompt.shape
    Bs, S, _ = x_sample.shape
    P = cache_k.shape[2]
    H = cache_k.shape[3]
    n_qk = H * 2 * HEAD_DIM
    G, N = ssm_a_re.shape[1], ssm_a_re.shape[2]
    alpha = (2.0 * depth) ** 0.25
    lam_init = 0.8 - 0.6 * math.exp(-0.3 * 0)
    out_scale = 1.0 - lam_init
    l = 0

    w_in_bf = w_in[l].astype(BF16)
    lam = (jnp.exp(jnp.sum(lam_q1[l].astype(F32) * lam_k1[l].astype(F32)))
           - jnp.exp(jnp.sum(lam_q2[l].astype(F32) * lam_k2[l].astype(F32))) + lam_init).reshape(1).astype(F32)
    g_sub = subln_g[l].astype(F32).reshape(1, V_DIM)
    w_glu_bf = w_glu[l].astype(BF16)
    ssm_w = D - H * V_DIM
    d_row = ssm_d[l].astype(F32).reshape(1, ssm_w)
    ng_row = ssm_norm_g[l].astype(F32).reshape(1, ssm_w)
    wa = w_out[l, :H * V_DIM].astype(BF16)
    wb = w_out[l, H * V_DIM:].astype(BF16)
    ln1g, ln1b = ln1_g[l].astype(F32).reshape(1, D), ln1_b[l].astype(F32).reshape(1, D)
    ln2g, ln2b = ln2_g[l].astype(F32).reshape(1, D), ln2_b[l].astype(F32).reshape(1, D)
    w_r = jnp.concatenate([w_rt[l].astype(F32), w_grp[l].astype(F32),
                           jnp.zeros((D, LANES - N_EXPERTS - N_EXPERT_GROUPS), F32)], axis=1)
    rhi = w_r.astype(BF16)
    rw = jnp.concatenate([rhi, (w_r - rhi.astype(F32)).astype(BF16)], axis=1)
    rb =jnp.concatenate([b_rt[l].astype(F32), b_grp[l].astype(F32),
                          jnp.zeros((LANES - N_EXPERTS - N_EXPERT_GROUPS,), F32)]).reshape(1, LANES)
    wg, wu, wd = w_gate[l].astype(BF16), w_up[l].astype(BF16), w_down[l].astype(BF16)

    def tail(att, ssm, x2d):
        return _tail(att, ssm, x2d, wa, wb, ln1g, ln1b, rw, rb, wg, wu, wd, ln2g, ln2b,
                     tm=_row_tile(x2d.shape[0], 512), alpha=alpha)

    cos_p, sin_p = _rope_tables(jnp.arange(L))
    xp2 = x_prompt.reshape(B * L, D)
    qp, kpf, kpb, vpf, vpb, up = _in_proj(xp2, w_in_bf, cos_p, sin_p, tm=_row_tile(L, 512),
                                          t_attn=_row_tile(L, 512))
    r3 = lambda a: a.reshape(B, L, -1)
    att_p = _attn_prompt(lam, qp, r3(kpb), vpb, g_sub, out_scale=out_scale)
    n_steps_p = _row_tile(L, 256) // SUBLANES
    tabs_p = _s5_tables(ssm_a_re[l], ssm_a_im[l], ssm_log_dt[l], ssm_b_re[l].astype(F32), ssm_b_im[l].astype(F32),
                        ssm_c_re[l], ssm_c_im[l], n_steps_p)
    ssm_p, hre_p, him_p = _s5_mixer(r3(up), tabs_p, d_row, w_glu_bf, ng_row, n_steps=n_steps_p)
    y_prompt = tail(att_p.reshape(B * L, -1), ssm_p.reshape(B * L, -1), xp2).reshape(B, L, D)

    cos_s, sin_s = _rope_tables(P + jnp.arange(S))
    cos_s, sin_s = jnp.tile(cos_s, (Bs, 1)), jnp.tile(sin_s, (Bs, 1))
    xs2 = x_sample.reshape(Bs * S, D)
    qs, ksf, ksb, vsf, vsb, us = _in_proj(xs2, w_in_bf, cos_s, sin_s, tm=_row_tile(Bs * S, 512))
    s3 = lambda a: a.reshape(Bs, S, -1)
    cache_kt = jnp.transpose(cache_k[l], (0, 2, 3, 4, 1)).reshape(Bs * n_qk, P)
    att_s = _attn_sample(lam, s3(qs), cache_kt, cache_v[l].reshape(Bs * P * H, V_DIM),
                         s3(ksb), s3(vsb), g_sub, out_scale=out_scale)
    tabs_s = _s5_tables(ssm_a_re[l], ssm_a_im[l], ssm_log_dt[l], ssm_b_re[l].astype(F32), ssm_b_im[l].astype(F32),
                        ssm_c_re[l], ssm_c_im[l], S)
    h0 = (state_ssm_re[l].astype(F32).reshape(Bs, G * N), state_ssm_im[l].astype(F32).reshape(Bs, G * N))
    ssm_s, hre_s, him_s = _s5_mixer(s3(us), tabs_s, d_row, w_glu_bf, ng_row, n_steps=S, h0=h0)
    y_sample = tail(att_s.reshape(Bs * S, -1), ssm_s.reshape(Bs * S, -1), xs2).reshape(Bs, S, D)

    return (y_prompt, y_sample,
            jnp.transpose(kpf.reshape(1, B, H, 2, HEAD_DIM, L), (0, 1, 5, 2, 3, 4)), vpf.reshape(1, B, L, H, V_DIM),
            hre_p.reshape(1, B, G, N), him_p.reshape(1, B, G, N),
            ksf.reshape(1, Bs, S, H, 2, HEAD_DIM), vsf.reshape(1, Bs, S, H, V_DIM),
            hre_s.reshape(1, Bs, G, N), him_s.reshape(1, Bs, G, N))
```

```python
import functools
import math

import jax
import jax.numpy as jnp
from jax import lax
from jax.experimental import pallas as pl
from jax.experimental.pallas import tpu as pltpu

F32 = jnp.float32
BF16 = jnp.bfloat16

HEAD_DIM = 64
V_DIM = 2 * HEAD_DIM
CHUNK = 64
SSM_GROUP = 16
SSM_STATE = 64
N_EXPERT_GROUPS = 4
EXPERTS_PER_GROUP = 4
N_EXPERTS = N_EXPERT_GROUPS * EXPERTS_PER_GROUP
ROPE_THETA = 10000.0
LN_EPS = 1e-5
RMS_EPS = 1e-6
LANES = 128
SUBLANES = 8
MXU_N = 256
NEG_BIG = -1e30
SUM_ROWS = 16
LOG2E = 1.4426950408889634
VMEM_LIMIT = 52 * 1024 * 1024


def _cparams(*sem):
    return pltpu.CompilerParams(dimension_semantics=sem, vmem_limit_bytes=VMEM_LIMIT)


def _store_transposed_tiles(ref, a):
    n_t, n_h, _, t = ref.shape
    at = a.T
    for n in range(n_t):
        for h in range(n_h):
            ref[n, h] = at[V_DIM * h:V_DIM * (h + 1), t * n:t * (n + 1)].astype(ref.dtype)


def _inproj_kernel(x_ref, w_ref, cos_ref, sin_ref, q_ref, kf_ref, kb_ref, vf_ref, vb_ref, u_ref,
                   *, q_scale, n_qk, transposed):
    xb = x_ref[...].astype(BF16)
    cos = cos_ref[...]
    sin = sin_ref[...]
    lane = lax.broadcasted_iota(jnp.int32, cos.shape, 1)
    first = (lane & (HEAD_DIM - 1)) < HEAD_DIM // 2

    def rope(t):
        rot = jnp.where(first, pltpu.roll(t, LANES - HEAD_DIM // 2, 1), pltpu.roll(t, HEAD_DIM // 2, 1))
        return t * cos + rot * sin

    pq = jnp.dot(xb, w_ref[:, 0:n_qk], preferred_element_type=F32)
    qr = jnp.concatenate([rope(pq[:, LANES * j:LANES * (j + 1)]) for j in range(n_qk // LANES)], axis=1) * q_scale
    if transposed:
        _store_transposed_tiles(q_ref, qr)
    else:
        q_ref[...] = qr.astype(BF16)
    pk = jnp.dot(xb, w_ref[:, n_qk:2 * n_qk], preferred_element_type=F32)
    kr = jnp.concatenate([rope(pk[:, LANES * j:LANES * (j + 1)]) for j in range(n_qk // LANES)], axis=1)
    kb_ref[...] = kr.astype(BF16)
    n_v = u_ref.shape[1]
    pv = jnp.dot(xb, w_ref[:, 2 * n_qk:2 * n_qk + n_v], preferred_element_type=F32)
    if transposed:
        kf_ref[...] = kr.T
        n_h = n_v // V_DIM
        for h in range(n_h):
            vf_ref[pl.ds(h, pv.shape[0], stride=n_h), :] = pv[:, V_DIM * h:V_DIM * (h + 1)]
        _store_transposed_tiles(vb_ref, pv)
    else:
        kf_ref[...] = kr
        vf_ref[...] = pv
        vb_ref[...] = pv.astype(BF16)
    u_ref[...] = jnp.dot(xb, w_ref[:, 2 * n_qk + n_v:], preferred_element_type=F32)


def _in_proj(x2d, w_bf, cos_t, sin_t, *, tm, t_attn=None):
    T, D = x2d.shape
    n_cols = w_bf.shape[1]
    n_qk = n_cols // 4
    n_tab = cos_t.shape[0] // tm
    row = lambda i: (i, 0)
    tab = lambda i: (i % n_tab, 0)
    out_sds = lambda dt: jax.ShapeDtypeStruct((T, n_qk), dt)
    spec = pl.BlockSpec((tm, n_qk), row)
    qv_spec, qv_sds = spec, out_sds(BF16)
    kf_spec, kf_sds, vf_spec, vf_sds = spec, out_sds(F32), spec, out_sds(F32)
    if t_attn is not None:
        H = n_qk // V_DIM
        qv_spec = pl.BlockSpec((tm // t_attn, H, V_DIM, t_attn), lambda i: (i, 0, 0, 0))
        qv_sds = jax.ShapeDtypeStruct((T // t_attn, H, V_DIM, t_attn), BF16)
        kf_spec = pl.BlockSpec((None, n_qk, tm), lambda i: (i // n_tab, 0, i % n_tab))
        kf_sds = jax.ShapeDtypeStruct((T // cos_t.shape[0], n_qk, cos_t.shape[0]), F32)
        vf_spec = pl.BlockSpec((tm * H, V_DIM), row)
        vf_sds = jax.ShapeDtypeStruct((T * H, V_DIM), F32)
    return pl.pallas_call(
        functools.partial(_inproj_kernel, q_scale=HEAD_DIM ** -0.5 * LOG2E, n_qk=n_qk,
                          transposed=t_attn is not None),
        grid=(T // tm,),
        in_specs=[pl.BlockSpec((tm, D), row),
                  pl.BlockSpec((D, n_cols), lambda i: (0, 0)),
                  pl.BlockSpec((tm, LANES), tab),
                  pl.BlockSpec((tm, LANES), tab)],
        out_specs=[qv_spec, kf_spec, spec, vf_spec, qv_spec, spec],
        out_shape=[qv_sds, kf_sds, out_sds(BF16), vf_sds, qv_sds, out_sds(F32)],
        compiler_params=_cparams("parallel"),
        name="in_proj_rope",
    )(x2d, w_bf, cos_t, sin_t)


def _split_maps(q):
    lane = lax.broadcasted_iota(jnp.int32, q.shape, 1)
    zero = jnp.zeros_like(q)
    return jnp.concatenate([jnp.where(lane < HEAD_DIM, q, zero), jnp.where(lane >= HEAD_DIM, q, zero)], axis=0)


def _qk(qs, kb):
    return lax.dot_general(qs, kb, (((1,), (1,)), ((), ())), preferred_element_type=F32)


def _subln(o, g, out_scale):
    ms = jnp.mean(o * o, axis=-1, keepdims=True)
    return o * lax.rsqrt(ms + RMS_EPS) * g * out_scale


def _attn_prompt_kernel(lam_ref, qt_ref, k_ref, vt_ref, g_ref, o_ref, qs_ref, acc_ref, s0_ref, s1_ref, m_ref,
                        *, tq, out_scale):
    i = pl.program_id(2)
    s_buf = (s0_ref, s1_ref)
    n_cb = 2 * tq // MXU_N
    qt = qt_ref[...]
    row = lax.broadcasted_iota(jnp.int32, qt.shape, 0)
    zero = jnp.zeros_like(qt)
    q_maps = (jnp.where(row < HEAD_DIM, qt, zero), jnp.where(row >= HEAD_DIM, qt, zero))
    for cb in range(n_cb):
        j = cb % (n_cb // 2)
        qs_ref[cb] = q_maps[cb // (n_cb // 2)][:, MXU_N * j:MXU_N * (j + 1)]
    acc_ref[...] = jnp.zeros(acc_ref.shape, F32)
    m_ref[...] = jnp.full(m_ref.shape, NEG_BIG, F32)
    ones_rows = jnp.ones((SUM_ROWS, tq), BF16)

    def scores(kt, slot):
        kb = k_ref[pl.ds(pl.multiple_of(kt * tq, tq), tq), :]
        for cb in range(n_cb):
            s_buf[slot][cb] = jnp.dot(kb, qs_ref[cb], preferred_element_type=F32)

    def softmax_pv(kt, slot, masked):
        lhs = jnp.concatenate([vt_ref[kt], ones_rows], axis=0)
        for cb in range(n_cb):
            st = s_buf[slot][cb]
            if masked:
                c = lax.broadcasted_iota(jnp.int32, st.shape, 0)
                r = (lax.broadcasted_iota(jnp.int32, st.shape, 1) + MXU_N * cb) % tq
                st = jnp.where((c // CHUNK) <= (r // CHUNK), st, NEG_BIG)
            m_prev = m_ref[cb]
            m_new = jnp.maximum(m_prev, jnp.max(st, axis=0, keepdims=True))
            alpha = jnp.exp2(m_prev - m_new)
            pt = jnp.exp2((st - m_new).astype(BF16))
            m_ref[cb] = m_new
            acc_ref[cb] = alpha * acc_ref[cb] + jnp.dot(lhs, pt, preferred_element_type=F32)

    def step(t, par):
        scores(t, par)
        softmax_pv(t - 1, 1 - par, False)

    scores(0, 0)

    def pair(j, carry):
        step(2 * j + 1, 1)
        step(2 * j + 2, 0)
        return carry

    lax.fori_loop(0, i // 2, pair, 0)

    @pl.when(i % 2 == 0)
    def _():
        softmax_pv(i, 0, True)

    @pl.when(i % 2 == 1)
    def _():
        step(i, 1)
        softmax_pv(i, 1, True)

    def normalized(cb):
        return acc_ref[cb, 0:V_DIM, :] * (1.0 / acc_ref[cb, V_DIM:V_DIM + 1, :])

    half = n_cb // 2
    ot = jnp.concatenate([normalized(j) - lam_ref[0] * normalized(j + half) for j in range(half)], axis=1)
    o_ref[...] = _subln(ot.T, g_ref[...], out_scale).astype(o_ref.dtype)


def _attn_prompt(lam, qt, k, vt, g, *, out_scale):
    B, L, W = k.shape
    n_t, H, _, t = qt.shape
    nq = L // t
    return pl.pallas_call(
        functools.partial(_attn_prompt_kernel, tq=t, out_scale=out_scale),
        grid=(B, H, nq),
        in_specs=[pl.BlockSpec(memory_space=pltpu.SMEM),
                  pl.BlockSpec((None, None, V_DIM, t), lambda b, h, i: (b * nq + i, h, 0, 0)),
                  pl.BlockSpec((None, L, V_DIM), lambda b, h, i: (b, 0, h)),
                  pl.BlockSpec((nq, None, V_DIM, t), lambda b, h, i: (b, h, 0, 0)),
                  pl.BlockSpec((1, V_DIM), lambda b, h, i: (0, 0))],
        out_specs=pl.BlockSpec((None, t, V_DIM), lambda b, h, i: (b, i, h)),
        out_shape=jax.ShapeDtypeStruct((B, L, W), BF16),
        scratch_shapes=[pltpu.VMEM((2 * t // MXU_N, V_DIM, MXU_N), BF16),
                        pltpu.VMEM((2 * t // MXU_N, V_DIM + SUM_ROWS, MXU_N), F32),
                        pltpu.VMEM((2 * t // MXU_N, t, MXU_N), F32), pltpu.VMEM((2 * t // MXU_N, t, MXU_N), F32),
                        pltpu.VMEM((2 * t // MXU_N, 1, MXU_N), F32)],
        compiler_params=_cparams("parallel", "parallel", "parallel"),
        name="diff_attn_prompt",
    )(lam, qt, k, vt, g)


def _attn_sample_kernel(lam_ref, q_ref, kct_ref, vc_ref, kn_ref, vn_ref, g_ref, o_ref, *, past_len, n_heads,
                        out_scale):
    for h in range(n_heads):
        hs = slice(V_DIM * h, V_DIM * (h + 1))
        vc = vc_ref[pl.ds(h, past_len, stride=n_heads), :]
        o_ref[:, hs] = _attn_sample_head(lam_ref[0], q_ref[:, hs], kct_ref[hs, :], vc, kn_ref[:, hs], vn_ref[:, hs],
                                         g_ref[...], past_len, out_scale).astype(o_ref.dtype)


def _attn_sample_head(lam, q, kct, vc, kn, vn, g, past_len, out_scale):
    tq = q.shape[0]
    qs = _split_maps(q)
    s_c = jnp.dot(qs, kct.astype(BF16), preferred_element_type=F32)
    s_n = _qk(qs, kn)
    r = lax.broadcasted_iota(jnp.int32, s_n.shape, 0)
    c = lax.broadcasted_iota(jnp.int32, s_n.shape, 1)
    r = jnp.where(r >= tq, r - tq, r)
    s_n = jnp.where(((past_len + c) // CHUNK) <= ((past_len + r) // CHUNK), s_n, NEG_BIG)
    m = jnp.maximum(jnp.max(s_c, axis=-1, keepdims=True), jnp.max(s_n, axis=-1, keepdims=True))
    p_c = jnp.exp2(s_c - m)
    p_n = jnp.exp2(s_n - m)
    inv_l = 1.0 / (jnp.sum(p_c, axis=-1, keepdims=True) + jnp.sum(p_n, axis=-1, keepdims=True))
    p_c = p_c * inv_l
    p_n = p_n * inv_l
    w_c = p_c[0:tq] - lam * p_c[tq:2 * tq]
    w_n = p_n[0:tq] - lam * p_n[tq:2 * tq]
    o = (jnp.dot(w_c.astype(BF16), vc.astype(BF16), preferred_element_type=F32)
         + jnp.dot(w_n.astype(BF16), vn, preferred_element_type=F32))
    return _subln(o, g, out_scale)


def _attn_sample(lam, q, cache_kt, cache_v, k_new, v_new, g, *, out_scale):
    B, S, W = q.shape
    P = cache_kt.shape[1]
    H = W // V_DIM
    blk_new = pl.BlockSpec((None, S, W), lambda b: (b, 0, 0))
    return pl.pallas_call(
        functools.partial(_attn_sample_kernel, past_len=P, n_heads=H, out_scale=out_scale),
        grid=(B,),
        in_specs=[pl.BlockSpec(memory_space=pltpu.SMEM), blk_new,
                  pl.BlockSpec((W, P), lambda b: (b, 0)),
                  pl.BlockSpec((P * H, V_DIM), lambda b: (b, 0)),
                  blk_new, blk_new, pl.BlockSpec((1, V_DIM), lambda b: (0, 0))],
        out_specs=blk_new,
        out_shape=jax.ShapeDtypeStruct((B, S, W), BF16),
        compiler_params=_cparams("parallel"),
        name="diff_attn_sample",
    )(lam, q, cache_kt, cache_v, k_new, v_new, g)


def _cmul(ar, ai, br, bi):
    return ar * br - ai * bi, ar * bi + ai * br


def _s5_kernel(*refs, n_steps, chained, lane_chunk):
    if chained:
        (u_ref, perm_ref, permt_ref, wx_ref, cb_ref, d_ref, pre_ref, pim_ref, gre_ref, gim_ref,
         wglu_ref, ng_ref, y_ref, hre_ref, him_ref, xre, xim, cre, cim) = refs
    else:
        (u_ref, perm_ref, permt_ref, wx_ref, cb_ref, d_ref, pre_ref, pim_ref, gre_ref, gim_ref,
         wglu_ref, ng_ref, h0re_ref, h0im_ref, y_ref, hre_ref, him_ref, xre, xim) = refs
    R = n_steps
    n_state = xre.shape[1]
    n_blk = wx_ref.shape[0]
    blk_in = wx_ref.shape[1]
    blk_st = wx_ref.shape[2] // 2

    u = u_ref[...]
    u_hi = u.astype(BF16)
    u_lo = (u - u_hi.astype(F32)).astype(BF16)
    perm = perm_ref[...]
    up = (jnp.dot(perm, u_hi, preferred_element_type=F32) + jnp.dot(perm, u_lo, preferred_element_type=F32))
    upb = up.astype(BF16)
    for j in range(n_blk):
        xj = jnp.dot(upb[:, blk_in * j:blk_in * (j + 1)], wx_ref[j], preferred_element_type=F32)
        xre[:, blk_st * j:blk_st * (j + 1)] = xj[:, :blk_st]
        xim[:, blk_st * j:blk_st * (j + 1)] = xj[:, blk_st:]

    n_chunks = n_state // lane_chunk
    e_re, e_im = [], []
    for c in range(n_chunks):
        cs = slice(lane_chunk * c, lane_chunk * (c + 1))
        ar = jnp.broadcast_to(pre_ref[0:1, cs], (SUBLANES, lane_chunk))
        ai = jnp.broadcast_to(pim_ref[0:1, cs], (SUBLANES, lane_chunk))

        def scan_body(r, h, cs=cs, ar=ar, ai=ai):
            hr, hi = h
            rows = pl.ds(pl.multiple_of(r * SUBLANES, SUBLANES), SUBLANES)
            nr = ar * hr - ai * hi + xre[rows, cs]
            ni = ar * hi + ai * hr + xim[rows, cs]
            xre[rows, cs] = nr
            xim[rows, cs] = ni
            return nr, ni

        z = jnp.zeros((SUBLANES, lane_chunk), F32)
        er, ei = lax.fori_loop(0, R, scan_body, (z, z), unroll=4)
        e_re.append(er)
        e_im.append(ei)
    e_re = jnp.concatenate(e_re, axis=1)
    e_im = jnp.concatenate(e_im, axis=1)

    g1r = jnp.broadcast_to(gre_ref[0:1, :], e_re.shape)
    g1i = jnp.broadcast_to(gim_ref[0:1, :], e_re.shape)
    if chained:
        t = pl.program_id(1)

        @pl.when(t == 0)
        def _():
            cre[...] = jnp.zeros(cre.shape, F32)
            cim[...] = jnp.zeros(cim.shape, F32)

        sub = lax.broadcasted_iota(jnp.int32, e_re.shape, 0)
        c_r = cre[...]
        c_i = cim[...]
        inj_r, inj_i = _cmul(g1r, g1i, c_r, c_i)
        xr = e_re + jnp.where(sub == 0, inj_r, 0.0)
        xi = e_im + jnp.where(sub == 0, inj_i, 0.0)
        for lvl, k in enumerate((1, 2, 4)):
            gr = jnp.broadcast_to(gre_ref[lvl:lvl + 1, :], e_re.shape)
            gi = jnp.broadcast_to(gim_ref[lvl:lvl + 1, :], e_re.shape)
            sr = jnp.where(sub >= k, pltpu.roll(xr, k, 0), 0.0)
            si = jnp.where(sub >= k, pltpu.roll(xi, k, 0), 0.0)
            tr, ti = _cmul(gr, gi, sr, si)
            xr = xr + tr
            xi = xi + ti
        hend_r, hend_i = xr, xi
        hc_r = jnp.where(sub == 0, c_r, pltpu.roll(hend_r, 1, 0))
        hc_i = jnp.where(sub == 0, c_i, pltpu.roll(hend_i, 1, 0))
        last_r = jnp.broadcast_to(hend_r[SUBLANES - 1:SUBLANES, :], e_re.shape)
        last_i = jnp.broadcast_to(hend_i[SUBLANES - 1:SUBLANES, :], e_re.shape)
        cre[...] = last_r
        cim[...] = last_i
        hre_ref[...] = last_r[0:1, :]
        him_ref[...] = last_i[0:1, :]
    else:
        hc_r = h0re_ref[...]
        hc_i = h0im_ref[...]
        tr, ti = _cmul(g1r, g1i, hc_r, hc_i)
        hre_ref[...] = e_re + tr
        him_ref[...] = e_im + ti

    for c in range(n_chunks):
        cs = slice(lane_chunk * c, lane_chunk * (c + 1))
        hcr = hc_r[:, cs]
        hci = hc_i[:, cs]

        def fix_body(r, carry, cs=cs, hcr=hcr, hci=hci):
            rows = pl.ds(pl.multiple_of(r * SUBLANES, SUBLANES), SUBLANES)
            pr = jnp.broadcast_to(pre_ref[pl.ds(r, 1), cs], hcr.shape)
            pi = jnp.broadcast_to(pim_ref[pl.ds(r, 1), cs], hcr.shape)
            tr, ti = _cmul(pr, pi, hcr, hci)
            xre[rows, cs] = xre[rows, cs] + tr
            xim[rows, cs] = xim[rows, cs] + ti
            return carry

        lax.fori_loop(0, R, fix_body, 0, unroll=4)

    ys = []
    for j in range(n_blk):
        st = slice(blk_st * j, blk_st * (j + 1))
        ys.append(jnp.dot(xre[:, st].astype(BF16), cb_ref[j, 0:blk_st, :], preferred_element_type=F32)
                  + jnp.dot(xim[:, st].astype(BF16), cb_ref[j, blk_st:, :], preferred_element_type=F32))
    y = jnp.concatenate(ys, axis=1) + d_ref[...] * up
    y = 0.5 * y * (1.0 + jnp.tanh(math.sqrt(2.0 / math.pi) * (y + 0.044715 * (y * y * y))))
    gl = jnp.dot(y.astype(BF16), wglu_ref[...], preferred_element_type=F32)
    half = gl.shape[1] // 2
    z = gl[:, :half] * (1.0 / (1.0 + jnp.exp(-gl[:, half:])))
    z = z * lax.rsqrt(jnp.mean(z * z, axis=-1, keepdims=True) + RMS_EPS) * ng_ref[...]
    y_ref[...] = jnp.dot(permt_ref[...], z.astype(BF16), preferred_element_type=F32).astype(y_ref.dtype)


def _s5_tables(a_re, a_im, log_dt, b_re, b_im, c_re, c_im, n_steps):
    G, N = a_re.shape
    C = b_re.shape[2]
    dt = jnp.exp(log_dt.astype(F32))[:, None]
    a_re = a_re.astype(F32)
    a_im = a_im.astype(F32)

    def power(k):
        mag = jnp.exp(k * a_re * dt)
        return mag * jnp.cos(k * a_im * dt), mag * jnp.sin(k * a_im * dt)

    ab_re, ab_im = power(1.0)
    nr, ni = ab_re - 1.0, ab_im
    den = a_re * a_re + a_im * a_im
    cf_re = (nr * a_re + ni * a_im) / den
    cf_im = (ni * a_re - nr * a_im) / den
    ks = jnp.arange(1, n_steps + 1, dtype=F32)[:, None, None]
    p_re, p_im = power(ks)
    g_re, g_im = zip(*[power(float(m * n_steps)) for m in (1, 2, 4)])
    pad = jnp.zeros((SUBLANES - 3, G * N), F32)
    g_re = jnp.concatenate([jnp.stack(g_re).reshape(3, G * N), pad])
    g_im = jnp.concatenate([jnp.stack(g_im).reshape(3, G * N), pad])
    wx_re = cf_re[..., None] * b_re - cf_im[..., None] * b_im
    wx_im = cf_re[..., None] * b_im + cf_im[..., None] * b_re
    gpb = LANES // C
    nb = G // gpb
    eye = jnp.eye(gpb, dtype=F32)

    def in_block(w):
        w = w.reshape(nb, gpb, N, C)
        return jnp.einsum('jgnc,gh->jgchn', w, eye).reshape(nb, gpb * C, gpb * N)

    def out_block(w):
        w = w.reshape(nb, gpb, C, N)
        return jnp.einsum('jgcn,gh->jgnhc', w, eye).reshape(nb, gpb * N, gpb * C)

    wx = jnp.concatenate([in_block(wx_re), in_block(wx_im)], axis=2).astype(BF16)
    cb = jnp.concatenate([out_block(c_re.astype(F32)), out_block(-c_im.astype(F32))], axis=1).astype(BF16)
    return (p_re.reshape(n_steps, G * N), p_im.reshape(n_steps, G * N), g_re, g_im, wx, cb)


def _perm_matrices(n_steps):
    tt = SUBLANES * n_steps
    dst = jnp.arange(tt)
    src = (dst % SUBLANES) * n_steps + dst // SUBLANES
    perm = (src[:, None] == jnp.arange(tt)[None, :]).astype(BF16)
    return perm, perm.T


def _s5_mixer(u, tables, d, w_glu_bf, norm_g, *, n_steps, h0=None):
    B, L, W = u.shape
    p_re, p_im, g_re, g_im, wx, cb = tables
    n_state = p_re.shape[1]
    tt = SUBLANES * n_steps
    perm, permt = _perm_matrices(n_steps)
    chained = h0 is None
    const2 = lambda *_: (0, 0)
    const3 = lambda *_: (0, 0, 0)
    common_specs = [pl.BlockSpec(perm.shape, const2), pl.BlockSpec(perm.shape, const2),
                    pl.BlockSpec(wx.shape, const3), pl.BlockSpec(cb.shape, const3),
                    pl.BlockSpec((1, W), const2),
                    pl.BlockSpec(p_re.shape, const2), pl.BlockSpec(p_im.shape, const2),
                    pl.BlockSpec(g_re.shape, const2), pl.BlockSpec(g_im.shape, const2),
                    pl.BlockSpec(w_glu_bf.shape, const2), pl.BlockSpec((1, W), const2)]
    common_args = (perm, permt, wx, cb, d, p_re, p_im, g_re, g_im, w_glu_bf, norm_g)
    state_scratch = [pltpu.VMEM((tt, n_state), F32), pltpu.VMEM((tt, n_state), F32)]
    kern = functools.partial(_s5_kernel, n_steps=n_steps, chained=chained, lane_chunk=512)
    if chained:
        y, hre, him = pl.pallas_call(
            kern,
            grid=(B, L // tt),
            in_specs=[pl.BlockSpec((None, tt, W), lambda b, t: (b, t, 0))] + common_specs,
            out_specs=[pl.BlockSpec((None, tt, W), lambda b, t: (b, t, 0)),
                       pl.BlockSpec((None, 1, n_state), lambda b, t: (b, 0, 0)),
                       pl.BlockSpec((None, 1, n_state), lambda b, t: (b, 0, 0))],
            out_shape=[jax.ShapeDtypeStruct((B, L, W), BF16),
                       jax.ShapeDtypeStruct((B, 1, n_state), F32),
                       jax.ShapeDtypeStruct((B, 1, n_state), F32)],
            scratch_shapes=state_scratch + [pltpu.VMEM((SUBLANES, n_state), F32)] * 2,
            compiler_params=_cparams("parallel", "arbitrary"),
            name="s5_mixer_chained",
        )(u, *common_args)
        return y, hre.reshape(B, n_state), him.reshape(B, n_state)
    u2 = u.reshape(B * L, W)
    h0re, h0im = h0
    y, hre, him = pl.pallas_call(
        kern,
        grid=(B // SUBLANES,),
        in_specs=[pl.BlockSpec((tt, W), lambda i: (i, 0))] + common_specs
        + [pl.BlockSpec((SUBLANES, n_state), lambda i: (i, 0))] * 2,
        out_specs=[pl.BlockSpec((tt, W), lambda i: (i, 0)),
                   pl.BlockSpec((SUBLANES, n_state), lambda i: (i, 0)),
                   pl.BlockSpec((SUBLANES, n_state), lambda i: (i, 0))],
        out_shape=[jax.ShapeDtypeStruct((B * L, W), BF16),
                   jax.ShapeDtypeStruct((B, n_state), F32),
                   jax.ShapeDtypeStruct((B, n_state), F32)],
        scratch_shapes=state_scratch,
        compiler_params=_cparams("parallel"),
        name="s5_mixer_batched",
    )(u2, *common_args, h0re, h0im)
    return y.reshape(B, L, W), hre, him


def _layer_norm(z, g, b):
    mu = jnp.mean(z, axis=-1, keepdims=True)
    zc = z - mu
    var = jnp.mean(zc * zc, axis=-1, keepdims=True)
    return zc * lax.rsqrt(var + LN_EPS) * g + b


def _router_gates(x1, hi, rw_ref, rb_ref):
    lo = (x1 - hi.astype(F32)).astype(BF16)
    both = jnp.dot(hi, rw_ref[...], preferred_element_type=F32)
    lg = (both[:, :LANES] + both[:, LANES:]
          + jnp.dot(lo, rw_ref[:, :LANES], preferred_element_type=F32)) + rb_ref[...]
    lane = lax.broadcasted_iota(jnp.int32, lg.shape, 1)
    big = jnp.int32(LANES)
    is_grp = (lane >= N_EXPERTS) & (lane < N_EXPERTS + N_EXPERT_GROUPS)
    gl = jnp.where(is_grp, lg, NEG_BIG)
    gexp = jnp.where(is_grp, jnp.exp(gl - jnp.max(gl, axis=-1, keepdims=True)), 0.0)
    gprob = gexp / jnp.sum(gexp, axis=-1, keepdims=True)
    g_w = jnp.max(gprob, axis=-1, keepdims=True)
    g_idx = jnp.min(jnp.where(is_grp & (gprob == g_w), lane - N_EXPERTS, big), axis=-1, keepdims=True)
    valid = (lane < N_EXPERTS) & ((lane // EXPERTS_PER_GROUP) == g_idx)
    el = jnp.where(valid, lg, NEG_BIG)
    eexp = jnp.where(valid, jnp.exp(el - jnp.max(el, axis=-1, keepdims=True)), 0.0)
    eprob = eexp / jnp.sum(eexp, axis=-1, keepdims=True)
    w1 = jnp.max(jnp.where(valid, eprob, -1.0), axis=-1, keepdims=True)
    i1 = jnp.min(jnp.where(valid & (eprob == w1), lane, big), axis=-1, keepdims=True)
    rest = valid & (lane != i1)
    w2 = jnp.max(jnp.where(rest, eprob, -1.0), axis=-1, keepdims=True)
    i2 = jnp.min(jnp.where(rest & (eprob == w2), lane, big), axis=-1, keepdims=True)
    den = w1 + w2
    return (jnp.where(lane == i1, g_w * (w1 / den), 0.0)
            + jnp.where(lane == i2, g_w * (w2 / den), 0.0))


def _tail_kernel(att_ref, ssm_ref, x_ref, wa_ref, wb_ref, g1_ref, b1_ref, rw_ref, rb_ref,
                 wg_ref, wu_ref, wd_ref, g2_ref, b2_ref, o_ref, acc_ref, *, alpha):
    mix = (jnp.dot(att_ref[...], wa_ref[...], preferred_element_type=F32)
           + jnp.dot(ssm_ref[...], wb_ref[...], preferred_element_type=F32))
    x1 = _layer_norm(alpha * x_ref[...] + mix, g1_ref[...], b1_ref[...])
    xb = x1.astype(BF16)
    gates = _router_gates(x1, xb, rw_ref, rb_ref)
    for e in range(wg_ref.shape[0]):
        hg = jnp.dot(xb, wg_ref[e], preferred_element_type=F32)
        hu = jnp.dot(xb, wu_ref[e], preferred_element_type=F32)
        h = hg * (1.0 / (1.0 + jnp.exp(-hg))) * hu * gates[:, e:e + 1]
        contrib = jnp.dot(h.astype(BF16), wd_ref[e], preferred_element_type=F32)
        if e == 0:
            acc_ref[...] = contrib
        else:
            acc_ref[...] += contrib
    o_ref[...] = _layer_norm(alpha * x1 + acc_ref[...], g2_ref[...], b2_ref[...])


def _tail(att, ssm, x2d, wa, wb, g1, b1, rw, rb, wg, wu, wd, g2, b2, *, tm, alpha):
    T, D = x2d.shape
    Wh = att.shape[1]
    row = lambda i: (i, 0)
    const2 = lambda i: (0, 0)
    const3 = lambda i: (0, 0, 0)
    once = pl.Buffered(1)
    vec = pl.BlockSpec((1, D), const2)
    return pl.pallas_call(
        functools.partial(_tail_kernel, alpha=alpha),
        grid=(T // tm,),
        in_specs=[pl.BlockSpec((tm, Wh), row), pl.BlockSpec((tm, Wh), row), pl.BlockSpec((tm, D), row),
                  pl.BlockSpec(wa.shape, const2, pipeline_mode=once),
                  pl.BlockSpec(wb.shape, const2, pipeline_mode=once), vec, vec,
                  pl.BlockSpec(rw.shape, const2, pipeline_mode=once), pl.BlockSpec((1, LANES), const2),
                  pl.BlockSpec(wg.shape, const3, pipeline_mode=once),
                  pl.BlockSpec(wu.shape, const3, pipeline_mode=once),
                  pl.BlockSpec(wd.shape, const3, pipeline_mode=once), vec, vec],
        out_specs=pl.BlockSpec((tm, D), row),
        out_shape=jax.ShapeDtypeStruct((T, D), F32),
        scratch_shapes=[pltpu.VMEM((tm, D), F32)],
        compiler_params=_cparams("parallel"),
        name="out_proj_router_moe",
    )(att, ssm, x2d, wa, wb, g1, b1, rw, rb, wg, wu, wd, g2, b2)


def _rope_tables(pos):
    half = HEAD_DIM // 2
    inv = 1.0 / (ROPE_THETA ** (jnp.arange(half, dtype=F32) / half))
    ang = pos.astype(F32)[:, None] * inv[None, :]
    cos = jnp.cos(ang)
    sin = jnp.sin(ang)
    cos_t = jnp.tile(cos, (1, LANES // half))
    sin_t = jnp.tile(jnp.concatenate([-sin, sin], axis=1), (1, LANES // HEAD_DIM))
    return cos_t, sin_t


def _row_tile(n, pref):
    t = min(n, pref)
    while n % t:
        t //= 2
    return t


def kernel(x_prompt, x_sample, cache_k, cache_v, state_ssm_re, state_ssm_im, w_in, lam_q1, lam_k1, lam_q2, lam_k2, subln_g, ssm_a_re, ssm_a_im, ssm_log_dt, ssm_b_re, ssm_b_im, ssm_c_re, ssm_c_im, ssm_d, w_glu, ssm_norm_g, w_out, ln1_g, ln1_b, w_grp, b_grp, w_rt, b_rt, w_gate, w_up, w_down, ln2_g, ln2_b):
    depth = w_in.shape[0]
    assert depth == 1, "single-layer step"
    B, L, D = x_prompt.shape
    Bs, S, _ = x_sample.shape
    P = cache_k.shape[2]
    H = cache_k.shape[3]
    n_qk = H * 2 * HEAD_DIM
    G, N = ssm_a_re.shape[1], ssm_a_re.shape[2]
    alpha = (2.0 * depth) ** 0.25
    lam_init = 0.8 - 0.6 * math.exp(-0.3 * 0)
    out_scale = 1.0 - lam_init
    l = 0

    w_in_bf = w_in[l].astype(BF16)
    lam = (jnp.exp(jnp.sum(lam_q1[l].astype(F32) * lam_k1[l].astype(F32)))
           - jnp.exp(jnp.sum(lam_q2[l].astype(F32) * lam_k2[l].astype(F32))) + lam_init).reshape(1).astype(F32)
    g_sub = subln_g[l].astype(F32).reshape(1, V_DIM)
    w_glu_bf = w_glu[l].astype(BF16)
    ssm_w = D - H * V_DIM
    d_row = ssm_d[l].astype(F32).reshape(1, ssm_w)
    ng_row = ssm_norm_g[l].astype(F32).reshape(1, ssm_w)
    wa = w_out[l, :H * V_DIM].astype(BF16)
    wb = w_out[l, H * V_DIM:].astype(BF16)
    ln1g, ln1b = ln1_g[l].astype(F32).reshape(1, D), ln1_b[l].astype(F32).reshape(1, D)
    ln2g, ln2b = ln2_g[l].astype(F32).reshape(1, D), ln2_b[l].astype(F32).reshape(1, D)
    w_r = jnp.concatenate([w_rt[l].astype(F32), w_grp[l].astype(F32),
                           jnp.zeros((D, LANES - N_EXPERTS - N_EXPERT_GROUPS), F32)], axis=1)
    rhi = w_r.astype(BF16)
    rw = jnp.concatenate([rhi, (w_r - rhi.astype(F32)).astype(BF16)], axis=1)
    rb =jnp.concatenate([b_rt[l].astype(F32), b_grp[l].astype(F32),
                          jnp.zeros((LANES - N_EXPERTS - N_EXPERT_GROUPS,), F32)]).reshape(1, LANES)
    wg, wu, wd = w_gate[l].astype(BF16), w_up[l].astype(BF16), w_down[l].astype(BF16)

    def tail(att, ssm, x2d):
        return _tail(att, ssm, x2d, wa, wb, ln1g, ln1b, rw, rb, wg, wu, wd, ln2g, ln2b,
                     tm=_row_tile(x2d.shape[0], 512), alpha=alpha)

    cos_p, sin_p = _rope_tables(jnp.arange(L))
    xp2 = x_prompt.reshape(B * L, D)
    qp, kpf, kpb, vpf, vpb, up = _in_proj(xp2, w_in_bf, cos_p, sin_p, tm=_row_tile(L, 512),
                                          t_attn=_row_tile(L, 512))
    r3 = lambda a: a.reshape(B, L, -1)
    att_p = _attn_prompt(lam, qp, r3(kpb), vpb, g_sub, out_scale=out_scale)
    n_steps_p = _row_tile(L, 256) // SUBLANES
    tabs_p = _s5_tables(ssm_a_re[l], ssm_a_im[l], ssm_log_dt[l], ssm_b_re[l].astype(F32), ssm_b_im[l].astype(F32),
                        ssm_c_re[l], ssm_c_im[l], n_steps_p)
    ssm_p, hre_p, him_p = _s5_mixer(r3(up), tabs_p, d_row, w_glu_bf, ng_row, n_steps=n_steps_p)
    y_prompt = tail(att_p.reshape(B * L, -1), ssm_p.reshape(B * L, -1), xp2).reshape(B, L, D)

    cos_s, sin_s = _rope_tables(P + jnp.arange(S))
    cos_s, sin_s = jnp.tile(cos_s, (Bs, 1)), jnp.tile(sin_s, (Bs, 1))
    xs2 = x_sample.reshape(Bs * S, D)
    qs, ksf, ksb, vsf, vsb, us = _in_proj(xs2, w_in_bf, cos_s, sin_s, tm=_row_tile(Bs * S, 512))
    s3 = lambda a: a.reshape(Bs, S, -1)
    cache_kt = jnp.transpose(cache_k[l], (0, 2, 3, 4, 1)).reshape(Bs * n_qk, P)
    att_s = _attn_sample(lam, s3(qs), cache_kt, cache_v[l].reshape(Bs * P * H, V_DIM),
                         s3(ksb), s3(vsb), g_sub, out_scale=out_scale)
    tabs_s = _s5_tables(ssm_a_re[l], ssm_a_im[l], ssm_log_dt[l], ssm_b_re[l].astype(F32), ssm_b_im[l].astype(F32),
                        ssm_c_re[l], ssm_c_im[l], S)
    h0 = (state_ssm_re[l].astype(F32).reshape(Bs, G * N), state_ssm_im[l].astype(F32).reshape(Bs, G * N))
    ssm_s, hre_s, him_s = _s5_mixer(s3(us), tabs_s, d_row, w_glu_bf, ng_row, n_steps=S, h0=h0)
    y_sample = tail(att_s.reshape(Bs * S, -1), ssm_s.reshape(Bs * S, -1), xs2).reshape(Bs, S, D)

    return (y_prompt, y_sample,
            jnp.transpose(kpf.reshape(1, B, H, 2, HEAD_DIM, L), (0, 1, 5, 2, 3, 4)), vpf.reshape(1, B, L, H, V_DIM),
            hre_p.reshape(1, B, G, N), him_p.reshape(1, B, G, N),
            ksf.reshape(1, Bs, S, H, 2, HEAD_DIM), vsf.reshape(1, Bs, S, H, V_DIM),
            hre_s.reshape(1, Bs, G, N), him_s.reshape(1, Bs, G, N))
```

```python
import functools
import math

import jax
import jax.numpy as jnp
from jax import lax
from jax.experimental import pallas as pl
from jax.experimental.pallas import tpu as pltpu

F32 = jnp.float32
BF16 = jnp.bfloat16

HEAD_DIM = 64
V_DIM = 2 * HEAD_DIM
CHUNK = 64
SSM_GROUP = 16
SSM_STATE = 64
N_EXPERT_GROUPS = 4
EXPERTS_PER_GROUP = 4
N_EXPERTS = N_EXPERT_GROUPS * EXPERTS_PER_GROUP
ROPE_THETA = 10000.0
LN_EPS = 1e-5
RMS_EPS = 1e-6
LANES = 128
SUBLANES = 8
MXU_N = 256
ATTN_UNROLL = 4
NEG_BIG = -1e30
SUM_ROWS = 16
LOG2E = 1.4426950408889634
VMEM_LIMIT = 52 * 1024 * 1024


def _cparams(*sem):
    return pltpu.CompilerParams(dimension_semantics=sem, vmem_limit_bytes=VMEM_LIMIT)


def _store_transposed_tiles(ref, a):
    n_t, n_h, _, t = ref.shape
    at = a.T
    for n in range(n_t):
        for h in range(n_h):
            ref[n, h] = at[V_DIM * h:V_DIM * (h + 1), t * n:t * (n + 1)].astype(ref.dtype)


def _inproj_kernel(x_ref, w_ref, cos_ref, sin_ref, q_ref, kf_ref, kb_ref, vf_ref, vb_ref, u_ref,
                   *, q_scale, n_qk, transposed):
    xb = x_ref[...].astype(BF16)
    cos = cos_ref[...]
    sin = sin_ref[...]
    lane = lax.broadcasted_iota(jnp.int32, cos.shape, 1)
    first = (lane & (HEAD_DIM - 1)) < HEAD_DIM // 2

    def rope(t):
        rot = jnp.where(first, pltpu.roll(t, LANES - HEAD_DIM // 2, 1), pltpu.roll(t, HEAD_DIM // 2, 1))
        return t * cos + rot * sin

    pq = jnp.dot(xb, w_ref[:, 0:n_qk], preferred_element_type=F32)
    qr = jnp.concatenate([rope(pq[:, LANES * j:LANES * (j + 1)]) for j in range(n_qk // LANES)], axis=1) * q_scale
    if transposed:
        _store_transposed_tiles(q_ref, qr)
    else:
        q_ref[...] = qr.astype(BF16)
    pk = jnp.dot(xb, w_ref[:, n_qk:2 * n_qk], preferred_element_type=F32)
    kr = jnp.concatenate([rope(pk[:, LANES * j:LANES * (j + 1)]) for j in range(n_qk // LANES)], axis=1)
    kb_ref[...] = kr.astype(BF16)
    n_v = u_ref.shape[1]
    pv = jnp.dot(xb, w_ref[:, 2 * n_qk:2 * n_qk + n_v], preferred_element_type=F32)
    if transposed:
        kf_ref[...] = kr.T
        n_h = n_v // V_DIM
        for h in range(n_h):
            vf_ref[pl.ds(h, pv.shape[0], stride=n_h), :] = pv[:, V_DIM * h:V_DIM * (h + 1)]
        _store_transposed_tiles(vb_ref, pv)
    else:
        kf_ref[...] = kr
        vf_ref[...] = pv
        vb_ref[...] = pv.astype(BF16)
    u_ref[...] = jnp.dot(xb, w_ref[:, 2 * n_qk + n_v:], preferred_element_type=F32)


def _in_proj(x2d, w_bf, cos_t, sin_t, *, tm, t_attn=None):
    T, D = x2d.shape
    n_cols = w_bf.shape[1]
    n_qk = n_cols // 4
    n_tab = cos_t.shape[0] // tm
    row = lambda i: (i, 0)
    tab = lambda i: (i % n_tab, 0)
    out_sds = lambda dt: jax.ShapeDtypeStruct((T, n_qk), dt)
    spec = pl.BlockSpec((tm, n_qk), row)
    qv_spec, qv_sds = spec, out_sds(BF16)
    kf_spec, kf_sds, vf_spec, vf_sds = spec, out_sds(F32), spec, out_sds(F32)
    if t_attn is not None:
        H = n_qk // V_DIM
        qv_spec = pl.BlockSpec((tm // t_attn, H, V_DIM, t_attn), lambda i: (i, 0, 0, 0))
        qv_sds = jax.ShapeDtypeStruct((T // t_attn, H, V_DIM, t_attn), BF16)
        kf_spec = pl.BlockSpec((None, n_qk, tm), lambda i: (i // n_tab, 0, i % n_tab))
        kf_sds = jax.ShapeDtypeStruct((T // cos_t.shape[0], n_qk, cos_t.shape[0]), F32)
        vf_spec = pl.BlockSpec((tm * H, V_DIM), row)
        vf_sds = jax.ShapeDtypeStruct((T * H, V_DIM), F32)
    return pl.pallas_call(
        functools.partial(_inproj_kernel, q_scale=HEAD_DIM ** -0.5 * LOG2E, n_qk=n_qk,
                          transposed=t_attn is not None),
        grid=(T // tm,),
        in_specs=[pl.BlockSpec((tm, D), row),
                  pl.BlockSpec((D, n_cols), lambda i: (0, 0)),
                  pl.BlockSpec((tm, LANES), tab),
                  pl.BlockSpec((tm, LANES), tab)],
        out_specs=[qv_spec, kf_spec, spec, vf_spec, qv_spec, spec],
        out_shape=[qv_sds, kf_sds, out_sds(BF16), vf_sds, qv_sds, out_sds(F32)],
        compiler_params=_cparams("parallel"),
        name="in_proj_rope",
    )(x2d, w_bf, cos_t, sin_t)


def _split_maps(q):
    lane = lax.broadcasted_iota(jnp.int32, q.shape, 1)
    zero = jnp.zeros_like(q)
    return jnp.concatenate([jnp.where(lane < HEAD_DIM, q, zero), jnp.where(lane >= HEAD_DIM, q, zero)], axis=0)


def _qk(qs, kb):
    return lax.dot_general(qs, kb, (((1,), (1,)), ((), ())), preferred_element_type=F32)


def _subln(o, g, out_scale):
    ms = jnp.mean(o * o, axis=-1, keepdims=True)
    return o * lax.rsqrt(ms + RMS_EPS) * g * out_scale


def _attn_prompt_kernel(lam_ref, qt_ref, k_ref, vt_ref, g_ref, o_ref, qs_ref, acc_ref, s0_ref, s1_ref, m_ref,
                        *, tq, out_scale):
    i = pl.program_id(2)
    s_buf = (s0_ref, s1_ref)
    n_cb = 2 * tq // MXU_N
    qt = qt_ref[...]
    row = lax.broadcasted_iota(jnp.int32, qt.shape, 0)
    zero = jnp.zeros_like(qt)
    q_maps = (jnp.where(row < HEAD_DIM, qt, zero), jnp.where(row >= HEAD_DIM, qt, zero))
    for cb in range(n_cb):
        j = cb % (n_cb // 2)
        qs_ref[cb] = q_maps[cb // (n_cb // 2)][:, MXU_N * j:MXU_N * (j + 1)]
    acc_ref[...] = jnp.zeros(acc_ref.shape, F32)
    m_ref[...] = jnp.full(m_ref.shape, NEG_BIG, F32)
    ones_rows = jnp.ones((SUM_ROWS, tq), BF16)

    def scores(kt, slot):
        kb = k_ref[pl.ds(pl.multiple_of(kt * tq, tq), tq), :]
        for cb in range(n_cb):
            s_buf[slot][cb] = jnp.dot(kb, qs_ref[cb], preferred_element_type=F32)

    def softmax_pv(kt, slot, masked):
        lhs = jnp.concatenate([vt_ref[kt], ones_rows], axis=0)
        for cb in range(n_cb):
            st = s_buf[slot][cb]
            if masked:
                c = lax.broadcasted_iota(jnp.int32, st.shape, 0)
                r = (lax.broadcasted_iota(jnp.int32, st.shape, 1) + MXU_N * cb) % tq
                st = jnp.where((c // CHUNK) <= (r // CHUNK), st, NEG_BIG)
            m_prev = m_ref[cb]
            m_new = jnp.maximum(m_prev, jnp.max(st, axis=0, keepdims=True))
            alpha = jnp.exp2(m_prev - m_new)
            pt = jnp.exp2(st - m_new).astype(BF16)
            m_ref[cb] = m_new
            acc_ref[cb] = alpha * acc_ref[cb] + jnp.dot(lhs, pt, preferred_element_type=F32)

    def step(t, par):
        scores(t, par)
        softmax_pv(t - 1, 1 - par, False)

    scores(0, 0)

    def steps(t0, n):
        for d in range(1, n + 1):
            step(t0 + d, d % 2)

    def body(j, carry):
        steps(ATTN_UNROLL * j, ATTN_UNROLL)
        return carry

    lax.fori_loop(0, i // ATTN_UNROLL, body, 0)
    t_done = (i // ATTN_UNROLL) * ATTN_UNROLL
    for rem in range(ATTN_UNROLL):
        @pl.when(i - t_done == rem)
        def _(rem=rem):
            steps(t_done, rem)
            softmax_pv(i, rem % 2, True)

    def normalized(cb):
        return acc_ref[cb, 0:V_DIM, :] * (1.0 / acc_ref[cb, V_DIM:V_DIM + 1, :])

    half = n_cb // 2
    ot = jnp.concatenate([normalized(j) - lam_ref[0] * normalized(j + half) for j in range(half)], axis=1)
    o_ref[...] = _subln(ot.T, g_ref[...], out_scale).astype(o_ref.dtype)


def _attn_prompt(lam, qt, k, vt, g, *, out_scale):
    B, L, W = k.shape
    n_t, H, _, t = qt.shape
    nq = L // t
    return pl.pallas_call(
        functools.partial(_attn_prompt_kernel, tq=t, out_scale=out_scale),
        grid=(B, H, nq),
        in_specs=[pl.BlockSpec(memory_space=pltpu.SMEM),
                  pl.BlockSpec((None, None, V_DIM, t), lambda b, h, i: (b * nq + i, h, 0, 0)),
                  pl.BlockSpec((None, L, V_DIM), lambda b, h, i: (b, 0, h)),
                  pl.BlockSpec((nq, None, V_DIM, t), lambda b, h, i: (b, h, 0, 0)),
                  pl.BlockSpec((1, V_DIM), lambda b, h, i: (0, 0))],
        out_specs=pl.BlockSpec((None, t, V_DIM), lambda b, h, i: (b, i, h)),
        out_shape=jax.ShapeDtypeStruct((B, L, W), BF16),
        scratch_shapes=[pltpu.VMEM((2 * t // MXU_N, V_DIM, MXU_N), BF16),
                        pltpu.VMEM((2 * t // MXU_N, V_DIM + SUM_ROWS, MXU_N), F32),
                        pltpu.VMEM((2 * t // MXU_N, t, MXU_N), F32), pltpu.VMEM((2 * t // MXU_N, t, MXU_N), F32),
                        pltpu.VMEM((2 * t // MXU_N, 1, MXU_N), F32)],
        compiler_params=_cparams("parallel", "parallel", "parallel"),
        name="diff_attn_prompt",
    )(lam, qt, k, vt, g)


def _attn_sample_kernel(lam_ref, q_ref, kct_ref, vc_ref, kn_ref, vn_ref, g_ref, o_ref, *, past_len, n_heads,
                        out_scale):
    for h in range(n_heads):
        hs = slice(V_DIM * h, V_DIM * (h + 1))
        vc = vc_ref[pl.ds(h, past_len, stride=n_heads), :]
        o_ref[:, hs] = _attn_sample_head(lam_ref[0], q_ref[:, hs], kct_ref[hs, :], vc, kn_ref[:, hs], vn_ref[:, hs],
                                         g_ref[...], past_len, out_scale).astype(o_ref.dtype)


def _attn_sample_head(lam, q, kct, vc, kn, vn, g, past_len, out_scale):
    tq = q.shape[0]
    qs = _split_maps(q)
    s_c = jnp.dot(qs, kct.astype(BF16), preferred_element_type=F32)
    s_n = _qk(qs, kn)
    r = lax.broadcasted_iota(jnp.int32, s_n.shape, 0)
    c = lax.broadcasted_iota(jnp.int32, s_n.shape, 1)
    r = jnp.where(r >= tq, r - tq, r)
    s_n = jnp.where(((past_len + c) // CHUNK) <= ((past_len + r) // CHUNK), s_n, NEG_BIG)
    m = jnp.maximum(jnp.max(s_c, axis=-1, keepdims=True), jnp.max(s_n, axis=-1, keepdims=True))
    p_c = jnp.exp2(s_c - m)
    p_n = jnp.exp2(s_n - m)
    inv_l = 1.0 / (jnp.sum(p_c, axis=-1, keepdims=True) + jnp.sum(p_n, axis=-1, keepdims=True))
    p_c = p_c * inv_l
    p_n = p_n * inv_l
    w_c = p_c[0:tq] - lam * p_c[tq:2 * tq]
    w_n = p_n[0:tq] - lam * p_n[tq:2 * tq]
    o = (jnp.dot(w_c.astype(BF16), vc.astype(BF16), preferred_element_type=F32)
         + jnp.dot(w_n.astype(BF16), vn, preferred_element_type=F32))
    return _subln(o, g, out_scale)


def _attn_sample(lam, q, cache_kt, cache_v, k_new, v_new, g, *, out_scale):
    B, S, W = q.shape
    P = cache_kt.shape[1]
    H = W // V_DIM
    blk_new = pl.BlockSpec((None, S, W), lambda b: (b, 0, 0))
    return pl.pallas_call(
        functools.partial(_attn_sample_kernel, past_len=P, n_heads=H, out_scale=out_scale),
        grid=(B,),
        in_specs=[pl.BlockSpec(memory_space=pltpu.SMEM), blk_new,
                  pl.BlockSpec((W, P), lambda b: (b, 0)),
                  pl.BlockSpec((P * H, V_DIM), lambda b: (b, 0)),
                  blk_new, blk_new, pl.BlockSpec((1, V_DIM), lambda b: (0, 0))],
        out_specs=blk_new,
        out_shape=jax.ShapeDtypeStruct((B, S, W), BF16),
        compiler_params=_cparams("parallel"),
        name="diff_attn_sample",
    )(lam, q, cache_kt, cache_v, k_new, v_new, g)


def _cmul(ar, ai, br, bi):
    return ar * br - ai * bi, ar * bi + ai * br


def _s5_kernel(*refs, n_steps, chained, lane_chunk):
    if chained:
        (u_ref, perm_ref, permt_ref, wx_ref, cb_ref, d_ref, pre_ref, pim_ref, gre_ref, gim_ref,
         wglu_ref, ng_ref, y_ref, hre_ref, him_ref, xre, xim, cre, cim) = refs
    else:
        (u_ref, perm_ref, permt_ref, wx_ref, cb_ref, d_ref, pre_ref, pim_ref, gre_ref, gim_ref,
         wglu_ref, ng_ref, h0re_ref, h0im_ref, y_ref, hre_ref, him_ref, xre, xim) = refs
    R = n_steps
    n_state = xre.shape[1]
    n_blk = wx_ref.shape[0]
    blk_in = wx_ref.shape[1]
    blk_st = wx_ref.shape[2] // 2

    u = u_ref[...]
    u_hi = u.astype(BF16)
    u_lo = (u - u_hi.astype(F32)).astype(BF16)
    perm = perm_ref[...]
    up = (jnp.dot(perm, u_hi, preferred_element_type=F32) + jnp.dot(perm, u_lo, preferred_element_type=F32))
    upb = up.astype(BF16)
    for j in range(n_blk):
        xj = jnp.dot(upb[:, blk_in * j:blk_in * (j + 1)], wx_ref[j], preferred_element_type=F32)
        xre[:, blk_st * j:blk_st * (j + 1)] = xj[:, :blk_st]
        xim[:, blk_st * j:blk_st * (j + 1)] = xj[:, blk_st:]

    n_chunks = n_state // lane_chunk
    e_re, e_im = [], []
    for c in range(n_chunks):
        cs = slice(lane_chunk * c, lane_chunk * (c + 1))
        ar = jnp.broadcast_to(pre_ref[0:1, cs], (SUBLANES, lane_chunk))
        ai = jnp.broadcast_to(pim_ref[0:1, cs], (SUBLANES, lane_chunk))

        def scan_body(r, h, cs=cs, ar=ar, ai=ai):
            hr, hi = h
            rows = pl.ds(pl.multiple_of(r * SUBLANES, SUBLANES), SUBLANES)
            nr = ar * hr - ai * hi + xre[rows, cs]
            ni = ar * hi + ai * hr + xim[rows, cs]
            xre[rows, cs] = nr
            xim[rows, cs] = ni
            return nr, ni

        z = jnp.zeros((SUBLANES, lane_chunk), F32)
        er, ei = lax.fori_loop(0, R, scan_body, (z, z), unroll=4)
        e_re.append(er)
        e_im.append(ei)
    e_re = jnp.concatenate(e_re, axis=1)
    e_im = jnp.concatenate(e_im, axis=1)

    g1r = jnp.broadcast_to(gre_ref[0:1, :], e_re.shape)
    g1i = jnp.broadcast_to(gim_ref[0:1, :], e_re.shape)
    if chained:
        t = pl.program_id(1)

        @pl.when(t == 0)
        def _():
            cre[...] = jnp.zeros(cre.shape, F32)
            cim[...] = jnp.zeros(cim.shape, F32)

        sub = lax.broadcasted_iota(jnp.int32, e_re.shape, 0)
        c_r = cre[...]
        c_i = cim[...]
        inj_r, inj_i = _cmul(g1r, g1i, c_r, c_i)
        xr = e_re + jnp.where(sub == 0, inj_r, 0.0)
        xi = e_im + jnp.where(sub == 0, inj_i, 0.0)
        for lvl, k in enumerate((1, 2, 4)):
            gr = jnp.broadcast_to(gre_ref[lvl:lvl + 1, :], e_re.shape)
            gi = jnp.broadcast_to(gim_ref[lvl:lvl + 1, :], e_re.shape)
            sr = jnp.where(sub >= k, pltpu.roll(xr, k, 0), 0.0)
            si = jnp.where(sub >= k, pltpu.roll(xi, k, 0), 0.0)
            tr, ti = _cmul(gr, gi, sr, si)
            xr = xr + tr
            xi = xi + ti
        hend_r, hend_i = xr, xi
        hc_r = jnp.where(sub == 0, c_r, pltpu.roll(hend_r, 1, 0))
        hc_i = jnp.where(sub == 0, c_i, pltpu.roll(hend_i, 1, 0))
        last_r = jnp.broadcast_to(hend_r[SUBLANES - 1:SUBLANES, :], e_re.shape)
        last_i = jnp.broadcast_to(hend_i[SUBLANES - 1:SUBLANES, :], e_re.shape)
        cre[...] = last_r
        cim[...] = last_i
        hre_ref[...] = last_r[0:1, :]
        him_ref[...] = last_i[0:1, :]
    else:
        hc_r = h0re_ref[...]
        hc_i = h0im_ref[...]
        tr, ti = _cmul(g1r, g1i, hc_r, hc_i)
        hre_ref[...] = e_re + tr
        him_ref[...] = e_im + ti

    for c in range(n_chunks):
        cs = slice(lane_chunk * c, lane_chunk * (c + 1))
        hcr = hc_r[:, cs]
        hci = hc_i[:, cs]

        def fix_body(r, carry, cs=cs, hcr=hcr, hci=hci):
            rows = pl.ds(pl.multiple_of(r * SUBLANES, SUBLANES), SUBLANES)
            pr = jnp.broadcast_to(pre_ref[pl.ds(r, 1), cs], hcr.shape)
            pi = jnp.broadcast_to(pim_ref[pl.ds(r, 1), cs], hcr.shape)
            tr, ti = _cmul(pr, pi, hcr, hci)
            xre[rows, cs] = xre[rows, cs] + tr
            xim[rows, cs] = xim[rows, cs] + ti
            return carry

        lax.fori_loop(0, R, fix_body, 0, unroll=4)

    ys = []
    for j in range(n_blk):
        st = slice(blk_st * j, blk_st * (j + 1))
        ys.append(jnp.dot(xre[:, st].astype(BF16), cb_ref[j, 0:blk_st, :], preferred_element_type=F32)
                  + jnp.dot(xim[:, st].astype(BF16), cb_ref[j, blk_st:, :], preferred_element_type=F32))
    y = jnp.concatenate(ys, axis=1) + d_ref[...] * up
    y = 0.5 * y * (1.0 + jnp.tanh(math.sqrt(2.0 / math.pi) * (y + 0.044715 * (y * y * y))))
    gl = jnp.dot(y.astype(BF16), wglu_ref[...], preferred_element_type=F32)
    half = gl.shape[1] // 2
    z = gl[:, :half] * (1.0 / (1.0 + jnp.exp(-gl[:, half:])))
    z = z * lax.rsqrt(jnp.mean(z * z, axis=-1, keepdims=True) + RMS_EPS) * ng_ref[...]
    y_ref[...] = jnp.dot(permt_ref[...], z.astype(BF16), preferred_element_type=F32).astype(y_ref.dtype)


def _s5_tables(a_re, a_im, log_dt, b_re, b_im, c_re, c_im, n_steps):
    G, N = a_re.shape
    C = b_re.shape[2]
    dt = jnp.exp(log_dt.astype(F32))[:, None]
    a_re = a_re.astype(F32)
    a_im = a_im.astype(F32)

    def power(k):
        mag = jnp.exp(k * a_re * dt)
        return mag * jnp.cos(k * a_im * dt), mag * jnp.sin(k * a_im * dt)

    ab_re, ab_im = power(1.0)
    nr, ni = ab_re - 1.0, ab_im
    den = a_re * a_re + a_im * a_im
    cf_re = (nr * a_re + ni * a_im) / den
    cf_im = (ni * a_re - nr * a_im) / den
    ks = jnp.arange(1, n_steps + 1, dtype=F32)[:, None, None]
    p_re, p_im = power(ks)
    g_re, g_im = zip(*[power(float(m * n_steps)) for m in (1, 2, 4)])
    pad = jnp.zeros((SUBLANES - 3, G * N), F32)
    g_re = jnp.concatenate([jnp.stack(g_re).reshape(3, G * N), pad])
    g_im = jnp.concatenate([jnp.stack(g_im).reshape(3, G * N), pad])
    wx_re = cf_re[..., None] * b_re - cf_im[..., None] * b_im
    wx_im = cf_re[..., None] * b_im + cf_im[..., None] * b_re
    gpb = LANES // C
    nb = G // gpb
    eye = jnp.eye(gpb, dtype=F32)

    def in_block(w):
        w = w.reshape(nb, gpb, N, C)
        return jnp.einsum('jgnc,gh->jgchn', w, eye).reshape(nb, gpb * C, gpb * N)

    def out_block(w):
        w = w.reshape(nb, gpb, C, N)
        return jnp.einsum('jgcn,gh->jgnhc', w, eye).reshape(nb, gpb * N, gpb * C)

    wx = jnp.concatenate([in_block(wx_re), in_block(wx_im)], axis=2).astype(BF16)
    cb = jnp.concatenate([out_block(c_re.astype(F32)), out_block(-c_im.astype(F32))], axis=1).astype(BF16)
    return (p_re.reshape(n_steps, G * N), p_im.reshape(n_steps, G * N), g_re, g_im, wx, cb)


def _perm_matrices(n_steps):
    tt = SUBLANES * n_steps
    dst = jnp.arange(tt)
    src = (dst % SUBLANES) * n_steps + dst // SUBLANES
    perm = (src[:, None] == jnp.arange(tt)[None, :]).astype(BF16)
    return perm, perm.T


def _s5_mixer(u, tables, d, w_glu_bf, norm_g, *, n_steps, h0=None):
    B, L, W = u.shape
    p_re, p_im, g_re, g_im, wx, cb = tables
    n_state = p_re.shape[1]
    tt = SUBLANES * n_steps
    perm, permt = _perm_matrices(n_steps)
    chained = h0 is None
    const2 = lambda *_: (0, 0)
    const3 = lambda *_: (0, 0, 0)
    common_specs = [pl.BlockSpec(perm.shape, const2), pl.BlockSpec(perm.shape, const2),
                    pl.BlockSpec(wx.shape, const3), pl.BlockSpec(cb.shape, const3),
                    pl.BlockSpec((1, W), const2),
                    pl.BlockSpec(p_re.shape, const2), pl.BlockSpec(p_im.shape, const2),
                    pl.BlockSpec(g_re.shape, const2), pl.BlockSpec(g_im.shape, const2),
                    pl.BlockSpec(w_glu_bf.shape, const2), pl.BlockSpec((1, W), const2)]
    common_args = (perm, permt, wx, cb, d, p_re, p_im, g_re, g_im, w_glu_bf, norm_g)
    state_scratch = [pltpu.VMEM((tt, n_state), F32), pltpu.VMEM((tt, n_state), F32)]
    kern = functools.partial(_s5_kernel, n_steps=n_steps, chained=chained, lane_chunk=512)
    if chained:
        y, hre, him = pl.pallas_call(
            kern,
            grid=(B, L // tt),
            in_specs=[pl.BlockSpec((None, tt, W), lambda b, t: (b, t, 0))] + common_specs,
            out_specs=[pl.BlockSpec((None, tt, W), lambda b, t: (b, t, 0)),
                       pl.BlockSpec((None, 1, n_state), lambda b, t: (b, 0, 0)),
                       pl.BlockSpec((None, 1, n_state), lambda b, t: (b, 0, 0))],
            out_shape=[jax.ShapeDtypeStruct((B, L, W), BF16),
                       jax.ShapeDtypeStruct((B, 1, n_state), F32),
                       jax.ShapeDtypeStruct((B, 1, n_state), F32)],
            scratch_shapes=state_scratch + [pltpu.VMEM((SUBLANES, n_state), F32)] * 2,
            compiler_params=_cparams("parallel", "arbitrary"),
            name="s5_mixer_chained",
        )(u, *common_args)
        return y, hre.reshape(B, n_state), him.reshape(B, n_state)
    u2 = u.reshape(B * L, W)
    h0re, h0im = h0
    y, hre, him = pl.pallas_call(
        kern,
        grid=(B // SUBLANES,),
        in_specs=[pl.BlockSpec((tt, W), lambda i: (i, 0))] + common_specs
        + [pl.BlockSpec((SUBLANES, n_state), lambda i: (i, 0))] * 2,
        out_specs=[pl.BlockSpec((tt, W), lambda i: (i, 0)),
                   pl.BlockSpec((SUBLANES, n_state), lambda i: (i, 0)),
                   pl.BlockSpec((SUBLANES, n_state), lambda i: (i, 0))],
        out_shape=[jax.ShapeDtypeStruct((B * L, W), BF16),
                   jax.ShapeDtypeStruct((B, n_state), F32),
                   jax.ShapeDtypeStruct((B, n_state), F32)],
        scratch_shapes=state_scratch,
        compiler_params=_cparams("parallel"),
        name="s5_mixer_batched",
    )(u2, *common_args, h0re, h0im)
    return y.reshape(B, L, W), hre, him


def _layer_norm(z, g, b):
    mu = jnp.mean(z, axis=-1, keepdims=True)
    zc = z - mu
    var = jnp.mean(zc * zc, axis=-1, keepdims=True)
    return zc * lax.rsqrt(var + LN_EPS) * g + b


def _router_gates(x1, hi, rw_ref, rb_ref):
    lo = (x1 - hi.astype(F32)).astype(BF16)
    both = jnp.dot(hi, rw_ref[...], preferred_element_type=F32)
    lg = (both[:, :LANES] + both[:, LANES:]
          + jnp.dot(lo, rw_ref[:, :LANES], preferred_element_type=F32)) + rb_ref[...]
    lane = lax.broadcasted_iota(jnp.int32, lg.shape, 1)
    big = jnp.int32(LANES)
    is_grp = (lane >= N_EXPERTS) & (lane < N_EXPERTS + N_EXPERT_GROUPS)
    gl = jnp.where(is_grp, lg, NEG_BIG)
    gexp = jnp.where(is_grp, jnp.exp(gl - jnp.max(gl, axis=-1, keepdims=True)), 0.0)
    gprob = gexp / jnp.sum(gexp, axis=-1, keepdims=True)
    g_w = jnp.max(gprob, axis=-1, keepdims=True)
    g_idx = jnp.min(jnp.where(is_grp & (gprob == g_w), lane - N_EXPERTS, big), axis=-1, keepdims=True)
    valid = (lane < N_EXPERTS) & ((lane // EXPERTS_PER_GROUP) == g_idx)
    el = jnp.where(valid, lg, NEG_BIG)
    eexp = jnp.where(valid, jnp.exp(el - jnp.max(el, axis=-1, keepdims=True)), 0.0)
    eprob = eexp / jnp.sum(eexp, axis=-1, keepdims=True)
    w1 = jnp.max(jnp.where(valid, eprob, -1.0), axis=-1, keepdims=True)
    i1 = jnp.min(jnp.where(valid & (eprob == w1), lane, big), axis=-1, keepdims=True)
    rest = valid & (lane != i1)
    w2 = jnp.max(jnp.where(rest, eprob, -1.0), axis=-1, keepdims=True)
    i2 = jnp.min(jnp.where(rest & (eprob == w2), lane, big), axis=-1, keepdims=True)
    den = w1 + w2
    return (jnp.where(lane == i1, g_w * (w1 / den), 0.0)
            + jnp.where(lane == i2, g_w * (w2 / den), 0.0))


def _tail_kernel(att_ref, ssm_ref, x_ref, wa_ref, wb_ref, g1_ref, b1_ref, rw_ref, rb_ref,
                 wg_ref, wu_ref, wd_ref, g2_ref, b2_ref, o_ref, acc_ref, *, alpha):
    mix = (jnp.dot(att_ref[...], wa_ref[...], preferred_element_type=F32)
           + jnp.dot(ssm_ref[...], wb_ref[...], preferred_element_type=F32))
    x1 = _layer_norm(alpha * x_ref[...] + mix, g1_ref[...], b1_ref[...])
    xb = x1.astype(BF16)
    gates = _router_gates(x1, xb, rw_ref, rb_ref)
    for e in range(wg_ref.shape[0]):
        hg = jnp.dot(xb, wg_ref[e], preferred_element_type=F32)
        hu = jnp.dot(xb, wu_ref[e], preferred_element_type=F32)
        h = hg * (1.0 / (1.0 + jnp.exp(-hg))) * hu * gates[:, e:e + 1]
        contrib = jnp.dot(h.astype(BF16), wd_ref[e], preferred_element_type=F32)
        if e == 0:
            acc_ref[...] = contrib
        else:
            acc_ref[...] += contrib
    o_ref[...] = _layer_norm(alpha * x1 + acc_ref[...], g2_ref[...], b2_ref[...])


def _tail(att, ssm, x2d, wa, wb, g1, b1, rw, rb, wg, wu, wd, g2, b2, *, tm, alpha):
    T, D = x2d.shape
    Wh = att.shape[1]
    row = lambda i: (i, 0)
    const2 = lambda i: (0, 0)
    const3 = lambda i: (0, 0, 0)
    once = pl.Buffered(1)
    vec = pl.BlockSpec((1, D), const2)
    return pl.pallas_call(
        functools.partial(_tail_kernel, alpha=alpha),
        grid=(T // tm,),
        in_specs=[pl.BlockSpec((tm, Wh), row), pl.BlockSpec((tm, Wh), row), pl.BlockSpec((tm, D), row),
                  pl.BlockSpec(wa.shape, const2, pipeline_mode=once),
                  pl.BlockSpec(wb.shape, const2, pipeline_mode=once), vec, vec,
                  pl.BlockSpec(rw.shape, const2, pipeline_mode=once), pl.BlockSpec((1, LANES), const2),
                  pl.BlockSpec(wg.shape, const3, pipeline_mode=once),
                  pl.BlockSpec(wu.shape, const3, pipeline_mode=once),
                  pl.BlockSpec(wd.shape, const3, pipeline_mode=once), vec, vec],
        out_specs=pl.BlockSpec((tm, D), row),
        out_shape=jax.ShapeDtypeStruct((T, D), F32),
        scratch_shapes=[pltpu.VMEM((tm, D), F32)],
        compiler_params=_cparams("parallel"),
        name="out_proj_router_moe",
    )(att, ssm, x2d, wa, wb, g1, b1, rw, rb, wg, wu, wd, g2, b2)


def _rope_tables(pos):
    half = HEAD_DIM // 2
    inv = 1.0 / (ROPE_THETA ** (jnp.arange(half, dtype=F32) / half))
    ang = pos.astype(F32)[:, None] * inv[None, :]
    cos = jnp.cos(ang)
    sin = jnp.sin(ang)
    cos_t = jnp.tile(cos, (1, LANES // half))
    sin_t = jnp.tile(jnp.concatenate([-sin, sin], axis=1), (1, LANES // HEAD_DIM))
    return cos_t, sin_t


def _row_tile(n, pref):
    t = min(n, pref)
    while n % t:
        t //= 2
    return t


def kernel(x_prompt, x_sample, cache_k, cache_v, state_ssm_re, state_ssm_im, w_in, lam_q1, lam_k1, lam_q2, lam_k2, subln_g, ssm_a_re, ssm_a_im, ssm_log_dt, ssm_b_re, ssm_b_im, ssm_c_re, ssm_c_im, ssm_d, w_glu, ssm_norm_g, w_out, ln1_g, ln1_b, w_grp, b_grp, w_rt, b_rt, w_gate, w_up, w_down, ln2_g, ln2_b):
    depth = w_in.shape[0]
    assert depth == 1, "single-layer step"
    B, L, D = x_prompt.shape
    Bs, S, _ = x_sample.shape
    P = cache_k.shape[2]
    H = cache_k.shape[3]
    n_qk = H * 2 * HEAD_DIM
    G, N = ssm_a_re.shape[1], ssm_a_re.shape[2]
    alpha = (2.0 * depth) ** 0.25
    lam_init = 0.8 - 0.6 * math.exp(-0.3 * 0)
    out_scale = 1.0 - lam_init
    l = 0

    w_in_bf = w_in[l].astype(BF16)
    lam = (jnp.exp(jnp.sum(lam_q1[l].astype(F32) * lam_k1[l].astype(F32)))
           - jnp.exp(jnp.sum(lam_q2[l].astype(F32) * lam_k2[l].astype(F32))) + lam_init).reshape(1).astype(F32)
    g_sub = subln_g[l].astype(F32).reshape(1, V_DIM)
    w_glu_bf = w_glu[l].astype(BF16)
    ssm_w = D - H * V_DIM
    d_row = ssm_d[l].astype(F32).reshape(1, ssm_w)
    ng_row = ssm_norm_g[l].astype(F32).reshape(1, ssm_w)
    wa = w_out[l, :H * V_DIM].astype(BF16)
    wb = w_out[l, H * V_DIM:].astype(BF16)
    ln1g, ln1b = ln1_g[l].astype(F32).reshape(1, D), ln1_b[l].astype(F32).reshape(1, D)
    ln2g, ln2b = ln2_g[l].astype(F32).reshape(1, D), ln2_b[l].astype(F32).reshape(1, D)
    w_r = jnp.concatenate([w_rt[l].astype(F32), w_grp[l].astype(F32),
                           jnp.zeros((D, LANES - N_EXPERTS - N_EXPERT_GROUPS), F32)], axis=1)
    rhi = w_r.astype(BF16)
    rw = jnp.concatenate([rhi, (w_r - rhi.astype(F32)).astype(BF16)], axis=1)
    rb =jnp.concatenate([b_rt[l].astype(F32), b_grp[l].astype(F32),
                          jnp.zeros((LANES - N_EXPERTS - N_EXPERT_GROUPS,), F32)]).reshape(1, LANES)
    wg, wu, wd = w_gate[l].astype(BF16), w_up[l].astype(BF16), w_down[l].astype(BF16)

    def tail(att, ssm, x2d):
        return _tail(att, ssm, x2d, wa, wb, ln1g, ln1b, rw, rb, wg, wu, wd, ln2g, ln2b,
                     tm=_row_tile(x2d.shape[0], 512), alpha=alpha)

    cos_p, sin_p = _rope_tables(jnp.arange(L))
    xp2 = x_prompt.reshape(B * L, D)
    qp, kpf, kpb, vpf, vpb, up = _in_proj(xp2, w_in_bf, cos_p, sin_p, tm=_row_tile(L, 512),
                                          t_attn=_row_tile(L, 512))
    r3 = lambda a: a.reshape(B, L, -1)
    att_p = _attn_prompt(lam, qp, r3(kpb), vpb, g_sub, out_scale=out_scale)
    n_steps_p = _row_tile(L, 256) // SUBLANES
    tabs_p = _s5_tables(ssm_a_re[l], ssm_a_im[l], ssm_log_dt[l], ssm_b_re[l].astype(F32), ssm_b_im[l].astype(F32),
                        ssm_c_re[l], ssm_c_im[l], n_steps_p)
    ssm_p, hre_p, him_p = _s5_mixer(r3(up), tabs_p, d_row, w_glu_bf, ng_row, n_steps=n_steps_p)
    y_prompt = tail(att_p.reshape(B * L, -1), ssm_p.reshape(B * L, -1), xp2).reshape(B, L, D)

    cos_s, sin_s = _rope_tables(P + jnp.arange(S))
    cos_s, sin_s = jnp.tile(cos_s, (Bs, 1)), jnp.tile(sin_s, (Bs, 1))
    xs2 = x_sample.reshape(Bs * S, D)
    qs, ksf, ksb, vsf, vsb, us = _in_proj(xs2, w_in_bf, cos_s, sin_s, tm=_row_tile(Bs * S, 512))
    s3 = lambda a: a.reshape(Bs, S, -1)
    cache_kt = jnp.transpose(cache_k[l], (0, 2, 3, 4, 1)).reshape(Bs * n_qk, P)
    att_s = _attn_sample(lam, s3(qs), cache_kt, cache_v[l].reshape(Bs * P * H, V_DIM),
                         s3(ksb), s3(vsb), g_sub, out_scale=out_scale)
    tabs_s = _s5_tables(ssm_a_re[l], ssm_a_im[l], ssm_log_dt[l], ssm_b_re[l].astype(F32), ssm_b_im[l].astype(F32),
                        ssm_c_re[l], ssm_c_im[l], S)
    h0 = (state_ssm_re[l].astype(F32).reshape(Bs, G * N), state_ssm_im[l].astype(F32).reshape(Bs, G * N))
    ssm_s, hre_s, him_s = _s5_mixer(s3(us), tabs_s, d_row, w_glu_bf, ng_row, n_steps=S, h0=h0)
    y_sample = tail(att_s.reshape(Bs * S, -1), ssm_s.reshape(Bs * S, -1), xs2).reshape(Bs, S, D)

    return (y_prompt, y_sample,
            jnp.transpose(kpf.reshape(1, B, H, 2, HEAD_DIM, L), (0, 1, 5, 2, 3, 4)), vpf.reshape(1, B, L, H, V_DIM),
            hre_p.reshape(1, B, G, N), him_p.reshape(1, B, G, N),
            ksf.reshape(1, Bs, S, H, 2, HEAD_DIM), vsf.reshape(1, Bs, S, H, V_DIM),
            hre_s.reshape(1, Bs, G, N), him_s.reshape(1, Bs, G, N))
```

```python
import functools
import math

import jax
import jax.numpy as jnp
from jax import lax
from jax.experimental import pallas as pl
from jax.experimental.pallas import tpu as pltpu

F32 = jnp.float32
BF16 = jnp.bfloat16

HEAD_DIM = 64
V_DIM = 2 * HEAD_DIM
CHUNK = 64
SSM_GROUP = 16
SSM_STATE = 64
N_EXPERT_GROUPS = 4
EXPERTS_PER_GROUP = 4
N_EXPERTS = N_EXPERT_GROUPS * EXPERTS_PER_GROUP
ROPE_THETA = 10000.0
LN_EPS = 1e-5
RMS_EPS = 1e-6
LANES = 128
SUBLANES = 8
MXU_N = 256
ATTN_UNROLL = 8
ATTN_TAIL = 4
NEG_BIG = -1e30
SUM_ROWS = 16
LOG2E = 1.4426950408889634
VMEM_LIMIT = 52 * 1024 * 1024


def _cparams(*sem):
    return pltpu.CompilerParams(dimension_semantics=sem, vmem_limit_bytes=VMEM_LIMIT)


def _store_transposed_tiles(ref, a):
    n_t, n_h, _, t = ref.shape
    at = a.T
    for n in range(n_t):
        for h in range(n_h):
            ref[n, h] = at[V_DIM * h:V_DIM * (h + 1), t * n:t * (n + 1)].astype(ref.dtype)


def _inproj_kernel(x_ref, w_ref, cos_ref, sin_ref, q_ref, kf_ref, kb_ref, vf_ref, vb_ref, u_ref,
                   *, q_scale, n_qk, transposed):
    xb = x_ref[...].astype(BF16)
    cos = cos_ref[...]
    sin = sin_ref[...]
    lane = lax.broadcasted_iota(jnp.int32, cos.shape, 1)
    first = (lane & (HEAD_DIM - 1)) < HEAD_DIM // 2

    def rope(t):
        rot = jnp.where(first, pltpu.roll(t, LANES - HEAD_DIM // 2, 1), pltpu.roll(t, HEAD_DIM // 2, 1))
        return t * cos + rot * sin

    pq = jnp.dot(xb, w_ref[:, 0:n_qk], preferred_element_type=F32)
    qr = jnp.concatenate([rope(pq[:, LANES * j:LANES * (j + 1)]) for j in range(n_qk // LANES)], axis=1) * q_scale
    if transposed:
        _store_transposed_tiles(q_ref, qr)
    else:
        q_ref[...] = qr.astype(BF16)
    pk = jnp.dot(xb, w_ref[:, n_qk:2 * n_qk], preferred_element_type=F32)
    kr = jnp.concatenate([rope(pk[:, LANES * j:LANES * (j + 1)]) for j in range(n_qk // LANES)], axis=1)
    kb_ref[...] = kr.astype(BF16)
    n_v = u_ref.shape[1]
    pv = jnp.dot(xb, w_ref[:, 2 * n_qk:2 * n_qk + n_v], preferred_element_type=F32)
    if transposed:
        kf_ref[...] = kr.T
        n_h = n_v // V_DIM
        for h in range(n_h):
            vf_ref[pl.ds(h, pv.shape[0], stride=n_h), :] = pv[:, V_DIM * h:V_DIM * (h + 1)]
        _store_transposed_tiles(vb_ref, pv)
    else:
        kf_ref[...] = kr
        vf_ref[...] = pv
        vb_ref[...] = pv.astype(BF16)
    u_ref[...] = jnp.dot(xb, w_ref[:, 2 * n_qk + n_v:], preferred_element_type=F32)


def _in_proj(x2d, w_bf, cos_t, sin_t, *, tm, t_attn=None):
    T, D = x2d.shape
    n_cols = w_bf.shape[1]
    n_qk = n_cols // 4
    n_tab = cos_t.shape[0] // tm
    row = lambda i: (i, 0)
    tab = lambda i: (i % n_tab, 0)
    out_sds = lambda dt: jax.ShapeDtypeStruct((T, n_qk), dt)
    spec = pl.BlockSpec((tm, n_qk), row)
    qv_spec, qv_sds = spec, out_sds(BF16)
    kf_spec, kf_sds, vf_spec, vf_sds = spec, out_sds(F32), spec, out_sds(F32)
    if t_attn is not None:
        H = n_qk // V_DIM
        qv_spec = pl.BlockSpec((tm // t_attn, H, V_DIM, t_attn), lambda i: (i, 0, 0, 0))
        qv_sds = jax.ShapeDtypeStruct((T // t_attn, H, V_DIM, t_attn), BF16)
        kf_spec = pl.BlockSpec((None, n_qk, tm), lambda i: (i // n_tab, 0, i % n_tab))
        kf_sds = jax.ShapeDtypeStruct((T // cos_t.shape[0], n_qk, cos_t.shape[0]), F32)
        vf_spec = pl.BlockSpec((tm * H, V_DIM), row)
        vf_sds = jax.ShapeDtypeStruct((T * H, V_DIM), F32)
    return pl.pallas_call(
        functools.partial(_inproj_kernel, q_scale=HEAD_DIM ** -0.5 * LOG2E, n_qk=n_qk,
                          transposed=t_attn is not None),
        grid=(T // tm,),
        in_specs=[pl.BlockSpec((tm, D), row),
                  pl.BlockSpec((D, n_cols), lambda i: (0, 0)),
                  pl.BlockSpec((tm, LANES), tab),
                  pl.BlockSpec((tm, LANES), tab)],
        out_specs=[qv_spec, kf_spec, spec, vf_spec, qv_spec, spec],
        out_shape=[qv_sds, kf_sds, out_sds(BF16), vf_sds, qv_sds, out_sds(F32)],
        compiler_params=_cparams("parallel"),
        name="in_proj_rope",
    )(x2d, w_bf, cos_t, sin_t)


def _split_maps(q):
    lane = lax.broadcasted_iota(jnp.int32, q.shape, 1)
    zero = jnp.zeros_like(q)
    return jnp.concatenate([jnp.where(lane < HEAD_DIM, q, zero), jnp.where(lane >= HEAD_DIM, q, zero)], axis=0)


def _qk(qs, kb):
    return lax.dot_general(qs, kb, (((1,), (1,)), ((), ())), preferred_element_type=F32)


def _subln(o, g, out_scale):
    ms = jnp.mean(o * o, axis=-1, keepdims=True)
    return o * lax.rsqrt(ms + RMS_EPS) * g * out_scale


def _attn_prompt_kernel(lam_ref, qt_ref, k_ref, vt_ref, g_ref, o_ref, qs_ref, acc_ref, s0_ref, s1_ref, m_ref,
                        *, tq, out_scale):
    i = pl.program_id(2)
    s_buf = (s0_ref, s1_ref)
    n_cb = 2 * tq // MXU_N
    qt = qt_ref[...]
    row = lax.broadcasted_iota(jnp.int32, qt.shape, 0)
    zero = jnp.zeros_like(qt)
    q_maps = (jnp.where(row < HEAD_DIM, qt, zero), jnp.where(row >= HEAD_DIM, qt, zero))
    for cb in range(n_cb):
        j = cb % (n_cb // 2)
        qs_ref[cb] = q_maps[cb // (n_cb // 2)][:, MXU_N * j:MXU_N * (j + 1)]
    acc_ref[...] = jnp.zeros(acc_ref.shape, F32)
    m_ref[...] = jnp.full(m_ref.shape, NEG_BIG, F32)
    ones_rows = jnp.ones((SUM_ROWS, tq), BF16)

    def scores(kt, slot):
        kb = k_ref[pl.ds(pl.multiple_of(kt * tq, tq), tq), :]
        for cb in range(n_cb):
            s_buf[slot][cb] = jnp.dot(kb, qs_ref[cb], preferred_element_type=F32)

    def softmax_pv(kt, slot, masked):
        lhs = jnp.concatenate([vt_ref[kt], ones_rows], axis=0)
        for cb in range(n_cb):
            st = s_buf[slot][cb]
            if masked:
                c = lax.broadcasted_iota(jnp.int32, st.shape, 0)
                r = (lax.broadcasted_iota(jnp.int32, st.shape, 1) + MXU_N * cb) % tq
                st = jnp.where((c // CHUNK) <= (r // CHUNK), st, NEG_BIG)
            m_prev = m_ref[cb]
            m_new = jnp.maximum(m_prev, jnp.max(st, axis=0, keepdims=True))
            alpha = jnp.exp2(m_prev - m_new)
            pt = jnp.exp2(st - m_new).astype(BF16)
            m_ref[cb] = m_new
            acc_ref[cb] = alpha * acc_ref[cb] + jnp.dot(lhs, pt, preferred_element_type=F32)

    def step(t, par):
        scores(t, par)
        softmax_pv(t - 1, 1 - par, False)

    scores(0, 0)

    def steps(t0, n):
        for d in range(1, n + 1):
            step(t0 + d, d % 2)

    def body(j, carry):
        steps(ATTN_UNROLL * j, ATTN_UNROLL)
        return carry

    lax.fori_loop(0, i // ATTN_UNROLL, body, 0)
    t_done = (i // ATTN_UNROLL) * ATTN_UNROLL
    rem_len = ATTN_UNROLL
    while rem_len > ATTN_TAIL:
        rem_len //= 2

        @pl.when((i - t_done) >= rem_len)
        def _(t_done=t_done, rem_len=rem_len):
            steps(t_done, rem_len)

        t_done = t_done + jnp.where((i - t_done) >= rem_len, rem_len, 0)
    for rem in range(ATTN_TAIL):
        @pl.when(i - t_done == rem)
        def _(rem=rem, t_done=t_done):
            steps(t_done, rem)
            softmax_pv(i, rem % 2, True)

    def normalized(cb):
        return acc_ref[cb, 0:V_DIM, :] * (1.0 / acc_ref[cb, V_DIM:V_DIM + 1, :])

    half = n_cb // 2
    ot = jnp.concatenate([normalized(j) - lam_ref[0] * normalized(j + half) for j in range(half)], axis=1)
    o_ref[...] = _subln(ot.T, g_ref[...], out_scale).astype(o_ref.dtype)


def _attn_prompt(lam, qt, k, vt, g, *, out_scale):
    B, L, W = k.shape
    n_t, H, _, t = qt.shape
    nq = L // t
    return pl.pallas_call(
        functools.partial(_attn_prompt_kernel, tq=t, out_scale=out_scale),
        grid=(B, H, nq),
        in_specs=[pl.BlockSpec(memory_space=pltpu.SMEM),
                  pl.BlockSpec((None, None, V_DIM, t), lambda b, h, i: (b * nq + i, h, 0, 0)),
                  pl.BlockSpec((None, L, V_DIM), lambda b, h, i: (b, 0, h)),
                  pl.BlockSpec((nq, None, V_DIM, t), lambda b, h, i: (b, h, 0, 0)),
                  pl.BlockSpec((1, V_DIM), lambda b, h, i: (0, 0))],
        out_specs=pl.BlockSpec((None, t, V_DIM), lambda b, h, i: (b, i, h)),
        out_shape=jax.ShapeDtypeStruct((B, L, W), BF16),
        scratch_shapes=[pltpu.VMEM((2 * t // MXU_N, V_DIM, MXU_N), BF16),
                        pltpu.VMEM((2 * t // MXU_N, V_DIM + SUM_ROWS, MXU_N), F32),
                        pltpu.VMEM((2 * t // MXU_N, t, MXU_N), F32), pltpu.VMEM((2 * t // MXU_N, t, MXU_N), F32),
                        pltpu.VMEM((2 * t // MXU_N, 1, MXU_N), F32)],
        compiler_params=_cparams("parallel", "parallel", "parallel"),
        name="diff_attn_prompt",
    )(lam, qt, k, vt, g)


def _attn_sample_kernel(lam_ref, q_ref, kct_ref, vc_ref, kn_ref, vn_ref, g_ref, o_ref, *, past_len, n_heads,
                        out_scale):
    for h in range(n_heads):
        hs = slice(V_DIM * h, V_DIM * (h + 1))
        vc = vc_ref[pl.ds(h, past_len, stride=n_heads), :]
        o_ref[:, hs] = _attn_sample_head(lam_ref[0], q_ref[:, hs], kct_ref[hs, :], vc, kn_ref[:, hs], vn_ref[:, hs],
                                         g_ref[...], past_len, out_scale).astype(o_ref.dtype)


def _attn_sample_head(lam, q, kct, vc, kn, vn, g, past_len, out_scale):
    tq = q.shape[0]
    qs = _split_maps(q)
    s_c = jnp.dot(qs, kct.astype(BF16), preferred_element_type=F32)
    s_n = _qk(qs, kn)
    r = lax.broadcasted_iota(jnp.int32, s_n.shape, 0)
    c = lax.broadcasted_iota(jnp.int32, s_n.shape, 1)
    r = jnp.where(r >= tq, r - tq, r)
    s_n = jnp.where(((past_len + c) // CHUNK) <= ((past_len + r) // CHUNK), s_n, NEG_BIG)
    m = jnp.maximum(jnp.max(s_c, axis=-1, keepdims=True), jnp.max(s_n, axis=-1, keepdims=True))
    p_c = jnp.exp2(s_c - m)
    p_n = jnp.exp2(s_n - m)
    inv_l = 1.0 / (jnp.sum(p_c, axis=-1, keepdims=True) + jnp.sum(p_n, axis=-1, keepdims=True))
    p_c = p_c * inv_l
    p_n = p_n * inv_l
    w_c = p_c[0:tq] - lam * p_c[tq:2 * tq]
    w_n = p_n[0:tq] - lam * p_n[tq:2 * tq]
    o = (jnp.dot(w_c.astype(BF16), vc.astype(BF16), preferred_element_type=F32)
         + jnp.dot(w_n.astype(BF16), vn, preferred_element_type=F32))
    return _subln(o, g, out_scale)


def _attn_sample(lam, q, cache_kt, cache_v, k_new, v_new, g, *, out_scale):
    B, S, W = q.shape
    P = cache_kt.shape[1]
    H = W // V_DIM
    blk_new = pl.BlockSpec((None, S, W), lambda b: (b, 0, 0))
    return pl.pallas_call(
        functools.partial(_attn_sample_kernel, past_len=P, n_heads=H, out_scale=out_scale),
        grid=(B,),
        in_specs=[pl.BlockSpec(memory_space=pltpu.SMEM), blk_new,
                  pl.BlockSpec((W, P), lambda b: (b, 0)),
                  pl.BlockSpec((P * H, V_DIM), lambda b: (b, 0)),
                  blk_new, blk_new, pl.BlockSpec((1, V_DIM), lambda b: (0, 0))],
        out_specs=blk_new,
        out_shape=jax.ShapeDtypeStruct((B, S, W), BF16),
        compiler_params=_cparams("parallel"),
        name="diff_attn_sample",
    )(lam, q, cache_kt, cache_v, k_new, v_new, g)


def _cmul(ar, ai, br, bi):
    return ar * br - ai * bi, ar * bi + ai * br


def _s5_kernel(*refs, n_steps, chained, lane_chunk):
    if chained:
        (u_ref, perm_ref, permt_ref, wx_ref, cb_ref, d_ref, pre_ref, pim_ref, gre_ref, gim_ref,
         wglu_ref, ng_ref, y_ref, hre_ref, him_ref, xre, xim, cre, cim) = refs
    else:
        (u_ref, perm_ref, permt_ref, wx_ref, cb_ref, d_ref, pre_ref, pim_ref, gre_ref, gim_ref,
         wglu_ref, ng_ref, h0re_ref, h0im_ref, y_ref, hre_ref, him_ref, xre, xim) = refs
    R = n_steps
    n_state = xre.shape[1]
    n_blk = wx_ref.shape[0]
    blk_in = wx_ref.shape[1]
    blk_st = wx_ref.shape[2] // 2

    u = u_ref[...]
    u_hi = u.astype(BF16)
    u_lo = (u - u_hi.astype(F32)).astype(BF16)
    perm = perm_ref[...]
    up = (jnp.dot(perm, u_hi, preferred_element_type=F32) + jnp.dot(perm, u_lo, preferred_element_type=F32))
    upb = up.astype(BF16)
    for j in range(n_blk):
        xj = jnp.dot(upb[:, blk_in * j:blk_in * (j + 1)], wx_ref[j], preferred_element_type=F32)
        xre[:, blk_st * j:blk_st * (j + 1)] = xj[:, :blk_st]
        xim[:, blk_st * j:blk_st * (j + 1)] = xj[:, blk_st:]

    n_chunks = n_state // lane_chunk
    e_re, e_im = [], []
    for c in range(n_chunks):
        cs = slice(lane_chunk * c, lane_chunk * (c + 1))
        ar = jnp.broadcast_to(pre_ref[0:1, cs], (SUBLANES, lane_chunk))
        ai = jnp.broadcast_to(pim_ref[0:1, cs], (SUBLANES, lane_chunk))

        def scan_body(r, h, cs=cs, ar=ar, ai=ai):
            hr, hi = h
            rows = pl.ds(pl.multiple_of(r * SUBLANES, SUBLANES), SUBLANES)
            nr = ar * hr - ai * hi + xre[rows, cs]
            ni = ar * hi + ai * hr + xim[rows, cs]
            xre[rows, cs] = nr
            xim[rows, cs] = ni
            return nr, ni

        z = jnp.zeros((SUBLANES, lane_chunk), F32)
        er, ei = lax.fori_loop(0, R, scan_body, (z, z), unroll=4)
        e_re.append(er)
        e_im.append(ei)
    e_re = jnp.concatenate(e_re, axis=1)
    e_im = jnp.concatenate(e_im, axis=1)

    g1r = jnp.broadcast_to(gre_ref[0:1, :], e_re.shape)
    g1i = jnp.broadcast_to(gim_ref[0:1, :], e_re.shape)
    if chained:
        t = pl.program_id(1)

        @pl.when(t == 0)
        def _():
            cre[...] = jnp.zeros(cre.shape, F32)
            cim[...] = jnp.zeros(cim.shape, F32)

        sub = lax.broadcasted_iota(jnp.int32, e_re.shape, 0)
        c_r = cre[...]
        c_i = cim[...]
        inj_r, inj_i = _cmul(g1r, g1i, c_r, c_i)
        xr = e_re + jnp.where(sub == 0, inj_r, 0.0)
        xi = e_im + jnp.where(sub == 0, inj_i, 0.0)
        for lvl, k in enumerate((1, 2, 4)):
            gr = jnp.broadcast_to(gre_ref[lvl:lvl + 1, :], e_re.shape)
            gi = jnp.broadcast_to(gim_ref[lvl:lvl + 1, :], e_re.shape)
            sr = jnp.where(sub >= k, pltpu.roll(xr, k, 0), 0.0)
            si = jnp.where(sub >= k, pltpu.roll(xi, k, 0), 0.0)
            tr, ti = _cmul(gr, gi, sr, si)
            xr = xr + tr
            xi = xi + ti
        hend_r, hend_i = xr, xi
        hc_r = jnp.where(sub == 0, c_r, pltpu.roll(hend_r, 1, 0))
        hc_i = jnp.where(sub == 0, c_i, pltpu.roll(hend_i, 1, 0))
        last_r = jnp.broadcast_to(hend_r[SUBLANES - 1:SUBLANES, :], e_re.shape)
        last_i = jnp.broadcast_to(hend_i[SUBLANES - 1:SUBLANES, :], e_re.shape)
        cre[...] = last_r
        cim[...] = last_i
        hre_ref[...] = last_r[0:1, :]
        him_ref[...] = last_i[0:1, :]
    else:
        hc_r = h0re_ref[...]
        hc_i = h0im_ref[...]
        tr, ti = _cmul(g1r, g1i, hc_r, hc_i)
        hre_ref[...] = e_re + tr
        him_ref[...] = e_im + ti

    for c in range(n_chunks):
        cs = slice(lane_chunk * c, lane_chunk * (c + 1))
        hcr = hc_r[:, cs]
        hci = hc_i[:, cs]

        def fix_body(r, carry, cs=cs, hcr=hcr, hci=hci):
            rows = pl.ds(pl.multiple_of(r * SUBLANES, SUBLANES), SUBLANES)
            pr = jnp.broadcast_to(pre_ref[pl.ds(r, 1), cs], hcr.shape)
            pi = jnp.broadcast_to(pim_ref[pl.ds(r, 1), cs], hcr.shape)
            tr, ti = _cmul(pr, pi, hcr, hci)
            xre[rows, cs] = xre[rows, cs] + tr
            xim[rows, cs] = xim[rows, cs] + ti
            return carry

        lax.fori_loop(0, R, fix_body, 0, unroll=4)

    ys = []
    for j in range(n_blk):
        st = slice(blk_st * j, blk_st * (j + 1))
        ys.append(jnp.dot(xre[:, st].astype(BF16), cb_ref[j, 0:blk_st, :], preferred_element_type=F32)
                  + jnp.dot(xim[:, st].astype(BF16), cb_ref[j, blk_st:, :], preferred_element_type=F32))
    y = jnp.concatenate(ys, axis=1) + d_ref[...] * up
    y = 0.5 * y * (1.0 + jnp.tanh(math.sqrt(2.0 / math.pi) * (y + 0.044715 * (y * y * y))))
    gl = jnp.dot(y.astype(BF16), wglu_ref[...], preferred_element_type=F32)
    half = gl.shape[1] // 2
    z = gl[:, :half] * (1.0 / (1.0 + jnp.exp(-gl[:, half:])))
    z = z * lax.rsqrt(jnp.mean(z * z, axis=-1, keepdims=True) + RMS_EPS) * ng_ref[...]
    y_ref[...] = jnp.dot(permt_ref[...], z.astype(BF16), preferred_element_type=F32).astype(y_ref.dtype)


def _s5_tables(a_re, a_im, log_dt, b_re, b_im, c_re, c_im, n_steps):
    G, N = a_re.shape
    C = b_re.shape[2]
    dt = jnp.exp(log_dt.astype(F32))[:, None]
    a_re = a_re.astype(F32)
    a_im = a_im.astype(F32)

    def power(k):
        mag = jnp.exp(k * a_re * dt)
        return mag * jnp.cos(k * a_im * dt), mag * jnp.sin(k * a_im * dt)

    ab_re, ab_im = power(1.0)
    nr, ni = ab_re - 1.0, ab_im
    den = a_re * a_re + a_im * a_im
    cf_re = (nr * a_re + ni * a_im) / den
    cf_im = (ni * a_re - nr * a_im) / den
    ks = jnp.arange(1, n_steps + 1, dtype=F32)[:, None, None]
    p_re, p_im = power(ks)
    g_re, g_im = zip(*[power(float(m * n_steps)) for m in (1, 2, 4)])
    pad = jnp.zeros((SUBLANES - 3, G * N), F32)
    g_re = jnp.concatenate([jnp.stack(g_re).reshape(3, G * N), pad])
    g_im = jnp.concatenate([jnp.stack(g_im).reshape(3, G * N), pad])
    wx_re = cf_re[..., None] * b_re - cf_im[..., None] * b_im
    wx_im = cf_re[..., None] * b_im + cf_im[..., None] * b_re
    gpb = LANES // C
    nb = G // gpb
    eye = jnp.eye(gpb, dtype=F32)

    def in_block(w):
        w = w.reshape(nb, gpb, N, C)
        return jnp.einsum('jgnc,gh->jgchn', w, eye).reshape(nb, gpb * C, gpb * N)

    def out_block(w):
        w = w.reshape(nb, gpb, C, N)
        return jnp.einsum('jgcn,gh->jgnhc', w, eye).reshape(nb, gpb * N, gpb * C)

    wx = jnp.concatenate([in_block(wx_re), in_block(wx_im)], axis=2).astype(BF16)
    cb = jnp.concatenate([out_block(c_re.astype(F32)), out_block(-c_im.astype(F32))], axis=1).astype(BF16)
    return (p_re.reshape(n_steps, G * N), p_im.reshape(n_steps, G * N), g_re, g_im, wx, cb)


def _perm_matrices(n_steps):
    tt = SUBLANES * n_steps
    dst = jnp.arange(tt)
    src = (dst % SUBLANES) * n_steps + dst // SUBLANES
    perm = (src[:, None] == jnp.arange(tt)[None, :]).astype(BF16)
    return perm, perm.T


def _s5_mixer(u, tables, d, w_glu_bf, norm_g, *, n_steps, h0=None):
    B, L, W = u.shape
    p_re, p_im, g_re, g_im, wx, cb = tables
    n_state = p_re.shape[1]
    tt = SUBLANES * n_steps
    perm, permt = _perm_matrices(n_steps)
    chained = h0 is None
    const2 = lambda *_: (0, 0)
    const3 = lambda *_: (0, 0, 0)
    common_specs = [pl.BlockSpec(perm.shape, const2), pl.BlockSpec(perm.shape, const2),
                    pl.BlockSpec(wx.shape, const3), pl.BlockSpec(cb.shape, const3),
                    pl.BlockSpec((1, W), const2),
                    pl.BlockSpec(p_re.shape, const2), pl.BlockSpec(p_im.shape, const2),
                    pl.BlockSpec(g_re.shape, const2), pl.BlockSpec(g_im.shape, const2),
                    pl.BlockSpec(w_glu_bf.shape, const2), pl.BlockSpec((1, W), const2)]
    common_args = (perm, permt, wx, cb, d, p_re, p_im, g_re, g_im, w_glu_bf, norm_g)
    state_scratch = [pltpu.VMEM((tt, n_state), F32), pltpu.VMEM((tt, n_state), F32)]
    kern = functools.partial(_s5_kernel, n_steps=n_steps, chained=chained, lane_chunk=512)
    if chained:
        y, hre, him = pl.pallas_call(
            kern,
            grid=(B, L // tt),
            in_specs=[pl.BlockSpec((None, tt, W), lambda b, t: (b, t, 0))] + common_specs,
            out_specs=[pl.BlockSpec((None, tt, W), lambda b, t: (b, t, 0)),
                       pl.BlockSpec((None, 1, n_state), lambda b, t: (b, 0, 0)),
                       pl.BlockSpec((None, 1, n_state), lambda b, t: (b, 0, 0))],
            out_shape=[jax.ShapeDtypeStruct((B, L, W), BF16),
                       jax.ShapeDtypeStruct((B, 1, n_state), F32),
                       jax.ShapeDtypeStruct((B, 1, n_state), F32)],
            scratch_shapes=state_scratch + [pltpu.VMEM((SUBLANES, n_state), F32)] * 2,
            compiler_params=_cparams("parallel", "arbitrary"),
            name="s5_mixer_chained",
        )(u, *common_args)
        return y, hre.reshape(B, n_state), him.reshape(B, n_state)
    u2 = u.reshape(B * L, W)
    h0re, h0im = h0
    y, hre, him = pl.pallas_call(
        kern,
        grid=(B // SUBLANES,),
        in_specs=[pl.BlockSpec((tt, W), lambda i: (i, 0))] + common_specs
        + [pl.BlockSpec((SUBLANES, n_state), lambda i: (i, 0))] * 2,
        out_specs=[pl.BlockSpec((tt, W), lambda i: (i, 0)),
                   pl.BlockSpec((SUBLANES, n_state), lambda i: (i, 0)),
                   pl.BlockSpec((SUBLANES, n_state), lambda i: (i, 0))],
        out_shape=[jax.ShapeDtypeStruct((B * L, W), BF16),
                   jax.ShapeDtypeStruct((B, n_state), F32),
                   jax.ShapeDtypeStruct((B, n_state), F32)],
        scratch_shapes=state_scratch,
        compiler_params=_cparams("parallel"),
        name="s5_mixer_batched",
    )(u2, *common_args, h0re, h0im)
    return y.reshape(B, L, W), hre, him


def _layer_norm(z, g, b):
    mu = jnp.mean(z, axis=-1, keepdims=True)
    zc = z - mu
    var = jnp.mean(zc * zc, axis=-1, keepdims=True)
    return zc * lax.rsqrt(var + LN_EPS) * g + b


def _router_gates(x1, hi, rw_ref, rb_ref):
    lo = (x1 - hi.astype(F32)).astype(BF16)
    both = jnp.dot(hi, rw_ref[...], preferred_element_type=F32)
    lg = (both[:, :LANES] + both[:, LANES:]
          + jnp.dot(lo, rw_ref[:, :LANES], preferred_element_type=F32)) + rb_ref[...]
    lane = lax.broadcasted_iota(jnp.int32, lg.shape, 1)
    big = jnp.int32(LANES)
    is_grp = (lane >= N_EXPERTS) & (lane < N_EXPERTS + N_EXPERT_GROUPS)
    gl = jnp.where(is_grp, lg, NEG_BIG)
    gexp = jnp.where(is_grp, jnp.exp(gl - jnp.max(gl, axis=-1, keepdims=True)), 0.0)
    gprob = gexp / jnp.sum(gexp, axis=-1, keepdims=True)
    g_w = jnp.max(gprob, axis=-1, keepdims=True)
    g_idx = jnp.min(jnp.where(is_grp & (gprob == g_w), lane - N_EXPERTS, big), axis=-1, keepdims=True)
    valid = (lane < N_EXPERTS) & ((lane // EXPERTS_PER_GROUP) == g_idx)
    el = jnp.where(valid, lg, NEG_BIG)
    eexp = jnp.where(valid, jnp.exp(el - jnp.max(el, axis=-1, keepdims=True)), 0.0)
    eprob = eexp / jnp.sum(eexp, axis=-1, keepdims=True)
    w1 = jnp.max(jnp.where(valid, eprob, -1.0), axis=-1, keepdims=True)
    i1 = jnp.min(jnp.where(valid & (eprob == w1), lane, big), axis=-1, keepdims=True)
    rest = valid & (lane != i1)
    w2 = jnp.max(jnp.where(rest, eprob, -1.0), axis=-1, keepdims=True)
    i2 = jnp.min(jnp.where(rest & (eprob == w2), lane, big), axis=-1, keepdims=True)
    den = w1 + w2
    return (jnp.where(lane == i1, g_w * (w1 / den), 0.0)
            + jnp.where(lane == i2, g_w * (w2 / den), 0.0))


def _tail_kernel(att_ref, ssm_ref, x_ref, wa_ref, wb_ref, g1_ref, b1_ref, rw_ref, rb_ref,
                 wg_ref, wu_ref, wd_ref, g2_ref, b2_ref, o_ref, acc_ref, *, alpha):
    mix = (jnp.dot(att_ref[...], wa_ref[...], preferred_element_type=F32)
           + jnp.dot(ssm_ref[...], wb_ref[...], preferred_element_type=F32))
    x1 = _layer_norm(alpha * x_ref[...] + mix, g1_ref[...], b1_ref[...])
    xb = x1.astype(BF16)
    gates = _router_gates(x1, xb, rw_ref, rb_ref)
    for e in range(wg_ref.shape[0]):
        hg = jnp.dot(xb, wg_ref[e], preferred_element_type=F32)
        hu = jnp.dot(xb, wu_ref[e], preferred_element_type=F32)
        h = hg * (1.0 / (1.0 + jnp.exp(-hg))) * hu * gates[:, e:e + 1]
        contrib = jnp.dot(h.astype(BF16), wd_ref[e], preferred_element_type=F32)
        if e == 0:
            acc_ref[...] = contrib
        else:
            acc_ref[...] += contrib
    o_ref[...] = _layer_norm(alpha * x1 + acc_ref[...], g2_ref[...], b2_ref[...])


def _tail(att, ssm, x2d, wa, wb, g1, b1, rw, rb, wg, wu, wd, g2, b2, *, tm, alpha):
    T, D = x2d.shape
    Wh = att.shape[1]
    row = lambda i: (i, 0)
    const2 = lambda i: (0, 0)
    const3 = lambda i: (0, 0, 0)
    once = pl.Buffered(1)
    vec = pl.BlockSpec((1, D), const2)
    return pl.pallas_call(
        functools.partial(_tail_kernel, alpha=alpha),
        grid=(T // tm,),
        in_specs=[pl.BlockSpec((tm, Wh), row), pl.BlockSpec((tm, Wh), row), pl.BlockSpec((tm, D), row),
                  pl.BlockSpec(wa.shape, const2, pipeline_mode=once),
                  pl.BlockSpec(wb.shape, const2, pipeline_mode=once), vec, vec,
                  pl.BlockSpec(rw.shape, const2, pipeline_mode=once), pl.BlockSpec((1, LANES), const2),
                  pl.BlockSpec(wg.shape, const3, pipeline_mode=once),
                  pl.BlockSpec(wu.shape, const3, pipeline_mode=once),
                  pl.BlockSpec(wd.shape, const3, pipeline_mode=once), vec, vec],
        out_specs=pl.BlockSpec((tm, D), row),
        out_shape=jax.ShapeDtypeStruct((T, D), F32),
        scratch_shapes=[pltpu.VMEM((tm, D), F32)],
        compiler_params=_cparams("parallel"),
        name="out_proj_router_moe",
    )(att, ssm, x2d, wa, wb, g1, b1, rw, rb, wg, wu, wd, g2, b2)


def _rope_tables(pos):
    half = HEAD_DIM // 2
    inv = 1.0 / (ROPE_THETA ** (jnp.arange(half, dtype=F32) / half))
    ang = pos.astype(F32)[:, None] * inv[None, :]
    cos = jnp.cos(ang)
    sin = jnp.sin(ang)
    cos_t = jnp.tile(cos, (1, LANES // half))
    sin_t = jnp.tile(jnp.concatenate([-sin, sin], axis=1), (1, LANES // HEAD_DIM))
    return cos_t, sin_t


def _row_tile(n, pref):
    t = min(n, pref)
    while n % t:
        t //= 2
    return t


def kernel(x_prompt, x_sample, cache_k, cache_v, state_ssm_re, state_ssm_im, w_in, lam_q1, lam_k1, lam_q2, lam_k2, subln_g, ssm_a_re, ssm_a_im, ssm_log_dt, ssm_b_re, ssm_b_im, ssm_c_re, ssm_c_im, ssm_d, w_glu, ssm_norm_g, w_out, ln1_g, ln1_b, w_grp, b_grp, w_rt, b_rt, w_gate, w_up, w_down, ln2_g, ln2_b):
    depth = w_in.shape[0]
    assert depth == 1, "single-layer step"
    B, L, D = x_prompt.shape
    Bs, S, _ = x_sample.shape
    P = cache_k.shape[2]
    H = cache_k.shape[3]
    n_qk = H * 2 * HEAD_DIM
    G, N = ssm_a_re.shape[1], ssm_a_re.shape[2]
    alpha = (2.0 * depth) ** 0.25
    lam_init = 0.8 - 0.6 * math.exp(-0.3 * 0)
    out_scale = 1.0 - lam_init
    l = 0

    w_in_bf = w_in[l].astype(BF16)
    lam = (jnp.exp(jnp.sum(lam_q1[l].astype(F32) * lam_k1[l].astype(F32)))
           - jnp.exp(jnp.sum(lam_q2[l].astype(F32) * lam_k2[l].astype(F32))) + lam_init).reshape(1).astype(F32)
    g_sub = subln_g[l].astype(F32).reshape(1, V_DIM)
    w_glu_bf = w_glu[l].astype(BF16)
    ssm_w = D - H * V_DIM
    d_row = ssm_d[l].astype(F32).reshape(1, ssm_w)
    ng_row = ssm_norm_g[l].astype(F32).reshape(1, ssm_w)
    wa = w_out[l, :H * V_DIM].astype(BF16)
    wb = w_out[l, H * V_DIM:].astype(BF16)
    ln1g, ln1b = ln1_g[l].astype(F32).reshape(1, D), ln1_b[l].astype(F32).reshape(1, D)
    ln2g, ln2b = ln2_g[l].astype(F32).reshape(1, D), ln2_b[l].astype(F32).reshape(1, D)
    w_r = jnp.concatenate([w_rt[l].astype(F32), w_grp[l].astype(F32),
                           jnp.zeros((D, LANES - N_EXPERTS - N_EXPERT_GROUPS), F32)], axis=1)
    rhi = w_r.astype(BF16)
    rw = jnp.concatenate([rhi, (w_r - rhi.astype(F32)).astype(BF16)], axis=1)
    rb =jnp.concatenate([b_rt[l].astype(F32), b_grp[l].astype(F32),
                          jnp.zeros((LANES - N_EXPERTS - N_EXPERT_GROUPS,), F32)]).reshape(1, LANES)
    wg, wu, wd = w_gate[l].astype(BF16), w_up[l].astype(BF16), w_down[l].astype(BF16)

    def tail(att, ssm, x2d):
        return _tail(att, ssm, x2d, wa, wb, ln1g, ln1b, rw, rb, wg, wu, wd, ln2g, ln2b,
                     tm=_row_tile(x2d.shape[0], 512), alpha=alpha)

    cos_p, sin_p = _rope_tables(jnp.arange(L))
    xp2 = x_prompt.reshape(B * L, D)
    qp, kpf, kpb, vpf, vpb, up = _in_proj(xp2, w_in_bf, cos_p, sin_p, tm=_row_tile(L, 512),
                                          t_attn=_row_tile(L, 512))
    r3 = lambda a: a.reshape(B, L, -1)
    att_p = _attn_prompt(lam, qp, r3(kpb), vpb, g_sub, out_scale=out_scale)
    n_steps_p = _row_tile(L, 256) // SUBLANES
    tabs_p = _s5_tables(ssm_a_re[l], ssm_a_im[l], ssm_log_dt[l], ssm_b_re[l].astype(F32), ssm_b_im[l].astype(F32),
                        ssm_c_re[l], ssm_c_im[l], n_steps_p)
    ssm_p, hre_p, him_p = _s5_mixer(r3(up), tabs_p, d_row, w_glu_bf, ng_row, n_steps=n_steps_p)
    y_prompt = tail(att_p.reshape(B * L, -1), ssm_p.reshape(B * L, -1), xp2).reshape(B, L, D)

    cos_s, sin_s = _rope_tables(P + jnp.arange(S))
    cos_s, sin_s = jnp.tile(cos_s, (Bs, 1)), jnp.tile(sin_s, (Bs, 1))
    xs2 = x_sample.reshape(Bs * S, D)
    qs, ksf, ksb, vsf, vsb, us = _in_proj(xs2, w_in_bf, cos_s, sin_s, tm=_row_tile(Bs * S, 512))
    s3 = lambda a: a.reshape(Bs, S, -1)
    cache_kt = jnp.transpose(cache_k[l], (0, 2, 3, 4, 1)).reshape(Bs * n_qk, P)
    att_s = _attn_sample(lam, s3(qs), cache_kt, cache_v[l].reshape(Bs * P * H, V_DIM),
                         s3(ksb), s3(vsb), g_sub, out_scale=out_scale)
    tabs_s = _s5_tables(ssm_a_re[l], ssm_a_im[l], ssm_log_dt[l], ssm_b_re[l].astype(F32), ssm_b_im[l].astype(F32),
                        ssm_c_re[l], ssm_c_im[l], S)
    h0 = (state_ssm_re[l].astype(F32).reshape(Bs, G * N), state_ssm_im[l].astype(F32).reshape(Bs, G * N))
    ssm_s, hre_s, him_s = _s5_mixer(s3(us), tabs_s, d_row, w_glu_bf, ng_row, n_steps=S, h0=h0)
    y_sample = tail(att_s.reshape(Bs * S, -1), ssm_s.reshape(Bs * S, -1), xs2).reshape(Bs, S, D)

    return (y_prompt, y_sample,
            jnp.transpose(kpf.reshape(1, B, H, 2, HEAD_DIM, L), (0, 1, 5, 2, 3, 4)), vpf.reshape(1, B, L, H, V_DIM),
            hre_p.reshape(1, B, G, N), him_p.reshape(1, B, G, N),
            ksf.reshape(1, Bs, S, H, 2, HEAD_DIM), vsf.reshape(1, Bs, S, H, V_DIM),
            hre_s.reshape(1, Bs, G, N), him_s.reshape(1, Bs, G, N))
```

```python
import functools
import math

import jax
import jax.numpy as jnp
from jax import lax
from jax.experimental import pallas as pl
from jax.experimental.pallas import tpu as pltpu

F32 = jnp.float32
BF16 = jnp.bfloat16

HEAD_DIM = 64
V_DIM = 2 * HEAD_DIM
CHUNK = 64
SSM_GROUP = 16
SSM_STATE = 64
N_EXPERT_GROUPS = 4
EXPERTS_PER_GROUP = 4
N_EXPERTS = N_EXPERT_GROUPS * EXPERTS_PER_GROUP
ROPE_THETA = 10000.0
LN_EPS = 1e-5
RMS_EPS = 1e-6
LANES = 128
SUBLANES = 8
MXU_N = 256
ATTN_UNROLL = 8
ATTN_TAIL = 4
NEG_BIG = -1e30
SUM_ROWS = 16
TAIL_CHUNKS = 1
LOG2E = 1.4426950408889634
VMEM_LIMIT = 52 * 1024 * 1024


def _cparams(*sem):
    return pltpu.CompilerParams(dimension_semantics=sem, vmem_limit_bytes=VMEM_LIMIT)


def _store_transposed_tiles(ref, a):
    n_t, n_h, _, t = ref.shape
    at = a.T
    for n in range(n_t):
        for h in range(n_h):
            ref[n, h] = at[V_DIM * h:V_DIM * (h + 1), t * n:t * (n + 1)].astype(ref.dtype)


def _inproj_kernel(x_ref, w_ref, cos_ref, sin_ref, q_ref, kf_ref, kb_ref, vf_ref, vb_ref, u_ref,
                   *, q_scale, n_qk, transposed):
    xb = x_ref[...].astype(BF16)
    cos = cos_ref[...]
    sin = sin_ref[...]
    lane = lax.broadcasted_iota(jnp.int32, cos.shape, 1)
    first = (lane & (HEAD_DIM - 1)) < HEAD_DIM // 2

    def rope(t):
        rot = jnp.where(first, pltpu.roll(t, LANES - HEAD_DIM // 2, 1), pltpu.roll(t, HEAD_DIM // 2, 1))
        return t * cos + rot * sin

    pq = jnp.dot(xb, w_ref[:, 0:n_qk], preferred_element_type=F32)
    qr = jnp.concatenate([rope(pq[:, LANES * j:LANES * (j + 1)]) for j in range(n_qk // LANES)], axis=1) * q_scale
    if transposed:
        _store_transposed_tiles(q_ref, qr)
    else:
        q_ref[...] = qr.astype(BF16)
    pk = jnp.dot(xb, w_ref[:, n_qk:2 * n_qk], preferred_element_type=F32)
    kr = jnp.concatenate([rope(pk[:, LANES * j:LANES * (j + 1)]) for j in range(n_qk // LANES)], axis=1)
    kb_ref[...] = kr.astype(BF16)
    n_v = u_ref.shape[1]
    pv = jnp.dot(xb, w_ref[:, 2 * n_qk:2 * n_qk + n_v], preferred_element_type=F32)
    if transposed:
        kf_ref[...] = kr.T
        n_h = n_v // V_DIM
        for h in range(n_h):
            vf_ref[pl.ds(h, pv.shape[0], stride=n_h), :] = pv[:, V_DIM * h:V_DIM * (h + 1)]
        _store_transposed_tiles(vb_ref, pv)
    else:
        kf_ref[...] = kr
        vf_ref[...] = pv
        vb_ref[...] = pv.astype(BF16)
    u_ref[...] = jnp.dot(xb, w_ref[:, 2 * n_qk + n_v:], preferred_element_type=F32)


def _in_proj(x2d, w_bf, cos_t, sin_t, *, tm, t_attn=None):
    T, D = x2d.shape
    n_cols = w_bf.shape[1]
    n_qk = n_cols // 4
    n_tab = cos_t.shape[0] // tm
    row = lambda i: (i, 0)
    tab = lambda i: (i % n_tab, 0)
    out_sds = lambda dt: jax.ShapeDtypeStruct((T, n_qk), dt)
    spec = pl.BlockSpec((tm, n_qk), row)
    qv_spec, qv_sds = spec, out_sds(BF16)
    kf_spec, kf_sds, vf_spec, vf_sds = spec, out_sds(F32), spec, out_sds(F32)
    if t_attn is not None:
        H = n_qk // V_DIM
        qv_spec = pl.BlockSpec((tm // t_attn, H, V_DIM, t_attn), lambda i: (i, 0, 0, 0))
        qv_sds = jax.ShapeDtypeStruct((T // t_attn, H, V_DIM, t_attn), BF16)
        kf_spec = pl.BlockSpec((None, n_qk, tm), lambda i: (i // n_tab, 0, i % n_tab))
        kf_sds = jax.ShapeDtypeStruct((T // cos_t.shape[0], n_qk, cos_t.shape[0]), F32)
        vf_spec = pl.BlockSpec((tm * H, V_DIM), row)
        vf_sds = jax.ShapeDtypeStruct((T * H, V_DIM), F32)
    return pl.pallas_call(
        functools.partial(_inproj_kernel, q_scale=HEAD_DIM ** -0.5 * LOG2E, n_qk=n_qk,
                          transposed=t_attn is not None),
        grid=(T // tm,),
        in_specs=[pl.BlockSpec((tm, D), row),
                  pl.BlockSpec((D, n_cols), lambda i: (0, 0)),
                  pl.BlockSpec((tm, LANES), tab),
                  pl.BlockSpec((tm, LANES), tab)],
        out_specs=[qv_spec, kf_spec, spec, vf_spec, qv_spec, spec],
        out_shape=[qv_sds, kf_sds, out_sds(BF16), vf_sds, qv_sds, out_sds(F32)],
        compiler_params=_cparams("parallel"),
        name="in_proj_rope",
    )(x2d, w_bf, cos_t, sin_t)


def _split_maps(q):
    lane = lax.broadcasted_iota(jnp.int32, q.shape, 1)
    zero = jnp.zeros_like(q)
    return jnp.concatenate([jnp.where(lane < HEAD_DIM, q, zero), jnp.where(lane >= HEAD_DIM, q, zero)], axis=0)


def _qk(qs, kb):
    return lax.dot_general(qs, kb, (((1,), (1,)), ((), ())), preferred_element_type=F32)


def _subln(o, g, out_scale):
    ms = jnp.mean(o * o, axis=-1, keepdims=True)
    return o * lax.rsqrt(ms + RMS_EPS) * g * out_scale


def _attn_prompt_kernel(lam_ref, qt_ref, k_ref, vt_ref, g_ref, o_ref, qs_ref, acc_ref, s0_ref, s1_ref, m_ref,
                        *, tq, out_scale):
    i = pl.program_id(2)
    s_buf = (s0_ref, s1_ref)
    n_cb = 2 * tq // MXU_N
    qt = qt_ref[...]
    row = lax.broadcasted_iota(jnp.int32, qt.shape, 0)
    zero = jnp.zeros_like(qt)
    q_maps = (jnp.where(row < HEAD_DIM, qt, zero), jnp.where(row >= HEAD_DIM, qt, zero))
    for cb in range(n_cb):
        j = cb % (n_cb // 2)
        qs_ref[cb] = q_maps[cb // (n_cb // 2)][:, MXU_N * j:MXU_N * (j + 1)]
    acc_ref[...] = jnp.zeros(acc_ref.shape, F32)
    m_ref[...] = jnp.full(m_ref.shape, NEG_BIG, F32)
    ones_rows = jnp.ones((SUM_ROWS, tq), BF16)

    def scores(kt, slot):
        kb = k_ref[pl.ds(pl.multiple_of(kt * tq, tq), tq), :]
        for cb in range(n_cb):
            s_buf[slot][cb] = jnp.dot(kb, qs_ref[cb], preferred_element_type=F32)

    def softmax_pv(kt, slot, masked):
        lhs = jnp.concatenate([vt_ref[kt], ones_rows], axis=0)
        for cb in range(n_cb):
            st = s_buf[slot][cb]
            if masked:
                c = lax.broadcasted_iota(jnp.int32, st.shape, 0)
                r = (lax.broadcasted_iota(jnp.int32, st.shape, 1) + MXU_N * cb) % tq
                st = jnp.where((c // CHUNK) <= (r // CHUNK), st, NEG_BIG)
            m_prev = m_ref[cb]
            m_new = jnp.maximum(m_prev, jnp.max(st, axis=0, keepdims=True))
            alpha = jnp.exp2(m_prev - m_new)
            pt = jnp.exp2(st - m_new).astype(BF16)
            m_ref[cb] = m_new
            acc_ref[cb] = alpha * acc_ref[cb] + jnp.dot(lhs, pt, preferred_element_type=F32)

    def step(t, par):
        scores(t, par)
        softmax_pv(t - 1, 1 - par, False)

    scores(0, 0)

    def steps(t0, n):
        for d in range(1, n + 1):
            step(t0 + d, d % 2)

    def body(j, carry):
        steps(ATTN_UNROLL * j, ATTN_UNROLL)
        return carry

    lax.fori_loop(0, i // ATTN_UNROLL, body, 0)
    t_done = (i // ATTN_UNROLL) * ATTN_UNROLL
    rem_len = ATTN_UNROLL
    while rem_len > ATTN_TAIL:
        rem_len //= 2

        @pl.when((i - t_done) >= rem_len)
        def _(t_done=t_done, rem_len=rem_len):
            steps(t_done, rem_len)

        t_done = t_done + jnp.where((i - t_done) >= rem_len, rem_len, 0)
    for rem in range(ATTN_TAIL):
        @pl.when(i - t_done == rem)
        def _(rem=rem, t_done=t_done):
            steps(t_done, rem)
            softmax_pv(i, rem % 2, True)

    def normalized(cb):
        return acc_ref[cb, 0:V_DIM, :] * (1.0 / acc_ref[cb, V_DIM:V_DIM + 1, :])

    half = n_cb // 2
    ot = jnp.concatenate([normalized(j) - lam_ref[0] * normalized(j + half) for j in range(half)], axis=1)
    o_ref[...] = _subln(ot.T, g_ref[...], out_scale).astype(o_ref.dtype)


def _attn_prompt(lam, qt, k, vt, g, *, out_scale):
    B, L, W = k.shape
    n_t, H, _, t = qt.shape
    nq = L // t
    return pl.pallas_call(
        functools.partial(_attn_prompt_kernel, tq=t, out_scale=out_scale),
        grid=(B, H, nq),
        in_specs=[pl.BlockSpec(memory_space=pltpu.SMEM),
                  pl.BlockSpec((None, None, V_DIM, t), lambda b, h, i: (b * nq + i, h, 0, 0)),
                  pl.BlockSpec((None, L, V_DIM), lambda b, h, i: (b, 0, h)),
                  pl.BlockSpec((nq, None, V_DIM, t), lambda b, h, i: (b, h, 0, 0)),
                  pl.BlockSpec((1, V_DIM), lambda b, h, i: (0, 0))],
        out_specs=pl.BlockSpec((None, t, V_DIM), lambda b, h, i: (b, i, h)),
        out_shape=jax.ShapeDtypeStruct((B, L, W), BF16),
        scratch_shapes=[pltpu.VMEM((2 * t // MXU_N, V_DIM, MXU_N), BF16),
                        pltpu.VMEM((2 * t // MXU_N, V_DIM + SUM_ROWS, MXU_N), F32),
                        pltpu.VMEM((2 * t // MXU_N, t, MXU_N), F32), pltpu.VMEM((2 * t // MXU_N, t, MXU_N), F32),
                        pltpu.VMEM((2 * t // MXU_N, 1, MXU_N), F32)],
        compiler_params=_cparams("parallel", "parallel", "parallel"),
        name="diff_attn_prompt",
    )(lam, qt, k, vt, g)


def _attn_sample_kernel(lam_ref, q_ref, kct_ref, vc_ref, kn_ref, vn_ref, g_ref, o_ref, *, past_len, n_heads,
                        out_scale):
    for h in range(n_heads):
        hs = slice(V_DIM * h, V_DIM * (h + 1))
        vc = vc_ref[pl.ds(h, past_len, stride=n_heads), :]
        o_ref[:, hs] = _attn_sample_head(lam_ref[0], q_ref[:, hs], kct_ref[hs, :], vc, kn_ref[:, hs], vn_ref[:, hs],
                                         g_ref[...], past_len, out_scale).astype(o_ref.dtype)


def _attn_sample_head(lam, q, kct, vc, kn, vn, g, past_len, out_scale):
    tq = q.shape[0]
    qs = _split_maps(q)
    s_c = jnp.dot(qs, kct.astype(BF16), preferred_element_type=F32)
    s_n = _qk(qs, kn)
    r = lax.broadcasted_iota(jnp.int32, s_n.shape, 0)
    c = lax.broadcasted_iota(jnp.int32, s_n.shape, 1)
    r = jnp.where(r >= tq, r - tq, r)
    s_n = jnp.where(((past_len + c) // CHUNK) <= ((past_len + r) // CHUNK), s_n, NEG_BIG)
    m = jnp.maximum(jnp.max(s_c, axis=-1, keepdims=True), jnp.max(s_n, axis=-1, keepdims=True))
    p_c = jnp.exp2(s_c - m)
    p_n = jnp.exp2(s_n - m)
    inv_l = 1.0 / (jnp.sum(p_c, axis=-1, keepdims=True) + jnp.sum(p_n, axis=-1, keepdims=True))
    p_c = p_c * inv_l
    p_n = p_n * inv_l
    w_c = p_c[0:tq] - lam * p_c[tq:2 * tq]
    w_n = p_n[0:tq] - lam * p_n[tq:2 * tq]
    o = (jnp.dot(w_c.astype(BF16), vc.astype(BF16), preferred_element_type=F32)
         + jnp.dot(w_n.astype(BF16), vn, preferred_element_type=F32))
    return _subln(o, g, out_scale)


def _attn_sample(lam, q, cache_kt, cache_v, k_new, v_new, g, *, out_scale):
    B, S, W = q.shape
    P = cache_kt.shape[1]
    H = W // V_DIM
    blk_new = pl.BlockSpec((None, S, W), lambda b: (b, 0, 0))
    return pl.pallas_call(
        functools.partial(_attn_sample_kernel, past_len=P, n_heads=H, out_scale=out_scale),
        grid=(B,),
        in_specs=[pl.BlockSpec(memory_space=pltpu.SMEM), blk_new,
                  pl.BlockSpec((W, P), lambda b: (b, 0)),
                  pl.BlockSpec((P * H, V_DIM), lambda b: (b, 0)),
                  blk_new, blk_new, pl.BlockSpec((1, V_DIM), lambda b: (0, 0))],
        out_specs=blk_new,
        out_shape=jax.ShapeDtypeStruct((B, S, W), BF16),
        compiler_params=_cparams("parallel"),
        name="diff_attn_sample",
    )(lam, q, cache_kt, cache_v, k_new, v_new, g)


def _cmul(ar, ai, br, bi):
    return ar * br - ai * bi, ar * bi + ai * br


def _s5_kernel(*refs, n_steps, chained):
    if chained:
        (u_ref, perm_ref, permt_ref, wx_ref, cb_ref, d_ref, pre_ref, pim_ref, gre_ref, gim_ref,
         wglu_ref, ng_ref, y_ref, hre_ref, him_ref, xre, xim, cre, cim) = refs
    else:
        (u_ref, perm_ref, permt_ref, wx_ref, cb_ref, d_ref, pre_ref, pim_ref, gre_ref, gim_ref,
         wglu_ref, ng_ref, h0re_ref, h0im_ref, y_ref, hre_ref, him_ref, xre, xim) = refs
    R = n_steps
    n_blk = wx_ref.shape[0]
    blk_in = wx_ref.shape[1]
    blk_st = wx_ref.shape[2] // 2
    blk_shape = (SUBLANES, blk_st)
    if chained:
        @pl.when(pl.program_id(1) == 0)
        def _():
            cre[...] = jnp.zeros(cre.shape, F32)
            cim[...] = jnp.zeros(cim.shape, F32)

    u = u_ref[...]
    u_hi = u.astype(BF16)
    u_lo = (u - u_hi.astype(F32)).astype(BF16)
    perm = perm_ref[...]
    up = (jnp.dot(perm, u_hi, preferred_element_type=F32) + jnp.dot(perm, u_lo, preferred_element_type=F32))
    upb = up.astype(BF16)
    for j in range(n_blk):
        xj = jnp.dot(upb[:, blk_in * j:blk_in * (j + 1)], wx_ref[j], preferred_element_type=F32)
        xre[:, blk_st * j:blk_st * (j + 1)] = xj[:, :blk_st]
        xim[:, blk_st * j:blk_st * (j + 1)] = xj[:, blk_st:]

    sub = lax.broadcasted_iota(jnp.int32, blk_shape, 0)
    ys = []
    for j in range(n_blk):
        cs = slice(blk_st * j, blk_st * (j + 1))
        row = lambda ref, r: jnp.broadcast_to(ref[r:r + 1, cs], blk_shape)
        ar, ai = row(pre_ref, 0), row(pim_ref, 0)
        hr = hi = jnp.zeros(blk_shape, F32)
        for r in range(R):
            rows = slice(SUBLANES * r, SUBLANES * (r + 1))
            hr, hi = ar * hr - ai * hi + xre[rows, cs], ar * hi + ai * hr + xim[rows, cs]
            xre[rows, cs] = hr
            xim[rows, cs] = hi
        g1r, g1i = row(gre_ref, 0), row(gim_ref, 0)
        if chained:
            c_r = cre[:, cs]
            c_i = cim[:, cs]
            inj_r, inj_i = _cmul(g1r, g1i, c_r, c_i)
            xr = hr + jnp.where(sub == 0, inj_r, 0.0)
            xi = hi + jnp.where(sub == 0, inj_i, 0.0)
            for lvl, k in enumerate((1, 2, 4)):
                gr, gi = row(gre_ref, lvl), row(gim_ref, lvl)
                sr = jnp.where(sub >= k, pltpu.roll(xr, k, 0), 0.0)
                si = jnp.where(sub >= k, pltpu.roll(xi, k, 0), 0.0)
                tr, ti = _cmul(gr, gi, sr, si)
                xr = xr + tr
                xi = xi + ti
            hc_r = jnp.where(sub == 0, c_r, pltpu.roll(xr, 1, 0))
            hc_i = jnp.where(sub == 0, c_i, pltpu.roll(xi, 1, 0))
            last_r = jnp.broadcast_to(xr[SUBLANES - 1:SUBLANES, :], blk_shape)
            last_i = jnp.broadcast_to(xi[SUBLANES - 1:SUBLANES, :], blk_shape)
            cre[:, cs] = last_r
            cim[:, cs] = last_i
            hre_ref[:, cs] = last_r[0:1, :]
            him_ref[:, cs] = last_i[0:1, :]
        else:
            hc_r = h0re_ref[:, cs]
            hc_i = h0im_ref[:, cs]
            tr, ti = _cmul(g1r, g1i, hc_r, hc_i)
            hre_ref[:, cs] = hr + tr
            him_ref[:, cs] = hi + ti
        for r in range(R):
            rows = slice(SUBLANES * r, SUBLANES * (r + 1))
            tr, ti = _cmul(row(pre_ref, r), row(pim_ref, r), hc_r, hc_i)
            xre[rows, cs] = xre[rows, cs] + tr
            xim[rows, cs] = xim[rows, cs] + ti
        ys.append(jnp.dot(xre[:, cs].astype(BF16), cb_ref[j, 0:blk_st, :], preferred_element_type=F32)
                  + jnp.dot(xim[:, cs].astype(BF16), cb_ref[j, blk_st:, :], preferred_element_type=F32))
    y = jnp.concatenate(ys, axis=1) + d_ref[...] * up
    y = 0.5 * y * (1.0 + jnp.tanh(math.sqrt(2.0 / math.pi) * (y + 0.044715 * (y * y * y))))
    gl = jnp.dot(y.astype(BF16), wglu_ref[...], preferred_element_type=F32)
    half = gl.shape[1] // 2
    z = gl[:, :half] * (1.0 / (1.0 + jnp.exp(-gl[:, half:])))
    z = z * lax.rsqrt(jnp.mean(z * z, axis=-1, keepdims=True) + RMS_EPS) * ng_ref[...]
    y_ref[...] = jnp.dot(permt_ref[...], z.astype(BF16), preferred_element_type=F32).astype(y_ref.dtype)


def _s5_tables(a_re, a_im, log_dt, b_re, b_im, c_re, c_im, n_steps):
    G, N = a_re.shape
    C = b_re.shape[2]
    dt = jnp.exp(log_dt.astype(F32))[:, None]
    a_re = a_re.astype(F32)
    a_im = a_im.astype(F32)

    def power(k):
        mag = jnp.exp(k * a_re * dt)
        return mag * jnp.cos(k * a_im * dt), mag * jnp.sin(k * a_im * dt)

    ab_re, ab_im = power(1.0)
    nr, ni = ab_re - 1.0, ab_im
    den = a_re * a_re + a_im * a_im
    cf_re = (nr * a_re + ni * a_im) / den
    cf_im = (ni * a_re - nr * a_im) / den
    ks = jnp.arange(1, n_steps + 1, dtype=F32)[:, None, None]
    p_re, p_im = power(ks)
    g_re, g_im = zip(*[power(float(m * n_steps)) for m in (1, 2, 4)])
    pad = jnp.zeros((SUBLANES - 3, G * N), F32)
    g_re = jnp.concatenate([jnp.stack(g_re).reshape(3, G * N), pad])
    g_im = jnp.concatenate([jnp.stack(g_im).reshape(3, G * N), pad])
    wx_re = cf_re[..., None] * b_re - cf_im[..., None] * b_im
    wx_im = cf_re[..., None] * b_im + cf_im[..., None] * b_re
    gpb = LANES // C
    nb = G // gpb
    eye = jnp.eye(gpb, dtype=F32)

    def in_block(w):
        w = w.reshape(nb, gpb, N, C)
        return jnp.einsum('jgnc,gh->jgchn', w, eye).reshape(nb, gpb * C, gpb * N)

    def out_block(w):
        w = w.reshape(nb, gpb, C, N)
        return jnp.einsum('jgcn,gh->jgnhc', w, eye).reshape(nb, gpb * N, gpb * C)

    wx = jnp.concatenate([in_block(wx_re), in_block(wx_im)], axis=2).astype(BF16)
    cb = jnp.concatenate([out_block(c_re.astype(F32)), out_block(-c_im.astype(F32))], axis=1).astype(BF16)
    return (p_re.reshape(n_steps, G * N), p_im.reshape(n_steps, G * N), g_re, g_im, wx, cb)


def _perm_matrices(n_steps):
    tt = SUBLANES * n_steps
    dst = jnp.arange(tt)
    src = (dst % SUBLANES) * n_steps + dst // SUBLANES
    perm = (src[:, None] == jnp.arange(tt)[None, :]).astype(BF16)
    return perm, perm.T


def _s5_mixer(u, tables, d, w_glu_bf, norm_g, *, n_steps, h0=None):
    B, L, W = u.shape
    p_re, p_im, g_re, g_im, wx, cb = tables
    n_state = p_re.shape[1]
    tt = SUBLANES * n_steps
    perm, permt = _perm_matrices(n_steps)
    chained = h0 is None
    const2 = lambda *_: (0, 0)
    const3 = lambda *_: (0, 0, 0)
    common_specs = [pl.BlockSpec(perm.shape, const2), pl.BlockSpec(perm.shape, const2),
                    pl.BlockSpec(wx.shape, const3), pl.BlockSpec(cb.shape, const3),
                    pl.BlockSpec((1, W), const2),
                    pl.BlockSpec(p_re.shape, const2), pl.BlockSpec(p_im.shape, const2),
                    pl.BlockSpec(g_re.shape, const2), pl.BlockSpec(g_im.shape, const2),
                    pl.BlockSpec(w_glu_bf.shape, const2), pl.BlockSpec((1, W), const2)]
    common_args = (perm, permt, wx, cb, d, p_re, p_im, g_re, g_im, w_glu_bf, norm_g)
    state_scratch = [pltpu.VMEM((tt, n_state), F32), pltpu.VMEM((tt, n_state), F32)]
    kern = functools.partial(_s5_kernel, n_steps=n_steps, chained=chained)
    if chained:
        y, hre, him = pl.pallas_call(
            kern,
            grid=(B, L // tt),
            in_specs=[pl.BlockSpec((None, tt, W), lambda b, t: (b, t, 0))] + common_specs,
            out_specs=[pl.BlockSpec((None, tt, W), lambda b, t: (b, t, 0)),
                       pl.BlockSpec((None, 1, n_state), lambda b, t: (b, 0, 0)),
                       pl.BlockSpec((None, 1, n_state), lambda b, t: (b, 0, 0))],
            out_shape=[jax.ShapeDtypeStruct((B, L, W), BF16),
                       jax.ShapeDtypeStruct((B, 1, n_state), F32),
                       jax.ShapeDtypeStruct((B, 1, n_state), F32)],
            scratch_shapes=state_scratch + [pltpu.VMEM((SUBLANES, n_state), F32)] * 2,
            compiler_params=_cparams("parallel", "arbitrary"),
            name="s5_mixer_chained",
        )(u, *common_args)
        return y, hre.reshape(B, n_state), him.reshape(B, n_state)
    u2 = u.reshape(B * L, W)
    h0re, h0im = h0
    y, hre, him = pl.pallas_call(
        kern,
        grid=(B // SUBLANES,),
        in_specs=[pl.BlockSpec((tt, W), lambda i: (i, 0))] + common_specs
        + [pl.BlockSpec((SUBLANES, n_state), lambda i: (i, 0))] * 2,
        out_specs=[pl.BlockSpec((tt, W), lambda i: (i, 0)),
                   pl.BlockSpec((SUBLANES, n_state), lambda i: (i, 0)),
                   pl.BlockSpec((SUBLANES, n_state), lambda i: (i, 0))],
        out_shape=[jax.ShapeDtypeStruct((B * L, W), BF16),
                   jax.ShapeDtypeStruct((B, n_state), F32),
                   jax.ShapeDtypeStruct((B, n_state), F32)],
        scratch_shapes=state_scratch,
        compiler_params=_cparams("parallel"),
        name="s5_mixer_batched",
    )(u2, *common_args, h0re, h0im)
    return y.reshape(B, L, W), hre, him


def _layer_norm(z, g, b):
    mu = jnp.mean(z, axis=-1, keepdims=True)
    zc = z - mu
    var = jnp.mean(zc * zc, axis=-1, keepdims=True)
    return zc * lax.rsqrt(var + LN_EPS) * g + b


def _router_gates(x1, hi, rw_ref, rb_ref):
    lo = (x1 - hi.astype(F32)).astype(BF16)
    both = jnp.dot(hi, rw_ref[...], preferred_element_type=F32)
    lg = (both[:, :LANES] + both[:, LANES:]
          + jnp.dot(lo, rw_ref[:, :LANES], preferred_element_type=F32)) + rb_ref[...]
    lane = lax.broadcasted_iota(jnp.int32, lg.shape, 1)
    big = jnp.int32(LANES)
    is_grp = (lane >= N_EXPERTS) & (lane < N_EXPERTS + N_EXPERT_GROUPS)
    gl = jnp.where(is_grp, lg, NEG_BIG)
    gexp = jnp.where(is_grp, jnp.exp(gl - jnp.max(gl, axis=-1, keepdims=True)), 0.0)
    gprob = gexp / jnp.sum(gexp, axis=-1, keepdims=True)
    g_w = jnp.max(gprob, axis=-1, keepdims=True)
    g_idx = jnp.min(jnp.where(is_grp & (gprob == g_w), lane - N_EXPERTS, big), axis=-1, keepdims=True)
    valid = (lane < N_EXPERTS) & ((lane // EXPERTS_PER_GROUP) == g_idx)
    el = jnp.where(valid, lg, NEG_BIG)
    eexp = jnp.where(valid, jnp.exp(el - jnp.max(el, axis=-1, keepdims=True)), 0.0)
    eprob = eexp / jnp.sum(eexp, axis=-1, keepdims=True)
    w1 = jnp.max(jnp.where(valid, eprob, -1.0), axis=-1, keepdims=True)
    i1 = jnp.min(jnp.where(valid & (eprob == w1), lane, big), axis=-1, keepdims=True)
    rest = valid & (lane != i1)
    w2 = jnp.max(jnp.where(rest, eprob, -1.0), axis=-1, keepdims=True)
    i2 = jnp.min(jnp.where(rest & (eprob == w2), lane, big), axis=-1, keepdims=True)
    den = w1 + w2
    return (jnp.where(lane == i1, g_w * (w1 / den), 0.0)
            + jnp.where(lane == i2, g_w * (w2 / den), 0.0))


def _tail_kernel(att_ref, ssm_ref, x_ref, wa_ref, wb_ref, g1_ref, b1_ref, rw_ref, rb_ref,
                 wg_ref, wu_ref, wd_ref, g2_ref, b2_ref, o_ref, acc_ref, *, alpha):
    tm = x_ref.shape[0]
    rows_per = tm // TAIL_CHUNKS
    for c in range(TAIL_CHUNKS):
        rs = slice(rows_per * c, rows_per * (c + 1))
        mix = (jnp.dot(att_ref[rs, :], wa_ref[...], preferred_element_type=F32)
               + jnp.dot(ssm_ref[rs, :], wb_ref[...], preferred_element_type=F32))
        x1 = _layer_norm(alpha * x_ref[rs, :] + mix, g1_ref[...], b1_ref[...])
        xb = x1.astype(BF16)
        gates = _router_gates(x1, xb, rw_ref, rb_ref)
        for e in range(wg_ref.shape[0]):
            hg = jnp.dot(xb, wg_ref[e], preferred_element_type=F32)
            hu = jnp.dot(xb, wu_ref[e], preferred_element_type=F32)
            h = hg * (1.0 / (1.0 + jnp.exp(-hg))) * hu * gates[:, e:e + 1]
            contrib = jnp.dot(h.astype(BF16), wd_ref[e], preferred_element_type=F32)
            if e == 0:
                acc_ref[rs, :] = contrib
            else:
                acc_ref[rs, :] += contrib
        o_ref[rs, :] = _layer_norm(alpha * x1 + acc_ref[rs, :], g2_ref[...], b2_ref[...])


def _tail(att, ssm, x2d, wa, wb, g1, b1, rw, rb, wg, wu, wd, g2, b2, *, tm, alpha):
    T, D = x2d.shape
    Wh = att.shape[1]
    row = lambda i: (i, 0)
    const2 = lambda i: (0, 0)
    const3 = lambda i: (0, 0, 0)
    once = pl.Buffered(1)
    vec = pl.BlockSpec((1, D), const2)
    return pl.pallas_call(
        functools.partial(_tail_kernel, alpha=alpha),
        grid=(T // tm,),
        in_specs=[pl.BlockSpec((tm, Wh), row), pl.BlockSpec((tm, Wh), row), pl.BlockSpec((tm, D), row),
                  pl.BlockSpec(wa.shape, const2, pipeline_mode=once),
                  pl.BlockSpec(wb.shape, const2, pipeline_mode=once), vec, vec,
                  pl.BlockSpec(rw.shape, const2, pipeline_mode=once), pl.BlockSpec((1, LANES), const2),
                  pl.BlockSpec(wg.shape, const3, pipeline_mode=once),
                  pl.BlockSpec(wu.shape, const3, pipeline_mode=once),
                  pl.BlockSpec(wd.shape, const3, pipeline_mode=once), vec, vec],
        out_specs=pl.BlockSpec((tm, D), row),
        out_shape=jax.ShapeDtypeStruct((T, D), F32),
        scratch_shapes=[pltpu.VMEM((tm, D), F32)],
        compiler_params=_cparams("parallel"),
        name="out_proj_router_moe",
    )(att, ssm, x2d, wa, wb, g1, b1, rw, rb, wg, wu, wd, g2, b2)


def _rope_tables(pos):
    half = HEAD_DIM // 2
    inv = 1.0 / (ROPE_THETA ** (jnp.arange(half, dtype=F32) / half))
    ang = pos.astype(F32)[:, None] * inv[None, :]
    cos = jnp.cos(ang)
    sin = jnp.sin(ang)
    cos_t = jnp.tile(cos, (1, LANES // half))
    sin_t = jnp.tile(jnp.concatenate([-sin, sin], axis=1), (1, LANES // HEAD_DIM))
    return cos_t, sin_t


def _row_tile(n, pref):
    t = min(n, pref)
    while n % t:
        t //= 2
    return t


def kernel(x_prompt, x_sample, cache_k, cache_v, state_ssm_re, state_ssm_im, w_in, lam_q1, lam_k1, lam_q2, lam_k2, subln_g, ssm_a_re, ssm_a_im, ssm_log_dt, ssm_b_re, ssm_b_im, ssm_c_re, ssm_c_im, ssm_d, w_glu, ssm_norm_g, w_out, ln1_g, ln1_b, w_grp, b_grp, w_rt, b_rt, w_gate, w_up, w_down, ln2_g, ln2_b):
    depth = w_in.shape[0]
    assert depth == 1, "single-layer step"
    B, L, D = x_prompt.shape
    Bs, S, _ = x_sample.shape
    P = cache_k.shape[2]
    H = cache_k.shape[3]
    n_qk = H * 2 * HEAD_DIM
    G, N = ssm_a_re.shape[1], ssm_a_re.shape[2]
    alpha = (2.0 * depth) ** 0.25
    lam_init = 0.8 - 0.6 * math.exp(-0.3 * 0)
    out_scale = 1.0 - lam_init
    l = 0

    w_in_bf = w_in[l].astype(BF16)
    lam = (jnp.exp(jnp.sum(lam_q1[l].astype(F32) * lam_k1[l].astype(F32)))
           - jnp.exp(jnp.sum(lam_q2[l].astype(F32) * lam_k2[l].astype(F32))) + lam_init).reshape(1).astype(F32)
    g_sub = subln_g[l].astype(F32).reshape(1, V_DIM)
    w_glu_bf = w_glu[l].astype(BF16)
    ssm_w = D - H * V_DIM
    d_row = ssm_d[l].astype(F32).reshape(1, ssm_w)
    ng_row = ssm_norm_g[l].astype(F32).reshape(1, ssm_w)
    wa = w_out[l, :H * V_DIM].astype(BF16)
    wb = w_out[l, H * V_DIM:].astype(BF16)
    ln1g, ln1b = ln1_g[l].astype(F32).reshape(1, D), ln1_b[l].astype(F32).reshape(1, D)
    ln2g, ln2b = ln2_g[l].astype(F32).reshape(1, D), ln2_b[l].astype(F32).reshape(1, D)
    w_r = jnp.concatenate([w_rt[l].astype(F32), w_grp[l].astype(F32),
                           jnp.zeros((D, LANES - N_EXPERTS - N_EXPERT_GROUPS), F32)], axis=1)
    rhi = w_r.astype(BF16)
    rw = jnp.concatenate([rhi, (w_r - rhi.astype(F32)).astype(BF16)], axis=1)
    rb =jnp.concatenate([b_rt[l].astype(F32), b_grp[l].astype(F32),
                          jnp.zeros((LANES - N_EXPERTS - N_EXPERT_GROUPS,), F32)]).reshape(1, LANES)
    wg, wu, wd = w_gate[l].astype(BF16), w_up[l].astype(BF16), w_down[l].astype(BF16)

    def tail(att, ssm, x2d):
        return _tail(att, ssm, x2d, wa, wb, ln1g, ln1b, rw, rb, wg, wu, wd, ln2g, ln2b,
                     tm=_row_tile(x2d.shape[0], 512), alpha=alpha)

    cos_p, sin_p = _rope_tables(jnp.arange(L))
    xp2 = x_prompt.reshape(B * L, D)
    qp, kpf, kpb, vpf, vpb, up = _in_proj(xp2, w_in_bf, cos_p, sin_p, tm=_row_tile(L, 512),
                                          t_attn=_row_tile(L, 512))
    r3 = lambda a: a.reshape(B, L, -1)
    att_p = _attn_prompt(lam, qp, r3(kpb), vpb, g_sub, out_scale=out_scale)
    n_steps_p = _row_tile(L, 256) // SUBLANES
    tabs_p = _s5_tables(ssm_a_re[l], ssm_a_im[l], ssm_log_dt[l], ssm_b_re[l].astype(F32), ssm_b_im[l].astype(F32),
                        ssm_c_re[l], ssm_c_im[l], n_steps_p)
    ssm_p, hre_p, him_p = _s5_mixer(r3(up), tabs_p, d_row, w_glu_bf, ng_row, n_steps=n_steps_p)
    y_prompt = tail(att_p.reshape(B * L, -1), ssm_p.reshape(B * L, -1), xp2).reshape(B, L, D)

    cos_s, sin_s = _rope_tables(P + jnp.arange(S))
    cos_s, sin_s = jnp.tile(cos_s, (Bs, 1)), jnp.tile(sin_s, (Bs, 1))
    xs2 = x_sample.reshape(Bs * S, D)
    qs, ksf, ksb, vsf, vsb, us = _in_proj(xs2, w_in_bf, cos_s, sin_s, tm=_row_tile(Bs * S, 512))
    s3 = lambda a: a.reshape(Bs, S, -1)
    cache_kt = jnp.transpose(cache_k[l], (0, 2, 3, 4, 1)).reshape(Bs * n_qk, P)
    att_s = _attn_sample(lam, s3(qs), cache_kt, cache_v[l].reshape(Bs * P * H, V_DIM),
                         s3(ksb), s3(vsb), g_sub, out_scale=out_scale)
    tabs_s = _s5_tables(ssm_a_re[l], ssm_a_im[l], ssm_log_dt[l], ssm_b_re[l].astype(F32), ssm_b_im[l].astype(F32),
                        ssm_c_re[l], ssm_c_im[l], S)
    h0 = (state_ssm_re[l].astype(F32).reshape(Bs, G * N), state_ssm_im[l].astype(F32).reshape(Bs, G * N))
    ssm_s, hre_s, him_s = _s5_mixer(s3(us), tabs_s, d_row, w_glu_bf, ng_row, n_steps=S, h0=h0)
    y_sample = tail(att_s.reshape(Bs * S, -1), ssm_s.reshape(Bs * S, -1), xs2).reshape(Bs, S, D)

    return (y_prompt, y_sample,
            jnp.transpose(kpf.reshape(1, B, H, 2, HEAD_DIM, L), (0, 1, 5, 2, 3, 4)), vpf.reshape(1, B, L, H, V_DIM),
            hre_p.reshape(1, B, G, N), him_p.reshape(1, B, G, N),
            ksf.reshape(1, Bs, S, H, 2, HEAD_DIM), vsf.reshape(1, Bs, S, H, V_DIM),
            hre_s.reshape(1, Bs, G, N), him_s.reshape(1, Bs, G, N))
```

```python
import functools
import math

import jax
import jax.numpy as jnp
from jax import lax
from jax.experimental import pallas as pl
from jax.experimental.pallas import tpu as pltpu

F32 = jnp.float32
BF16 = jnp.bfloat16

HEAD_DIM = 64
V_DIM = 2 * HEAD_DIM
CHUNK = 64
SSM_GROUP = 16
SSM_STATE = 64
N_EXPERT_GROUPS = 4
EXPERTS_PER_GROUP = 4
N_EXPERTS = N_EXPERT_GROUPS * EXPERTS_PER_GROUP
ROPE_THETA = 10000.0
LN_EPS = 1e-5
RMS_EPS = 1e-6
LANES = 128
SUBLANES = 8
MXU_N = 256
ATTN_UNROLL = 16
ATTN_TAIL = 4
NEG_BIG = -1e30
SUM_ROWS = 16
TAIL_CHUNKS = 1
LOG2E = 1.4426950408889634
VMEM_LIMIT = 52 * 1024 * 1024


def _cparams(*sem):
    return pltpu.CompilerParams(dimension_semantics=sem, vmem_limit_bytes=VMEM_LIMIT)


def _store_transposed_tiles(ref, a):
    n_t, n_h, _, t = ref.shape
    at = a.T
    for n in range(n_t):
        for h in range(n_h):
            ref[n, h] = at[V_DIM * h:V_DIM * (h + 1), t * n:t * (n + 1)].astype(ref.dtype)


def _inproj_kernel(x_ref, w_ref, cos_ref, sin_ref, q_ref, kf_ref, kb_ref, vf_ref, vb_ref, u_ref,
                   *, q_scale, n_qk, transposed):
    xb = x_ref[...].astype(BF16)
    cos = cos_ref[...]
    sin = sin_ref[...]
    lane = lax.broadcasted_iota(jnp.int32, cos.shape, 1)
    first = (lane & (HEAD_DIM - 1)) < HEAD_DIM // 2

    def rope(t):
        rot = jnp.where(first, pltpu.roll(t, LANES - HEAD_DIM // 2, 1), pltpu.roll(t, HEAD_DIM // 2, 1))
        return t * cos + rot * sin

    pq = jnp.dot(xb, w_ref[:, 0:n_qk], preferred_element_type=F32)
    qr = jnp.concatenate([rope(pq[:, LANES * j:LANES * (j + 1)]) for j in range(n_qk // LANES)], axis=1) * q_scale
    if transposed:
        _store_transposed_tiles(q_ref, qr)
    else:
        q_ref[...] = qr.astype(BF16)
    pk = jnp.dot(xb, w_ref[:, n_qk:2 * n_qk], preferred_element_type=F32)
    kr = jnp.concatenate([rope(pk[:, LANES * j:LANES * (j + 1)]) for j in range(n_qk // LANES)], axis=1)
    kb_ref[...] = kr.astype(BF16)
    n_v = u_ref.shape[1]
    pv = jnp.dot(xb, w_ref[:, 2 * n_qk:2 * n_qk + n_v], preferred_element_type=F32)
    if transposed:
        kf_ref[...] = kr.T
        n_h = n_v // V_DIM
        for h in range(n_h):
            vf_ref[pl.ds(h, pv.shape[0], stride=n_h), :] = pv[:, V_DIM * h:V_DIM * (h + 1)]
        _store_transposed_tiles(vb_ref, pv)
    else:
        kf_ref[...] = kr
        vf_ref[...] = pv
        vb_ref[...] = pv.astype(BF16)
    u_ref[...] = jnp.dot(xb, w_ref[:, 2 * n_qk + n_v:], preferred_element_type=F32)


def _in_proj(x2d, w_bf, cos_t, sin_t, *, tm, t_attn=None):
    T, D = x2d.shape
    n_cols = w_bf.shape[1]
    n_qk = n_cols // 4
    n_tab = cos_t.shape[0] // tm
    row = lambda i: (i, 0)
    tab = lambda i: (i % n_tab, 0)
    out_sds = lambda dt: jax.ShapeDtypeStruct((T, n_qk), dt)
    spec = pl.BlockSpec((tm, n_qk), row)
    qv_spec, qv_sds = spec, out_sds(BF16)
    kf_spec, kf_sds, vf_spec, vf_sds = spec, out_sds(F32), spec, out_sds(F32)
    if t_attn is not None:
        H = n_qk // V_DIM
        qv_spec = pl.BlockSpec((tm // t_attn, H, V_DIM, t_attn), lambda i: (i, 0, 0, 0))
        qv_sds = jax.ShapeDtypeStruct((T // t_attn, H, V_DIM, t_attn), BF16)
        kf_spec = pl.BlockSpec((None, n_qk, tm), lambda i: (i // n_tab, 0, i % n_tab))
        kf_sds = jax.ShapeDtypeStruct((T // cos_t.shape[0], n_qk, cos_t.shape[0]), F32)
        vf_spec = pl.BlockSpec((tm * H, V_DIM), row)
        vf_sds = jax.ShapeDtypeStruct((T * H, V_DIM), F32)
    return pl.pallas_call(
        functools.partial(_inproj_kernel, q_scale=HEAD_DIM ** -0.5 * LOG2E, n_qk=n_qk,
                          transposed=t_attn is not None),
        grid=(T // tm,),
        in_specs=[pl.BlockSpec((tm, D), row),
                  pl.BlockSpec((D, n_cols), lambda i: (0, 0)),
                  pl.BlockSpec((tm, LANES), tab),
                  pl.BlockSpec((tm, LANES), tab)],
        out_specs=[qv_spec, kf_spec, spec, vf_spec, qv_spec, spec],
        out_shape=[qv_sds, kf_sds, out_sds(BF16), vf_sds, qv_sds, out_sds(F32)],
        compiler_params=_cparams("parallel"),
        name="in_proj_rope",
    )(x2d, w_bf, cos_t, sin_t)


def _split_maps(q):
    lane = lax.broadcasted_iota(jnp.int32, q.shape, 1)
    zero = jnp.zeros_like(q)
    return jnp.concatenate([jnp.where(lane < HEAD_DIM, q, zero), jnp.where(lane >= HEAD_DIM, q, zero)], axis=0)


def _qk(qs, kb):
    return lax.dot_general(qs, kb, (((1,), (1,)), ((), ())), preferred_element_type=F32)


def _subln(o, g, out_scale):
    ms = jnp.mean(o * o, axis=-1, keepdims=True)
    return o * lax.rsqrt(ms + RMS_EPS) * g * out_scale


def _attn_prompt_kernel(lam_ref, qt_ref, k_ref, vt_ref, g_ref, o_ref, qs_ref, acc_ref, s0_ref, s1_ref, m_ref,
                        *, tq, out_scale):
    i = pl.program_id(2)
    s_buf = (s0_ref, s1_ref)
    n_cb = 2 * tq // MXU_N
    qt = qt_ref[...]
    row = lax.broadcasted_iota(jnp.int32, qt.shape, 0)
    zero = jnp.zeros_like(qt)
    q_maps = (jnp.where(row < HEAD_DIM, qt, zero), jnp.where(row >= HEAD_DIM, qt, zero))
    for cb in range(n_cb):
        j = cb % (n_cb // 2)
        qs_ref[cb] = q_maps[cb // (n_cb // 2)][:, MXU_N * j:MXU_N * (j + 1)]
    acc_ref[...] = jnp.zeros(acc_ref.shape, F32)
    m_ref[...] = jnp.full(m_ref.shape, NEG_BIG, F32)
    ones_rows = jnp.ones((SUM_ROWS, tq), BF16)

    def scores(kt, slot):
        kb = k_ref[pl.ds(pl.multiple_of(kt * tq, tq), tq), :]
        for cb in range(n_cb):
            s_buf[slot][cb] = jnp.dot(kb, qs_ref[cb], preferred_element_type=F32)

    def softmax_pv(kt, slot, masked):
        lhs = jnp.concatenate([vt_ref[kt], ones_rows], axis=0)
        for cb in range(n_cb):
            st = s_buf[slot][cb]
            if masked:
                c = lax.broadcasted_iota(jnp.int32, st.shape, 0)
                r = (lax.broadcasted_iota(jnp.int32, st.shape, 1) + MXU_N * cb) % tq
                st = jnp.where((c // CHUNK) <= (r // CHUNK), st, NEG_BIG)
            m_prev = m_ref[cb]
            m_new = jnp.maximum(m_prev, jnp.max(st, axis=0, keepdims=True))
            alpha = jnp.exp2(m_prev - m_new)
            pt = jnp.exp2(st - m_new).astype(BF16)
            m_ref[cb] = m_new
            acc_ref[cb] = alpha * acc_ref[cb] + jnp.dot(lhs, pt, preferred_element_type=F32)

    def step(t, par):
        scores(t, par)
        softmax_pv(t - 1, 1 - par, False)

    scores(0, 0)

    def steps(t0, n):
        for d in range(1, n + 1):
            step(t0 + d, d % 2)

    def body(j, carry):
        steps(ATTN_UNROLL * j, ATTN_UNROLL)
        return carry

    lax.fori_loop(0, i // ATTN_UNROLL, body, 0)
    t_done = (i // ATTN_UNROLL) * ATTN_UNROLL
    rem_len = ATTN_UNROLL
    while rem_len > ATTN_TAIL:
        rem_len //= 2

        @pl.when((i - t_done) >= rem_len)
        def _(t_done=t_done, rem_len=rem_len):
            steps(t_done, rem_len)

        t_done = t_done + jnp.where((i - t_done) >= rem_len, rem_len, 0)
    for rem in range(ATTN_TAIL):
        @pl.when(i - t_done == rem)
        def _(rem=rem, t_done=t_done):
            steps(t_done, rem)
            softmax_pv(i, rem % 2, True)

    def normalized(cb):
        return acc_ref[cb, 0:V_DIM, :] * (1.0 / acc_ref[cb, V_DIM:V_DIM + 1, :])

    half = n_cb // 2
    ot = jnp.concatenate([normalized(j) - lam_ref[0] * normalized(j + half) for j in range(half)], axis=1)
    o_ref[...] = _subln(ot.T, g_ref[...], out_scale).astype(o_ref.dtype)


def _attn_prompt(lam, qt, k, vt, g, *, out_scale):
    B, L, W = k.shape
    n_t, H, _, t = qt.shape
    nq = L // t
    return pl.pallas_call(
        functools.partial(_attn_prompt_kernel, tq=t, out_scale=out_scale),
        grid=(B, H, nq),
        in_specs=[pl.BlockSpec(memory_space=pltpu.SMEM),
                  pl.BlockSpec((None, None, V_DIM, t), lambda b, h, i: (b * nq + i, h, 0, 0)),
                  pl.BlockSpec((None, L, V_DIM), lambda b, h, i: (b, 0, h)),
                  pl.BlockSpec((nq, None, V_DIM, t), lambda b, h, i: (b, h, 0, 0)),
                  pl.BlockSpec((1, V_DIM), lambda b, h, i: (0, 0))],
        out_specs=pl.BlockSpec((None, t, V_DIM), lambda b, h, i: (b, i, h)),
        out_shape=jax.ShapeDtypeStruct((B, L, W), BF16),
        scratch_shapes=[pltpu.VMEM((2 * t // MXU_N, V_DIM, MXU_N), BF16),
                        pltpu.VMEM((2 * t // MXU_N, V_DIM + SUM_ROWS, MXU_N), F32),
                        pltpu.VMEM((2 * t // MXU_N, t, MXU_N), F32), pltpu.VMEM((2 * t // MXU_N, t, MXU_N), F32),
                        pltpu.VMEM((2 * t // MXU_N, 1, MXU_N), F32)],
        compiler_params=_cparams("parallel", "parallel", "parallel"),
        name="diff_attn_prompt",
    )(lam, qt, k, vt, g)


def _attn_sample_kernel(lam_ref, q_ref, kct_ref, vc_ref, kn_ref, vn_ref, g_ref, o_ref, *, past_len, n_heads,
                        out_scale):
    for h in range(n_heads):
        hs = slice(V_DIM * h, V_DIM * (h + 1))
        vc = vc_ref[pl.ds(h, past_len, stride=n_heads), :]
        o_ref[:, hs] = _attn_sample_head(lam_ref[0], q_ref[:, hs], kct_ref[hs, :], vc, kn_ref[:, hs], vn_ref[:, hs],
                                         g_ref[...], past_len, out_scale).astype(o_ref.dtype)


def _attn_sample_head(lam, q, kct, vc, kn, vn, g, past_len, out_scale):
    tq = q.shape[0]
    qs = _split_maps(q)
    s_c = jnp.dot(qs, kct.astype(BF16), preferred_element_type=F32)
    s_n = _qk(qs, kn)
    r = lax.broadcasted_iota(jnp.int32, s_n.shape, 0)
    c = lax.broadcasted_iota(jnp.int32, s_n.shape, 1)
    r = jnp.where(r >= tq, r - tq, r)
    s_n = jnp.where(((past_len + c) // CHUNK) <= ((past_len + r) // CHUNK), s_n, NEG_BIG)
    m = jnp.maximum(jnp.max(s_c, axis=-1, keepdims=True), jnp.max(s_n, axis=-1, keepdims=True))
    p_c = jnp.exp2(s_c - m)
    p_n = jnp.exp2(s_n - m)
    inv_l = 1.0 / (jnp.sum(p_c, axis=-1, keepdims=True) + jnp.sum(p_n, axis=-1, keepdims=True))
    p_c = p_c * inv_l
    p_n = p_n * inv_l
    w_c = p_c[0:tq] - lam * p_c[tq:2 * tq]
    w_n = p_n[0:tq] - lam * p_n[tq:2 * tq]
    o = (jnp.dot(w_c.astype(BF16), vc.astype(BF16), preferred_element_type=F32)
         + jnp.dot(w_n.astype(BF16), vn, preferred_element_type=F32))
    return _subln(o, g, out_scale)


def _attn_sample(lam, q, cache_kt, cache_v, k_new, v_new, g, *, out_scale):
    B, S, W = q.shape
    P = cache_kt.shape[1]
    H = W // V_DIM
    blk_new = pl.BlockSpec((None, S, W), lambda b: (b, 0, 0))
    return pl.pallas_call(
        functools.partial(_attn_sample_kernel, past_len=P, n_heads=H, out_scale=out_scale),
        grid=(B,),
        in_specs=[pl.BlockSpec(memory_space=pltpu.SMEM), blk_new,
                  pl.BlockSpec((W, P), lambda b: (b, 0)),
                  pl.BlockSpec((P * H, V_DIM), lambda b: (b, 0)),
                  blk_new, blk_new, pl.BlockSpec((1, V_DIM), lambda b: (0, 0))],
        out_specs=blk_new,
        out_shape=jax.ShapeDtypeStruct((B, S, W), BF16),
        compiler_params=_cparams("parallel"),
        name="diff_attn_sample",
    )(lam, q, cache_kt, cache_v, k_new, v_new, g)


def _cmul(ar, ai, br, bi):
    return ar * br - ai * bi, ar * bi + ai * br


def _s5_kernel(*refs, n_steps, chained):
    if chained:
        (u_ref, perm_ref, permt_ref, wx_ref, cb_ref, d_ref, pre_ref, pim_ref, gre_ref, gim_ref,
         wglu_ref, ng_ref, y_ref, hre_ref, him_ref, xre, xim, cre, cim) = refs
    else:
        (u_ref, perm_ref, permt_ref, wx_ref, cb_ref, d_ref, pre_ref, pim_ref, gre_ref, gim_ref,
         wglu_ref, ng_ref, h0re_ref, h0im_ref, y_ref, hre_ref, him_ref, xre, xim) = refs
    R = n_steps
    n_blk = wx_ref.shape[0]
    blk_in = wx_ref.shape[1]
    blk_st = wx_ref.shape[2] // 2
    blk_shape = (SUBLANES, blk_st)
    if chained:
        @pl.when(pl.program_id(1) == 0)
        def _():
            cre[...] = jnp.zeros(cre.shape, F32)
            cim[...] = jnp.zeros(cim.shape, F32)

    u = u_ref[...]
    u_hi = u.astype(BF16)
    u_lo = (u - u_hi.astype(F32)).astype(BF16)
    perm = perm_ref[...]
    up = (jnp.dot(perm, u_hi, preferred_element_type=F32) + jnp.dot(perm, u_lo, preferred_element_type=F32))
    upb = up.astype(BF16)
    for j in range(n_blk):
        xj = jnp.dot(upb[:, blk_in * j:blk_in * (j + 1)], wx_ref[j], preferred_element_type=F32)
        xre[:, blk_st * j:blk_st * (j + 1)] = xj[:, :blk_st]
        xim[:, blk_st * j:blk_st * (j + 1)] = xj[:, blk_st:]

    sub = lax.broadcasted_iota(jnp.int32, blk_shape, 0)
    ys = []
    for j in range(n_blk):
        cs = slice(blk_st * j, blk_st * (j + 1))
        row = lambda ref, r: jnp.broadcast_to(ref[r:r + 1, cs], blk_shape)
        ar, ai = row(pre_ref, 0), row(pim_ref, 0)
        hr = hi = jnp.zeros(blk_shape, F32)
        for r in range(R):
            rows = slice(SUBLANES * r, SUBLANES * (r + 1))
            hr, hi = ar * hr - ai * hi + xre[rows, cs], ar * hi + ai * hr + xim[rows, cs]
            xre[rows, cs] = hr
            xim[rows, cs] = hi
        g1r, g1i = row(gre_ref, 0), row(gim_ref, 0)
        if chained:
            c_r = cre[:, cs]
            c_i = cim[:, cs]
            inj_r, inj_i = _cmul(g1r, g1i, c_r, c_i)
            xr = hr + jnp.where(sub == 0, inj_r, 0.0)
            xi = hi + jnp.where(sub == 0, inj_i, 0.0)
            for lvl, k in enumerate((1, 2, 4)):
                gr, gi = row(gre_ref, lvl), row(gim_ref, lvl)
                sr = jnp.where(sub >= k, pltpu.roll(xr, k, 0), 0.0)
                si = jnp.where(sub >= k, pltpu.roll(xi, k, 0), 0.0)
                tr, ti = _cmul(gr, gi, sr, si)
                xr = xr + tr
                xi = xi + ti
            hc_r = jnp.where(sub == 0, c_r, pltpu.roll(xr, 1, 0))
            hc_i = jnp.where(sub == 0, c_i, pltpu.roll(xi, 1, 0))
            last_r = jnp.broadcast_to(xr[SUBLANES - 1:SUBLANES, :], blk_shape)
            last_i = jnp.broadcast_to(xi[SUBLANES - 1:SUBLANES, :], blk_shape)
            cre[:, cs] = last_r
            cim[:, cs] = last_i
            hre_ref[:, cs] = last_r[0:1, :]
            him_ref[:, cs] = last_i[0:1, :]
        else:
            hc_r = h0re_ref[:, cs]
            hc_i = h0im_ref[:, cs]
            tr, ti = _cmul(g1r, g1i, hc_r, hc_i)
            hre_ref[:, cs] = hr + tr
            him_ref[:, cs] = hi + ti
        for r in range(R):
            rows = slice(SUBLANES * r, SUBLANES * (r + 1))
            tr, ti = _cmul(row(pre_ref, r), row(pim_ref, r), hc_r, hc_i)
            xre[rows, cs] = xre[rows, cs] + tr
            xim[rows, cs] = xim[rows, cs] + ti
        ys.append(jnp.dot(xre[:, cs].astype(BF16), cb_ref[j, 0:blk_st, :], preferred_element_type=F32)
                  + jnp.dot(xim[:, cs].astype(BF16), cb_ref[j, blk_st:, :], preferred_element_type=F32))
    y = jnp.concatenate(ys, axis=1) + d_ref[...] * up
    y = 0.5 * y * (1.0 + jnp.tanh(math.sqrt(2.0 / math.pi) * (y + 0.044715 * (y * y * y))))
    gl = jnp.dot(y.astype(BF16), wglu_ref[...], preferred_element_type=F32)
    half = gl.shape[1] // 2
    z = gl[:, :half] * (1.0 / (1.0 + jnp.exp(-gl[:, half:])))
    z = z * lax.rsqrt(jnp.mean(z * z, axis=-1, keepdims=True) + RMS_EPS) * ng_ref[...]
    y_ref[...] = jnp.dot(permt_ref[...], z.astype(BF16), preferred_element_type=F32).astype(y_ref.dtype)


def _s5_tables(a_re, a_im, log_dt, b_re, b_im, c_re, c_im, n_steps):
    G, N = a_re.shape
    C = b_re.shape[2]
    dt = jnp.exp(log_dt.astype(F32))[:, None]
    a_re = a_re.astype(F32)
    a_im = a_im.astype(F32)

    def power(k):
        mag = jnp.exp(k * a_re * dt)
        return mag * jnp.cos(k * a_im * dt), mag * jnp.sin(k * a_im * dt)

    ab_re, ab_im = power(1.0)
    nr, ni = ab_re - 1.0, ab_im
    den = a_re * a_re + a_im * a_im
    cf_re = (nr * a_re + ni * a_im) / den
    cf_im = (ni * a_re - nr * a_im) / den
    ks = jnp.arange(1, n_steps + 1, dtype=F32)[:, None, None]
    p_re, p_im = power(ks)
    g_re, g_im = zip(*[power(float(m * n_steps)) for m in (1, 2, 4)])
    pad = jnp.zeros((SUBLANES - 3, G * N), F32)
    g_re = jnp.concatenate([jnp.stack(g_re).reshape(3, G * N), pad])
    g_im = jnp.concatenate([jnp.stack(g_im).reshape(3, G * N), pad])
    wx_re = cf_re[..., None] * b_re - cf_im[..., None] * b_im
    wx_im = cf_re[..., None] * b_im + cf_im[..., None] * b_re
    gpb = LANES // C
    nb = G // gpb
    eye = jnp.eye(gpb, dtype=F32)

    def in_block(w):
        w = w.reshape(nb, gpb, N, C)
        return jnp.einsum('jgnc,gh->jgchn', w, eye).reshape(nb, gpb * C, gpb * N)

    def out_block(w):
        w = w.reshape(nb, gpb, C, N)
        return jnp.einsum('jgcn,gh->jgnhc', w, eye).reshape(nb, gpb * N, gpb * C)

    wx = jnp.concatenate([in_block(wx_re), in_block(wx_im)], axis=2).astype(BF16)
    cb = jnp.concatenate([out_block(c_re.astype(F32)), out_block(-c_im.astype(F32))], axis=1).astype(BF16)
    return (p_re.reshape(n_steps, G * N), p_im.reshape(n_steps, G * N), g_re, g_im, wx, cb)


def _perm_matrices(n_steps):
    tt = SUBLANES * n_steps
    dst = jnp.arange(tt)
    src = (dst % SUBLANES) * n_steps + dst // SUBLANES
    perm = (src[:, None] == jnp.arange(tt)[None, :]).astype(BF16)
    return perm, perm.T


def _s5_mixer(u, tables, d, w_glu_bf, norm_g, *, n_steps, h0=None):
    B, L, W = u.shape
    p_re, p_im, g_re, g_im, wx, cb = tables
    n_state = p_re.shape[1]
    tt = SUBLANES * n_steps
    perm, permt = _perm_matrices(n_steps)
    chained = h0 is None
    const2 = lambda *_: (0, 0)
    const3 = lambda *_: (0, 0, 0)
    common_specs = [pl.BlockSpec(perm.shape, const2), pl.BlockSpec(perm.shape, const2),
                    pl.BlockSpec(wx.shape, const3), pl.BlockSpec(cb.shape, const3),
                    pl.BlockSpec((1, W), const2),
                    pl.BlockSpec(p_re.shape, const2), pl.BlockSpec(p_im.shape, const2),
                    pl.BlockSpec(g_re.shape, const2), pl.BlockSpec(g_im.shape, const2),
                    pl.BlockSpec(w_glu_bf.shape, const2), pl.BlockSpec((1, W), const2)]
    common_args = (perm, permt, wx, cb, d, p_re, p_im, g_re, g_im, w_glu_bf, norm_g)
    state_scratch = [pltpu.VMEM((tt, n_state), F32), pltpu.VMEM((tt, n_state), F32)]
    kern = functools.partial(_s5_kernel, n_steps=n_steps, chained=chained)
    if chained:
        y, hre, him = pl.pallas_call(
            kern,
            grid=(B, L // tt),
            in_specs=[pl.BlockSpec((None, tt, W), lambda b, t: (b, t, 0))] + common_specs,
            out_specs=[pl.BlockSpec((None, tt, W), lambda b, t: (b, t, 0)),
                       pl.BlockSpec((None, 1, n_state), lambda b, t: (b, 0, 0)),
                       pl.BlockSpec((None, 1, n_state), lambda b, t: (b, 0, 0))],
            out_shape=[jax.ShapeDtypeStruct((B, L, W), BF16),
                       jax.ShapeDtypeStruct((B, 1, n_state), F32),
                       jax.ShapeDtypeStruct((B, 1, n_state), F32)],
            scratch_shapes=state_scratch + [pltpu.VMEM((SUBLANES, n_state), F32)] * 2,
            compiler_params=_cparams("parallel", "arbitrary"),
            name="s5_mixer_chained",
        )(u, *common_args)
        return y, hre.reshape(B, n_state), him.reshape(B, n_state)
    u2 = u.reshape(B * L, W)
    h0re, h0im = h0
    y, hre, him = pl.pallas_call(
        kern,
        grid=(B // SUBLANES,),
        in_specs=[pl.BlockSpec((tt, W), lambda i: (i, 0))] + common_specs
        + [pl.BlockSpec((SUBLANES, n_state), lambda i: (i, 0))] * 2,
        out_specs=[pl.BlockSpec((tt, W), lambda i: (i, 0)),
                   pl.BlockSpec((SUBLANES, n_state), lambda i: (i, 0)),
                   pl.BlockSpec((SUBLANES, n_state), lambda i: (i, 0))],
        out_shape=[jax.ShapeDtypeStruct((B * L, W), BF16),
                   jax.ShapeDtypeStruct((B, n_state), F32),
                   jax.ShapeDtypeStruct((B, n_state), F32)],
        scratch_shapes=state_scratch,
        compiler_params=_cparams("parallel"),
        name="s5_mixer_batched",
    )(u2, *common_args, h0re, h0im)
    return y.reshape(B, L, W), hre, him


def _layer_norm(z, g, b):
    mu = jnp.mean(z, axis=-1, keepdims=True)
    zc = z - mu
    var = jnp.mean(zc * zc, axis=-1, keepdims=True)
    return zc * lax.rsqrt(var + LN_EPS) * g + b


def _router_gates(x1, hi, rw_ref, rb_ref):
    lo = (x1 - hi.astype(F32)).astype(BF16)
    both = jnp.dot(hi, rw_ref[...], preferred_element_type=F32)
    lg = (both[:, :LANES] + both[:, LANES:]
          + jnp.dot(lo, rw_ref[:, :LANES], preferred_element_type=F32)) + rb_ref[...]
    lane = lax.broadcasted_iota(jnp.int32, lg.shape, 1)
    big = jnp.int32(LANES)
    is_grp = (lane >= N_EXPERTS) & (lane < N_EXPERTS + N_EXPERT_GROUPS)
    gl = jnp.where(is_grp, lg, NEG_BIG)
    gexp = jnp.where(is_grp, jnp.exp(gl - jnp.max(gl, axis=-1, keepdims=True)), 0.0)
    gprob = gexp / jnp.sum(gexp, axis=-1, keepdims=True)
    g_w = jnp.max(gprob, axis=-1, keepdims=True)
    g_idx = jnp.min(jnp.where(is_grp & (gprob == g_w), lane - N_EXPERTS, big), axis=-1, keepdims=True)
    valid = (lane < N_EXPERTS) & ((lane // EXPERTS_PER_GROUP) == g_idx)
    el = jnp.where(valid, lg, NEG_BIG)
    eexp = jnp.where(valid, jnp.exp(el - jnp.max(el, axis=-1, keepdims=True)), 0.0)
    eprob = eexp / jnp.sum(eexp, axis=-1, keepdims=True)
    w1 = jnp.max(jnp.where(valid, eprob, -1.0), axis=-1, keepdims=True)
    i1 = jnp.min(jnp.where(valid & (eprob == w1), lane, big), axis=-1, keepdims=True)
    rest = valid & (lane != i1)
    w2 = jnp.max(jnp.where(rest, eprob, -1.0), axis=-1, keepdims=True)
    i2 = jnp.min(jnp.where(rest & (eprob == w2), lane, big), axis=-1, keepdims=True)
    den = w1 + w2
    return (jnp.where(lane == i1, g_w * (w1 / den), 0.0)
            + jnp.where(lane == i2, g_w * (w2 / den), 0.0))


def _tail_kernel(att_ref, ssm_ref, x_ref, wa_ref, wb_ref, g1_ref, b1_ref, rw_ref, rb_ref,
                 wg_ref, wu_ref, wd_ref, g2_ref, b2_ref, o_ref, acc_ref, *, alpha):
    tm = x_ref.shape[0]
    rows_per = tm // TAIL_CHUNKS
    for c in range(TAIL_CHUNKS):
        rs = slice(rows_per * c, rows_per * (c + 1))
        mix = (jnp.dot(att_ref[rs, :], wa_ref[...], preferred_element_type=F32)
               + jnp.dot(ssm_ref[rs, :], wb_ref[...], preferred_element_type=F32))
        x1 = _layer_norm(alpha * x_ref[rs, :] + mix, g1_ref[...], b1_ref[...])
        xb = x1.astype(BF16)
        gates = _router_gates(x1, xb, rw_ref, rb_ref)
        for e in range(wg_ref.shape[0]):
            hg = jnp.dot(xb, wg_ref[e], preferred_element_type=F32)
            hu = jnp.dot(xb, wu_ref[e], preferred_element_type=F32)
            h = hg * (1.0 / (1.0 + jnp.exp(-hg))) * hu * gates[:, e:e + 1]
            contrib = jnp.dot(h.astype(BF16), wd_ref[e], preferred_element_type=F32)
            if e == 0:
                acc_ref[rs, :] = contrib
            else:
                acc_ref[rs, :] += contrib
        o_ref[rs, :] = _layer_norm(alpha * x1 + acc_ref[rs, :], g2_ref[...], b2_ref[...])


def _tail(att, ssm, x2d, wa, wb, g1, b1, rw, rb, wg, wu, wd, g2, b2, *, tm, alpha):
    T, D = x2d.shape
    Wh = att.shape[1]
    row = lambda i: (i, 0)
    const2 = lambda i: (0, 0)
    const3 = lambda i: (0, 0, 0)
    once = pl.Buffered(1)
    vec = pl.BlockSpec((1, D), const2)
    return pl.pallas_call(
        functools.partial(_tail_kernel, alpha=alpha),
        grid=(T // tm,),
        in_specs=[pl.BlockSpec((tm, Wh), row), pl.BlockSpec((tm, Wh), row), pl.BlockSpec((tm, D), row),
                  pl.BlockSpec(wa.shape, const2, pipeline_mode=once),
                  pl.BlockSpec(wb.shape, const2, pipeline_mode=once), vec, vec,
                  pl.BlockSpec(rw.shape, const2, pipeline_mode=once), pl.BlockSpec((1, LANES), const2),
                  pl.BlockSpec(wg.shape, const3, pipeline_mode=once),
                  pl.BlockSpec(wu.shape, const3, pipeline_mode=once),
                  pl.BlockSpec(wd.shape, const3, pipeline_mode=once), vec, vec],
        out_specs=pl.BlockSpec((tm, D), row),
        out_shape=jax.ShapeDtypeStruct((T, D), F32),
        scratch_shapes=[pltpu.VMEM((tm, D), F32)],
        compiler_params=_cparams("parallel"),
        name="out_proj_router_moe",
    )(att, ssm, x2d, wa, wb, g1, b1, rw, rb, wg, wu, wd, g2, b2)


def _rope_tables(pos):
    half = HEAD_DIM // 2
    inv = 1.0 / (ROPE_THETA ** (jnp.arange(half, dtype=F32) / half))
    ang = pos.astype(F32)[:, None] * inv[None, :]
    cos = jnp.cos(ang)
    sin = jnp.sin(ang)
    cos_t = jnp.tile(cos, (1, LANES // half))
    sin_t = jnp.tile(jnp.concatenate([-sin, sin], axis=1), (1, LANES // HEAD_DIM))
    return cos_t, sin_t


def _row_tile(n, pref):
    t = min(n, pref)
    while n % t:
        t //= 2
    return t


def kernel(x_prompt, x_sample, cache_k, cache_v, state_ssm_re, state_ssm_im, w_in, lam_q1, lam_k1, lam_q2, lam_k2, subln_g, ssm_a_re, ssm_a_im, ssm_log_dt, ssm_b_re, ssm_b_im, ssm_c_re, ssm_c_im, ssm_d, w_glu, ssm_norm_g, w_out, ln1_g, ln1_b, w_grp, b_grp, w_rt, b_rt, w_gate, w_up, w_down, ln2_g, ln2_b):
    depth = w_in.shape[0]
    assert depth == 1, "single-layer step"
    B, L, D = x_prompt.shape
    Bs, S, _ = x_sample.shape
    P = cache_k.shape[2]
    H = cache_k.shape[3]
    n_qk = H * 2 * HEAD_DIM
    G, N = ssm_a_re.shape[1], ssm_a_re.shape[2]
    alpha = (2.0 * depth) ** 0.25
    lam_init = 0.8 - 0.6 * math.exp(-0.3 * 0)
    out_scale = 1.0 - lam_init
    l = 0

    w_in_bf = w_in[l].astype(BF16)
    lam = (jnp.exp(jnp.sum(lam_q1[l].astype(F32) * lam_k1[l].astype(F32)))
           - jnp.exp(jnp.sum(lam_q2[l].astype(F32) * lam_k2[l].astype(F32))) + lam_init).reshape(1).astype(F32)
    g_sub = subln_g[l].astype(F32).reshape(1, V_DIM)
    w_glu_bf = w_glu[l].astype(BF16)
    ssm_w = D - H * V_DIM
    d_row = ssm_d[l].astype(F32).reshape(1, ssm_w)
    ng_row = ssm_norm_g[l].astype(F32).reshape(1, ssm_w)
    wa = w_out[l, :H * V_DIM].astype(BF16)
    wb = w_out[l, H * V_DIM:].astype(BF16)
    ln1g, ln1b = ln1_g[l].astype(F32).reshape(1, D), ln1_b[l].astype(F32).reshape(1, D)
    ln2g, ln2b = ln2_g[l].astype(F32).reshape(1, D), ln2_b[l].astype(F32).reshape(1, D)
    w_r = jnp.concatenate([w_rt[l].astype(F32), w_grp[l].astype(F32),
                           jnp.zeros((D, LANES - N_EXPERTS - N_EXPERT_GROUPS), F32)], axis=1)
    rhi = w_r.astype(BF16)
    rw = jnp.concatenate([rhi, (w_r - rhi.astype(F32)).astype(BF16)], axis=1)
    rb =jnp.concatenate([b_rt[l].astype(F32), b_grp[l].astype(F32),
                          jnp.zeros((LANES - N_EXPERTS - N_EXPERT_GROUPS,), F32)]).reshape(1, LANES)
    wg, wu, wd = w_gate[l].astype(BF16), w_up[l].astype(BF16), w_down[l].astype(BF16)

    def tail(att, ssm, x2d):
        return _tail(att, ssm, x2d, wa, wb, ln1g, ln1b, rw, rb, wg, wu, wd, ln2g, ln2b,
                     tm=_row_tile(x2d.shape[0], 512), alpha=alpha)

    cos_p, sin_p = _rope_tables(jnp.arange(L))
    xp2 = x_prompt.reshape(B * L, D)
    qp, kpf, kpb, vpf, vpb, up = _in_proj(xp2, w_in_bf, cos_p, sin_p, tm=_row_tile(L, 512),
                                          t_attn=_row_tile(L, 512))
    r3 = lambda a: a.reshape(B, L, -1)
    att_p = _attn_prompt(lam, qp, r3(kpb), vpb, g_sub, out_scale=out_scale)
    n_steps_p = _row_tile(L, 256) // SUBLANES
    tabs_p = _s5_tables(ssm_a_re[l], ssm_a_im[l], ssm_log_dt[l], ssm_b_re[l].astype(F32), ssm_b_im[l].astype(F32),
                        ssm_c_re[l], ssm_c_im[l], n_steps_p)
    ssm_p, hre_p, him_p = _s5_mixer(r3(up), tabs_p, d_row, w_glu_bf, ng_row, n_steps=n_steps_p)
    y_prompt = tail(att_p.reshape(B * L, -1), ssm_p.reshape(B * L, -1), xp2).reshape(B, L, D)

    cos_s, sin_s = _rope_tables(P + jnp.arange(S))
    cos_s, sin_s = jnp.tile(cos_s, (Bs, 1)), jnp.tile(sin_s, (Bs, 1))
    xs2 = x_sample.reshape(Bs * S, D)
    qs, ksf, ksb, vsf, vsb, us = _in_proj(xs2, w_in_bf, cos_s, sin_s, tm=_row_tile(Bs * S, 512))
    s3 = lambda a: a.reshape(Bs, S, -1)
    cache_kt = jnp.transpose(cache_k[l], (0, 2, 3, 4, 1)).reshape(Bs * n_qk, P)
    att_s = _attn_sample(lam, s3(qs), cache_kt, cache_v[l].reshape(Bs * P * H, V_DIM),
                         s3(ksb), s3(vsb), g_sub, out_scale=out_scale)
    tabs_s = _s5_tables(ssm_a_re[l], ssm_a_im[l], ssm_log_dt[l], ssm_b_re[l].astype(F32), ssm_b_im[l].astype(F32),
                        ssm_c_re[l], ssm_c_im[l], S)
    h0 = (state_ssm_re[l].astype(F32).reshape(Bs, G * N), state_ssm_im[l].astype(F32).reshape(Bs, G * N))
    ssm_s, hre_s, him_s = _s5_mixer(s3(us), tabs_s, d_row, w_glu_bf, ng_row, n_steps=S, h0=h0)
    y_sample = tail(att_s.reshape(Bs * S, -1), ssm_s.reshape(Bs * S, -1), xs2).reshape(Bs, S, D)

    return (y_prompt, y_sample,
            jnp.transpose(kpf.reshape(1, B, H, 2, HEAD_DIM, L), (0, 1, 5, 2, 3, 4)), vpf.reshape(1, B, L, H, V_DIM),
            hre_p.reshape(1, B, G, N), him_p.reshape(1, B, G, N),
            ksf.reshape(1, Bs, S, H, 2, HEAD_DIM), vsf.reshape(1, Bs, S, H, V_DIM),
            hre_s.reshape(1, Bs, G, N), him_s.reshape(1, Bs, G, N))
```

```python
import functools
import math

import jax
import jax.numpy as jnp
from jax import lax
from jax.experimental import pallas as pl
from jax.experimental.pallas import tpu as pltpu

F32 = jnp.float32
BF16 = jnp.bfloat16

HEAD_DIM = 64
V_DIM = 2 * HEAD_DIM
CHUNK = 64
N_EXPERT_GROUPS = 4
EXPERTS_PER_GROUP = 4
N_EXPERTS = N_EXPERT_GROUPS * EXPERTS_PER_GROUP
ROPE_THETA = 10000.0
LN_EPS = 1e-5
RMS_EPS = 1e-6
LANES = 128
SUBLANES = 8
MXU_N = 256
ATTN_UNROLL = 16
ATTN_TAIL = 4
NEG_BIG = -1e30
SUM_ROWS = 16
ROW_TILE = 512
S5_TILE = 256
LOG2E = 1.4426950408889634
VMEM_LIMIT = 52 * 1024 * 1024


def _cparams(*sem):
    return pltpu.CompilerParams(dimension_semantics=sem, vmem_limit_bytes=VMEM_LIMIT)


def _store_transposed_tiles(ref, a):
    n_t, n_h, _, t = ref.shape
    at = a.T
    for n in range(n_t):
        for h in range(n_h):
            ref[n, h] = at[V_DIM * h:V_DIM * (h + 1), t * n:t * (n + 1)].astype(ref.dtype)


def _inproj_kernel(x_ref, w_ref, cos_ref, sin_ref, q_ref, kf_ref, kb_ref, vf_ref, vb_ref, u_ref,
                   *, q_scale, n_qk, transposed):
    xb = x_ref[...].astype(BF16)
    cos = cos_ref[...]
    sin = sin_ref[...]
    lane = lax.broadcasted_iota(jnp.int32, cos.shape, 1)
    first = (lane & (HEAD_DIM - 1)) < HEAD_DIM // 2

    def rope(t):
        rot = jnp.where(first, pltpu.roll(t, LANES - HEAD_DIM // 2, 1), pltpu.roll(t, HEAD_DIM // 2, 1))
        return t * cos + rot * sin

    pq = jnp.dot(xb, w_ref[:, 0:n_qk], preferred_element_type=F32)
    qr = jnp.concatenate([rope(pq[:, LANES * j:LANES * (j + 1)]) for j in range(n_qk // LANES)], axis=1) * q_scale
    if transposed:
        _store_transposed_tiles(q_ref, qr)
    else:
        q_ref[...] = qr.astype(BF16)
    pk = jnp.dot(xb, w_ref[:, n_qk:2 * n_qk], preferred_element_type=F32)
    kr = jnp.concatenate([rope(pk[:, LANES * j:LANES * (j + 1)]) for j in range(n_qk // LANES)], axis=1)
    kb_ref[...] = kr.astype(BF16)
    n_v = u_ref.shape[1]
    pv = jnp.dot(xb, w_ref[:, 2 * n_qk:2 * n_qk + n_v], preferred_element_type=F32)
    if transposed:
        kf_ref[...] = kr.T
        n_h = n_v // V_DIM
        for h in range(n_h):
            vf_ref[pl.ds(h, pv.shape[0], stride=n_h), :] = pv[:, V_DIM * h:V_DIM * (h + 1)]
        _store_transposed_tiles(vb_ref, pv)
    else:
        kf_ref[...] = kr
        vf_ref[...] = pv
        vb_ref[...] = pv.astype(BF16)
    u_ref[...] = jnp.dot(xb, w_ref[:, 2 * n_qk + n_v:], preferred_element_type=F32)


def _in_proj(x2d, w_bf, cos_t, sin_t, *, tm, t_attn=None):
    T, D = x2d.shape
    n_cols = w_bf.shape[1]
    n_qk = n_cols // 4
    n_tab = cos_t.shape[0] // tm
    row = lambda i: (i, 0)
    tab = lambda i: (i % n_tab, 0)
    out_sds = lambda dt: jax.ShapeDtypeStruct((T, n_qk), dt)
    spec = pl.BlockSpec((tm, n_qk), row)
    qv_spec, qv_sds = spec, out_sds(BF16)
    kf_spec, kf_sds, vf_spec, vf_sds = spec, out_sds(F32), spec, out_sds(F32)
    if t_attn is not None:
        H = n_qk // V_DIM
        qv_spec = pl.BlockSpec((tm // t_attn, H, V_DIM, t_attn), lambda i: (i, 0, 0, 0))
        qv_sds = jax.ShapeDtypeStruct((T // t_attn, H, V_DIM, t_attn), BF16)
        kf_spec = pl.BlockSpec((None, n_qk, tm), lambda i: (i // n_tab, 0, i % n_tab))
        kf_sds = jax.ShapeDtypeStruct((T // cos_t.shape[0], n_qk, cos_t.shape[0]), F32)
        vf_spec = pl.BlockSpec((tm * H, V_DIM), row)
        vf_sds = jax.ShapeDtypeStruct((T * H, V_DIM), F32)
    return pl.pallas_call(
        functools.partial(_inproj_kernel, q_scale=HEAD_DIM ** -0.5 * LOG2E, n_qk=n_qk,
                          transposed=t_attn is not None),
        grid=(T // tm,),
        in_specs=[pl.BlockSpec((tm, D), row),
                  pl.BlockSpec((D, n_cols), lambda i: (0, 0)),
                  pl.BlockSpec((tm, LANES), tab),
                  pl.BlockSpec((tm, LANES), tab)],
        out_specs=[qv_spec, kf_spec, spec, vf_spec, qv_spec, spec],
        out_shape=[qv_sds, kf_sds, out_sds(BF16), vf_sds, qv_sds, out_sds(F32)],
        compiler_params=_cparams("parallel"),
        name="in_proj_rope",
    )(x2d, w_bf, cos_t, sin_t)


def _split_maps(q):
    lane = lax.broadcasted_iota(jnp.int32, q.shape, 1)
    zero = jnp.zeros_like(q)
    return jnp.concatenate([jnp.where(lane < HEAD_DIM, q, zero), jnp.where(lane >= HEAD_DIM, q, zero)], axis=0)


def _qk(qs, kb):
    return lax.dot_general(qs, kb, (((1,), (1,)), ((), ())), preferred_element_type=F32)


def _subln(o, g, out_scale):
    ms = jnp.mean(o * o, axis=-1, keepdims=True)
    return o * lax.rsqrt(ms + RMS_EPS) * g * out_scale


def _attn_prompt_kernel(lam_ref, qt_ref, k_ref, vt_ref, g_ref, o_ref, qs_ref, acc_ref, s0_ref, s1_ref, m_ref,
                        *, tq, out_scale):
    i = pl.program_id(2)
    s_buf = (s0_ref, s1_ref)
    n_cb = 2 * tq // MXU_N
    qt = qt_ref[...]
    row = lax.broadcasted_iota(jnp.int32, qt.shape, 0)
    zero = jnp.zeros_like(qt)
    q_maps = (jnp.where(row < HEAD_DIM, qt, zero), jnp.where(row >= HEAD_DIM, qt, zero))
    for cb in range(n_cb):
        j = cb % (n_cb // 2)
        qs_ref[cb] = q_maps[cb // (n_cb // 2)][:, MXU_N * j:MXU_N * (j + 1)]
    acc_ref[...] = jnp.zeros(acc_ref.shape, F32)
    m_ref[...] = jnp.full(m_ref.shape, NEG_BIG, F32)
    ones_rows = jnp.ones((SUM_ROWS, tq), BF16)

    def scores(kt, slot):
        kb = k_ref[pl.ds(pl.multiple_of(kt * tq, tq), tq), :]
        for cb in range(n_cb):
            s_buf[slot][cb] = jnp.dot(kb, qs_ref[cb], preferred_element_type=F32)

    def softmax_pv(kt, slot, masked):
        lhs = jnp.concatenate([vt_ref[kt], ones_rows], axis=0)
        for cb in range(n_cb):
            st = s_buf[slot][cb]
            if masked:
                c = lax.broadcasted_iota(jnp.int32, st.shape, 0)
                r = (lax.broadcasted_iota(jnp.int32, st.shape, 1) + MXU_N * cb) % tq
                st = jnp.where((c // CHUNK) <= (r // CHUNK), st, NEG_BIG)
            m_prev = m_ref[cb]
            m_new = jnp.maximum(m_prev, jnp.max(st, axis=0, keepdims=True))
            alpha = jnp.exp2(m_prev - m_new)
            pt = jnp.exp2(st - m_new).astype(BF16)
            m_ref[cb] = m_new
            acc_ref[cb] = alpha * acc_ref[cb] + jnp.dot(lhs, pt, preferred_element_type=F32)

    def step(t, par):
        scores(t, par)
        softmax_pv(t - 1, 1 - par, False)

    scores(0, 0)

    def steps(t0, n):
        for d in range(1, n + 1):
            step(t0 + d, d % 2)

    def body(j, carry):
        steps(ATTN_UNROLL * j, ATTN_UNROLL)
        return carry

    lax.fori_loop(0, i // ATTN_UNROLL, body, 0)
    t_done = (i // ATTN_UNROLL) * ATTN_UNROLL
    rem_len = ATTN_UNROLL
    while rem_len > ATTN_TAIL:
        rem_len //= 2

        @pl.when((i - t_done) >= rem_len)
        def _(t_done=t_done, rem_len=rem_len):
            steps(t_done, rem_len)

        t_done = t_done + jnp.where((i - t_done) >= rem_len, rem_len, 0)
    for rem in range(ATTN_TAIL):
        @pl.when(i - t_done == rem)
        def _(rem=rem, t_done=t_done):
            steps(t_done, rem)
            softmax_pv(i, rem % 2, True)

    def normalized(cb):
        return acc_ref[cb, 0:V_DIM, :] * (1.0 / acc_ref[cb, V_DIM:V_DIM + 1, :])

    half = n_cb // 2
    ot = jnp.concatenate([normalized(j) - lam_ref[0] * normalized(j + half) for j in range(half)], axis=1)
    o_ref[...] = _subln(ot.T, g_ref[...], out_scale).astype(o_ref.dtype)


def _attn_prompt(lam, qt, k, vt, g, *, out_scale):
    B, L, W = k.shape
    n_t, H, _, t = qt.shape
    nq = L // t
    return pl.pallas_call(
        functools.partial(_attn_prompt_kernel, tq=t, out_scale=out_scale),
        grid=(B, H, nq),
        in_specs=[pl.BlockSpec(memory_space=pltpu.SMEM),
                  pl.BlockSpec((None, None, V_DIM, t), lambda b, h, i: (b * nq + i, h, 0, 0)),
                  pl.BlockSpec((None, L, V_DIM), lambda b, h, i: (b, 0, h)),
                  pl.BlockSpec((nq, None, V_DIM, t), lambda b, h, i: (b, h, 0, 0)),
                  pl.BlockSpec((1, V_DIM), lambda b, h, i: (0, 0))],
        out_specs=pl.BlockSpec((None, t, V_DIM), lambda b, h, i: (b, i, h)),
        out_shape=jax.ShapeDtypeStruct((B, L, W), BF16),
        scratch_shapes=[pltpu.VMEM((2 * t // MXU_N, V_DIM, MXU_N), BF16),
                        pltpu.VMEM((2 * t // MXU_N, V_DIM + SUM_ROWS, MXU_N), F32),
                        pltpu.VMEM((2 * t // MXU_N, t, MXU_N), F32), pltpu.VMEM((2 * t // MXU_N, t, MXU_N), F32),
                        pltpu.VMEM((2 * t // MXU_N, 1, MXU_N), F32)],
        compiler_params=_cparams("parallel", "parallel", "parallel"),
        name="diff_attn_prompt",
    )(lam, qt, k, vt, g)


def _attn_sample_kernel(lam_ref, q_ref, kct_ref, vc_ref, kn_ref, vn_ref, g_ref, o_ref, *, past_len, n_heads,
                        out_scale):
    for h in range(n_heads):
        hs = slice(V_DIM * h, V_DIM * (h + 1))
        vc = vc_ref[pl.ds(h, past_len, stride=n_heads), :]
        o_ref[:, hs] = _attn_sample_head(lam_ref[0], q_ref[:, hs], kct_ref[hs, :], vc, kn_ref[:, hs], vn_ref[:, hs],
                                         g_ref[...], past_len, out_scale).astype(o_ref.dtype)


def _attn_sample_head(lam, q, kct, vc, kn, vn, g, past_len, out_scale):
    tq = q.shape[0]
    qs = _split_maps(q)
    s_c = jnp.dot(qs, kct.astype(BF16), preferred_element_type=F32)
    s_n = _qk(qs, kn)
    r = lax.broadcasted_iota(jnp.int32, s_n.shape, 0)
    c = lax.broadcasted_iota(jnp.int32, s_n.shape, 1)
    r = jnp.where(r >= tq, r - tq, r)
    s_n = jnp.where(((past_len + c) // CHUNK) <= ((past_len + r) // CHUNK), s_n, NEG_BIG)
    m = jnp.maximum(jnp.max(s_c, axis=-1, keepdims=True), jnp.max(s_n, axis=-1, keepdims=True))
    p_c = jnp.exp2(s_c - m)
    p_n = jnp.exp2(s_n - m)
    inv_l = 1.0 / (jnp.sum(p_c, axis=-1, keepdims=True) + jnp.sum(p_n, axis=-1, keepdims=True))
    p_c = p_c * inv_l
    p_n = p_n * inv_l
    w_c = p_c[0:tq] - lam * p_c[tq:2 * tq]
    w_n = p_n[0:tq] - lam * p_n[tq:2 * tq]
    o = (jnp.dot(w_c.astype(BF16), vc.astype(BF16), preferred_element_type=F32)
         + jnp.dot(w_n.astype(BF16), vn, preferred_element_type=F32))
    return _subln(o, g, out_scale)


def _attn_sample(lam, q, cache_kt, cache_v, k_new, v_new, g, *, out_scale):
    B, S, W = q.shape
    P = cache_kt.shape[1]
    H = W // V_DIM
    blk_new = pl.BlockSpec((None, S, W), lambda b: (b, 0, 0))
    return pl.pallas_call(
        functools.partial(_attn_sample_kernel, past_len=P, n_heads=H, out_scale=out_scale),
        grid=(B,),
        in_specs=[pl.BlockSpec(memory_space=pltpu.SMEM), blk_new,
                  pl.BlockSpec((W, P), lambda b: (b, 0)),
                  pl.BlockSpec((P * H, V_DIM), lambda b: (b, 0)),
                  blk_new, blk_new, pl.BlockSpec((1, V_DIM), lambda b: (0, 0))],
        out_specs=blk_new,
        out_shape=jax.ShapeDtypeStruct((B, S, W), BF16),
        compiler_params=_cparams("parallel"),
        name="diff_attn_sample",
    )(lam, q, cache_kt, cache_v, k_new, v_new, g)


def _cmul(ar, ai, br, bi):
    return ar * br - ai * bi, ar * bi + ai * br


def _s5_kernel(*refs, n_steps, chained):
    if chained:
        (u_ref, perm_ref, permt_ref, wx_ref, cb_ref, d_ref, pre_ref, pim_ref, gre_ref, gim_ref,
         wglu_ref, ng_ref, y_ref, hre_ref, him_ref, xre, xim, cre, cim) = refs
    else:
        (u_ref, perm_ref, permt_ref, wx_ref, cb_ref, d_ref, pre_ref, pim_ref, gre_ref, gim_ref,
         wglu_ref, ng_ref, h0re_ref, h0im_ref, y_ref, hre_ref, him_ref, xre, xim) = refs
    R = n_steps
    n_blk = wx_ref.shape[0]
    blk_in = wx_ref.shape[1]
    blk_st = wx_ref.shape[2] // 2
    blk_shape = (SUBLANES, blk_st)
    if chained:
        @pl.when(pl.program_id(1) == 0)
        def _():
            cre[...] = jnp.zeros(cre.shape, F32)
            cim[...] = jnp.zeros(cim.shape, F32)

    u = u_ref[...]
    u_hi = u.astype(BF16)
    u_lo = (u - u_hi.astype(F32)).astype(BF16)
    perm = perm_ref[...]
    up = (jnp.dot(perm, u_hi, preferred_element_type=F32) + jnp.dot(perm, u_lo, preferred_element_type=F32))
    upb = up.astype(BF16)
    for j in range(n_blk):
        xj = jnp.dot(upb[:, blk_in * j:blk_in * (j + 1)], wx_ref[j], preferred_element_type=F32)
        xre[:, blk_st * j:blk_st * (j + 1)] = xj[:, :blk_st]
        xim[:, blk_st * j:blk_st * (j + 1)] = xj[:, blk_st:]

    sub = lax.broadcasted_iota(jnp.int32, blk_shape, 0)
    ys = []
    for j in range(n_blk):
        cs = slice(blk_st * j, blk_st * (j + 1))
        row = lambda ref, r: jnp.broadcast_to(ref[r:r + 1, cs], blk_shape)
        ar, ai = row(pre_ref, 0), row(pim_ref, 0)
        hr = hi = jnp.zeros(blk_shape, F32)
        for r in range(R):
            rows = slice(SUBLANES * r, SUBLANES * (r + 1))
            hr, hi = ar * hr - ai * hi + xre[rows, cs], ar * hi + ai * hr + xim[rows, cs]
            xre[rows, cs] = hr
            xim[rows, cs] = hi
        g1r, g1i = row(gre_ref, 0), row(gim_ref, 0)
        if chained:
            c_r = cre[:, cs]
            c_i = cim[:, cs]
            inj_r, inj_i = _cmul(g1r, g1i, c_r, c_i)
            xr = hr + jnp.where(sub == 0, inj_r, 0.0)
            xi = hi + jnp.where(sub == 0, inj_i, 0.0)
            for lvl, k in enumerate((1, 2, 4)):
                gr, gi = row(gre_ref, lvl), row(gim_ref, lvl)
                sr = jnp.where(sub >= k, pltpu.roll(xr, k, 0), 0.0)
                si = jnp.where(sub >= k, pltpu.roll(xi, k, 0), 0.0)
                tr, ti = _cmul(gr, gi, sr, si)
                xr = xr + tr
                xi = xi + ti
            hc_r = jnp.where(sub == 0, c_r, pltpu.roll(xr, 1, 0))
            hc_i = jnp.where(sub == 0, c_i, pltpu.roll(xi, 1, 0))
            last_r = jnp.broadcast_to(xr[SUBLANES - 1:SUBLANES, :], blk_shape)
            last_i = jnp.broadcast_to(xi[SUBLANES - 1:SUBLANES, :], blk_shape)
            cre[:, cs] = last_r
            cim[:, cs] = last_i
            hre_ref[:, cs] = last_r[0:1, :]
            him_ref[:, cs] = last_i[0:1, :]
        else:
            hc_r = h0re_ref[:, cs]
            hc_i = h0im_ref[:, cs]
            tr, ti = _cmul(g1r, g1i, hc_r, hc_i)
            hre_ref[:, cs] = hr + tr
            him_ref[:, cs] = hi + ti
        for r in range(R):
            rows = slice(SUBLANES * r, SUBLANES * (r + 1))
            tr, ti = _cmul(row(pre_ref, r), row(pim_ref, r), hc_r, hc_i)
            xre[rows, cs] = xre[rows, cs] + tr
            xim[rows, cs] = xim[rows, cs] + ti
        ys.append(jnp.dot(xre[:, cs].astype(BF16), cb_ref[j, 0:blk_st, :], preferred_element_type=F32)
                  + jnp.dot(xim[:, cs].astype(BF16), cb_ref[j, blk_st:, :], preferred_element_type=F32))
    y = jnp.concatenate(ys, axis=1) + d_ref[...] * up
    y = 0.5 * y * (1.0 + jnp.tanh(math.sqrt(2.0 / math.pi) * (y + 0.044715 * (y * y * y))))
    gl = jnp.dot(y.astype(BF16), wglu_ref[...], preferred_element_type=F32)
    half = gl.shape[1] // 2
    z = gl[:, :half] * (1.0 / (1.0 + jnp.exp(-gl[:, half:])))
    z = z * lax.rsqrt(jnp.mean(z * z, axis=-1, keepdims=True) + RMS_EPS) * ng_ref[...]
    y_ref[...] = jnp.dot(permt_ref[...], z.astype(BF16), preferred_element_type=F32).astype(y_ref.dtype)


def _s5_tables(a_re, a_im, log_dt, b_re, b_im, c_re, c_im, n_steps):
    G, N = a_re.shape
    C = b_re.shape[2]
    dt = jnp.exp(log_dt.astype(F32))[:, None]
    a_re = a_re.astype(F32)
    a_im = a_im.astype(F32)

    def power(k):
        mag = jnp.exp(k * a_re * dt)
        return mag * jnp.cos(k * a_im * dt), mag * jnp.sin(k * a_im * dt)

    ab_re, ab_im = power(1.0)
    nr, ni = ab_re - 1.0, ab_im
    den = a_re * a_re + a_im * a_im
    cf_re = (nr * a_re + ni * a_im) / den
    cf_im = (ni * a_re - nr * a_im) / den
    ks = jnp.arange(1, n_steps + 1, dtype=F32)[:, None, None]
    p_re, p_im = power(ks)
    g_re, g_im = zip(*[power(float(m * n_steps)) for m in (1, 2, 4)])
    pad = jnp.zeros((SUBLANES - 3, G * N), F32)
    g_re = jnp.concatenate([jnp.stack(g_re).reshape(3, G * N), pad])
    g_im = jnp.concatenate([jnp.stack(g_im).reshape(3, G * N), pad])
    wx_re = cf_re[..., None] * b_re - cf_im[..., None] * b_im
    wx_im = cf_re[..., None] * b_im + cf_im[..., None] * b_re
    gpb = LANES // C
    nb = G // gpb
    eye = jnp.eye(gpb, dtype=F32)

    def in_block(w):
        w = w.reshape(nb, gpb, N, C)
        return jnp.einsum('jgnc,gh->jgchn', w, eye).reshape(nb, gpb * C, gpb * N)

    def out_block(w):
        w = w.reshape(nb, gpb, C, N)
        return jnp.einsum('jgcn,gh->jgnhc', w, eye).reshape(nb, gpb * N, gpb * C)

    wx = jnp.concatenate([in_block(wx_re), in_block(wx_im)], axis=2).astype(BF16)
    cb = jnp.concatenate([out_block(c_re.astype(F32)), out_block(-c_im.astype(F32))], axis=1).astype(BF16)
    return (p_re.reshape(n_steps, G * N), p_im.reshape(n_steps, G * N), g_re, g_im, wx, cb)


def _perm_matrices(n_steps):
    tt = SUBLANES * n_steps
    dst = jnp.arange(tt)
    src = (dst % SUBLANES) * n_steps + dst // SUBLANES
    perm = (src[:, None] == jnp.arange(tt)[None, :]).astype(BF16)
    return perm, perm.T


def _s5_mixer(u, tables, d, w_glu_bf, norm_g, *, n_steps, h0=None):
    B, L, W = u.shape
    p_re, p_im, g_re, g_im, wx, cb = tables
    n_state = p_re.shape[1]
    tt = SUBLANES * n_steps
    perm, permt = _perm_matrices(n_steps)
    chained = h0 is None
    const2 = lambda *_: (0, 0)
    const3 = lambda *_: (0, 0, 0)
    common_specs = [pl.BlockSpec(perm.shape, const2), pl.BlockSpec(perm.shape, const2),
                    pl.BlockSpec(wx.shape, const3), pl.BlockSpec(cb.shape, const3),
                    pl.BlockSpec((1, W), const2),
                    pl.BlockSpec(p_re.shape, const2), pl.BlockSpec(p_im.shape, const2),
                    pl.BlockSpec(g_re.shape, const2), pl.BlockSpec(g_im.shape, const2),
                    pl.BlockSpec(w_glu_bf.shape, const2), pl.BlockSpec((1, W), const2)]
    common_args = (perm, permt, wx, cb, d, p_re, p_im, g_re, g_im, w_glu_bf, norm_g)
    state_scratch = [pltpu.VMEM((tt, n_state), F32), pltpu.VMEM((tt, n_state), F32)]
    kern = functools.partial(_s5_kernel, n_steps=n_steps, chained=chained)
    if chained:
        y, hre, him = pl.pallas_call(
            kern,
            grid=(B, L // tt),
            in_specs=[pl.BlockSpec((None, tt, W), lambda b, t: (b, t, 0))] + common_specs,
            out_specs=[pl.BlockSpec((None, tt, W), lambda b, t: (b, t, 0)),
                       pl.BlockSpec((None, 1, n_state), lambda b, t: (b, 0, 0)),
                       pl.BlockSpec((None, 1, n_state), lambda b, t: (b, 0, 0))],
            out_shape=[jax.ShapeDtypeStruct((B, L, W), BF16),
                       jax.ShapeDtypeStruct((B, 1, n_state), F32),
                       jax.ShapeDtypeStruct((B, 1, n_state), F32)],
            scratch_shapes=state_scratch + [pltpu.VMEM((SUBLANES, n_state), F32)] * 2,
            compiler_params=_cparams("parallel", "arbitrary"),
            name="s5_mixer_chained",
        )(u, *common_args)
        return y, hre.reshape(B, n_state), him.reshape(B, n_state)
    u2 = u.reshape(B * L, W)
    h0re, h0im = h0
    y, hre, him = pl.pallas_call(
        kern,
        grid=(B // SUBLANES,),
        in_specs=[pl.BlockSpec((tt, W), lambda i: (i, 0))] + common_specs
        + [pl.BlockSpec((SUBLANES, n_state), lambda i: (i, 0))] * 2,
        out_specs=[pl.BlockSpec((tt, W), lambda i: (i, 0)),
                   pl.BlockSpec((SUBLANES, n_state), lambda i: (i, 0)),
                   pl.BlockSpec((SUBLANES, n_state), lambda i: (i, 0))],
        out_shape=[jax.ShapeDtypeStruct((B * L, W), BF16),
                   jax.ShapeDtypeStruct((B, n_state), F32),
                   jax.ShapeDtypeStruct((B, n_state), F32)],
        scratch_shapes=state_scratch,
        compiler_params=_cparams("parallel"),
        name="s5_mixer_batched",
    )(u2, *common_args, h0re, h0im)
    return y.reshape(B, L, W), hre, him


def _layer_norm(z, g, b):
    mu = jnp.mean(z, axis=-1, keepdims=True)
    zc = z - mu
    var = jnp.mean(zc * zc, axis=-1, keepdims=True)
    return zc * lax.rsqrt(var + LN_EPS) * g + b


def _router_gates(x1, hi, rw_ref, rb_ref):
    lo = (x1 - hi.astype(F32)).astype(BF16)
    both = jnp.dot(hi, rw_ref[...], preferred_element_type=F32)
    lg = (both[:, :LANES] + both[:, LANES:]
          + jnp.dot(lo, rw_ref[:, :LANES], preferred_element_type=F32)) + rb_ref[...]
    lane = lax.broadcasted_iota(jnp.int32, lg.shape, 1)
    big = jnp.int32(LANES)
    is_grp = (lane >= N_EXPERTS) & (lane < N_EXPERTS + N_EXPERT_GROUPS)
    gl = jnp.where(is_grp, lg, NEG_BIG)
    gexp = jnp.where(is_grp, jnp.exp(gl - jnp.max(gl, axis=-1, keepdims=True)), 0.0)
    gprob = gexp / jnp.sum(gexp, axis=-1, keepdims=True)
    g_w = jnp.max(gprob, axis=-1, keepdims=True)
    g_idx = jnp.min(jnp.where(is_grp & (gprob == g_w), lane - N_EXPERTS, big), axis=-1, keepdims=True)
    valid = (lane < N_EXPERTS) & ((lane // EXPERTS_PER_GROUP) == g_idx)
    el = jnp.where(valid, lg, NEG_BIG)
    eexp = jnp.where(valid, jnp.exp(el - jnp.max(el, axis=-1, keepdims=True)), 0.0)
    eprob = eexp / jnp.sum(eexp, axis=-1, keepdims=True)
    w1 = jnp.max(jnp.where(valid, eprob, -1.0), axis=-1, keepdims=True)
    i1 = jnp.min(jnp.where(valid & (eprob == w1), lane, big), axis=-1, keepdims=True)
    rest = valid & (lane != i1)
    w2 = jnp.max(jnp.where(rest, eprob, -1.0), axis=-1, keepdims=True)
    i2 = jnp.min(jnp.where(rest & (eprob == w2), lane, big), axis=-1, keepdims=True)
    den = w1 + w2
    return (jnp.where(lane == i1, g_w * (w1 / den), 0.0)
            + jnp.where(lane == i2, g_w * (w2 / den), 0.0))


def _tail_kernel(att_ref, ssm_ref, x_ref, wa_ref, wb_ref, g1_ref, b1_ref, rw_ref, rb_ref,
                 wg_ref, wu_ref, wd_ref, g2_ref, b2_ref, o_ref, acc_ref, *, alpha):
    mix = (jnp.dot(att_ref[...], wa_ref[...], preferred_element_type=F32)
           + jnp.dot(ssm_ref[...], wb_ref[...], preferred_element_type=F32))
    x1 = _layer_norm(alpha * x_ref[...] + mix, g1_ref[...], b1_ref[...])
    xb = x1.astype(BF16)
    gates = _router_gates(x1, xb, rw_ref, rb_ref)
    for e in range(wg_ref.shape[0]):
        hg = jnp.dot(xb, wg_ref[e], preferred_element_type=F32)
        hu = jnp.dot(xb, wu_ref[e], preferred_element_type=F32)
        h = hg * (1.0 / (1.0 + jnp.exp(-hg))) * hu * gates[:, e:e + 1]
        contrib = jnp.dot(h.astype(BF16), wd_ref[e], preferred_element_type=F32)
        if e == 0:
            acc_ref[...] = contrib
        else:
            acc_ref[...] += contrib
    o_ref[...] = _layer_norm(alpha * x1 + acc_ref[...], g2_ref[...], b2_ref[...])


def _tail(att, ssm, x2d, wa, wb, g1, b1, rw, rb, wg, wu, wd, g2, b2, *, tm, alpha):
    T, D = x2d.shape
    Wh = att.shape[1]
    row = lambda i: (i, 0)
    const2 = lambda i: (0, 0)
    const3 = lambda i: (0, 0, 0)
    once = pl.Buffered(1)
    vec = pl.BlockSpec((1, D), const2)
    return pl.pallas_call(
        functools.partial(_tail_kernel, alpha=alpha),
        grid=(T // tm,),
        in_specs=[pl.BlockSpec((tm, Wh), row), pl.BlockSpec((tm, Wh), row), pl.BlockSpec((tm, D), row),
                  pl.BlockSpec(wa.shape, const2, pipeline_mode=once),
                  pl.BlockSpec(wb.shape, const2, pipeline_mode=once), vec, vec,
                  pl.BlockSpec(rw.shape, const2, pipeline_mode=once), pl.BlockSpec((1, LANES), const2),
                  pl.BlockSpec(wg.shape, const3, pipeline_mode=once),
                  pl.BlockSpec(wu.shape, const3, pipeline_mode=once),
                  pl.BlockSpec(wd.shape, const3, pipeline_mode=once), vec, vec],
        out_specs=pl.BlockSpec((tm, D), row),
        out_shape=jax.ShapeDtypeStruct((T, D), F32),
        scratch_shapes=[pltpu.VMEM((tm, D), F32)],
        compiler_params=_cparams("parallel"),
        name="out_proj_router_moe",
    )(att, ssm, x2d, wa, wb, g1, b1, rw, rb, wg, wu, wd, g2, b2)


def _rope_tables(pos):
    half = HEAD_DIM // 2
    inv = 1.0 / (ROPE_THETA ** (jnp.arange(half, dtype=F32) / half))
    ang = pos.astype(F32)[:, None] * inv[None, :]
    cos = jnp.cos(ang)
    sin = jnp.sin(ang)
    cos_t = jnp.tile(cos, (1, LANES // half))
    sin_t = jnp.tile(jnp.concatenate([-sin, sin], axis=1), (1, LANES // HEAD_DIM))
    return cos_t, sin_t


def _row_tile(n, pref):
    t = min(n, pref)
    while n % t:
        t //= 2
    return t


def kernel(x_prompt, x_sample, cache_k, cache_v, state_ssm_re, state_ssm_im, w_in, lam_q1, lam_k1, lam_q2, lam_k2, subln_g, ssm_a_re, ssm_a_im, ssm_log_dt, ssm_b_re, ssm_b_im, ssm_c_re, ssm_c_im, ssm_d, w_glu, ssm_norm_g, w_out, ln1_g, ln1_b, w_grp, b_grp, w_rt, b_rt, w_gate, w_up, w_down, ln2_g, ln2_b):
    depth = w_in.shape[0]
    assert depth == 1, "single-layer step"
    B, L, D = x_prompt.shape
    Bs, S, _ = x_sample.shape
    P = cache_k.shape[2]
    H = cache_k.shape[3]
    n_qk = H * 2 * HEAD_DIM
    G, N = ssm_a_re.shape[1], ssm_a_re.shape[2]
    alpha = (2.0 * depth) ** 0.25
    lam_init = 0.8 - 0.6 * math.exp(-0.3 * 0)
    out_scale = 1.0 - lam_init
    l = 0

    w_in_bf = w_in[l].astype(BF16)
    lam = (jnp.exp(jnp.sum(lam_q1[l].astype(F32) * lam_k1[l].astype(F32)))
           - jnp.exp(jnp.sum(lam_q2[l].astype(F32) * lam_k2[l].astype(F32))) + lam_init).reshape(1).astype(F32)
    g_sub = subln_g[l].astype(F32).reshape(1, V_DIM)
    w_glu_bf = w_glu[l].astype(BF16)
    ssm_w = D - H * V_DIM
    d_row = ssm_d[l].astype(F32).reshape(1, ssm_w)
    ng_row = ssm_norm_g[l].astype(F32).reshape(1, ssm_w)
    wa = w_out[l, :H * V_DIM].astype(BF16)
    wb = w_out[l, H * V_DIM:].astype(BF16)
    ln1g, ln1b = ln1_g[l].astype(F32).reshape(1, D), ln1_b[l].astype(F32).reshape(1, D)
    ln2g, ln2b = ln2_g[l].astype(F32).reshape(1, D), ln2_b[l].astype(F32).reshape(1, D)
    w_r = jnp.concatenate([w_rt[l].astype(F32), w_grp[l].astype(F32),
                           jnp.zeros((D, LANES - N_EXPERTS - N_EXPERT_GROUPS), F32)], axis=1)
    rhi = w_r.astype(BF16)
    rw = jnp.concatenate([rhi, (w_r - rhi.astype(F32)).astype(BF16)], axis=1)
    rb =jnp.concatenate([b_rt[l].astype(F32), b_grp[l].astype(F32),
                          jnp.zeros((LANES - N_EXPERTS - N_EXPERT_GROUPS,), F32)]).reshape(1, LANES)
    wg, wu, wd = w_gate[l].astype(BF16), w_up[l].astype(BF16), w_down[l].astype(BF16)

    def tail(att, ssm, x2d):
        return _tail(att, ssm, x2d, wa, wb, ln1g, ln1b, rw, rb, wg, wu, wd, ln2g, ln2b,
                     tm=_row_tile(x2d.shape[0], ROW_TILE), alpha=alpha)

    cos_p, sin_p = _rope_tables(jnp.arange(L))
    xp2 = x_prompt.reshape(B * L, D)
    qp, kpf, kpb, vpf, vpb, up = _in_proj(xp2, w_in_bf, cos_p, sin_p, tm=_row_tile(L, ROW_TILE),
                                          t_attn=_row_tile(L, ROW_TILE))
    r3 = lambda a: a.reshape(B, L, -1)
    att_p = _attn_prompt(lam, qp, r3(kpb), vpb, g_sub, out_scale=out_scale)
    n_steps_p = _row_tile(L, S5_TILE) // SUBLANES
    tabs_p = _s5_tables(ssm_a_re[l], ssm_a_im[l], ssm_log_dt[l], ssm_b_re[l].astype(F32), ssm_b_im[l].astype(F32),
                        ssm_c_re[l], ssm_c_im[l], n_steps_p)
    ssm_p, hre_p, him_p = _s5_mixer(r3(up), tabs_p, d_row, w_glu_bf, ng_row, n_steps=n_steps_p)
    y_prompt = tail(att_p.reshape(B * L, -1), ssm_p.reshape(B * L, -1), xp2).reshape(B, L, D)

    cos_s, sin_s = _rope_tables(P + jnp.arange(S))
    cos_s, sin_s = jnp.tile(cos_s, (Bs, 1)), jnp.tile(sin_s, (Bs, 1))
    xs2 = x_sample.reshape(Bs * S, D)
    qs, ksf, ksb, vsf, vsb, us = _in_proj(xs2, w_in_bf, cos_s, sin_s, tm=_row_tile(Bs * S, ROW_TILE))
    s3 = lambda a: a.reshape(Bs, S, -1)
    cache_kt = jnp.transpose(cache_k[l], (0, 2, 3, 4, 1)).reshape(Bs * n_qk, P)
    att_s = _attn_sample(lam, s3(qs), cache_kt, cache_v[l].reshape(Bs * P * H, V_DIM),
                         s3(ksb), s3(vsb), g_sub, out_scale=out_scale)
    tabs_s = _s5_tables(ssm_a_re[l], ssm_a_im[l], ssm_log_dt[l], ssm_b_re[l].astype(F32), ssm_b_im[l].astype(F32),
                        ssm_c_re[l], ssm_c_im[l], S)
    h0 = (state_ssm_re[l].astype(F32).reshape(Bs, G * N), state_ssm_im[l].astype(F32).reshape(Bs, G * N))
    ssm_s, hre_s, him_s = _s5_mixer(s3(us), tabs_s, d_row, w_glu_bf, ng_row, n_steps=S, h0=h0)
    y_sample = tail(att_s.reshape(Bs * S, -1), ssm_s.reshape(Bs * S, -1), xs2).reshape(Bs, S, D)

    return (y_prompt, y_sample,
            jnp.transpose(kpf.reshape(1, B, H, 2, HEAD_DIM, L), (0, 1, 5, 2, 3, 4)), vpf.reshape(1, B, L, H, V_DIM),
            hre_p.reshape(1, B, G, N), him_p.reshape(1, B, G, N),
            ksf.reshape(1, Bs, S, H, 2, HEAD_DIM), vsf.reshape(1, Bs, S, H, V_DIM),
            hre_s.reshape(1, Bs, G, N), him_s.reshape(1, Bs, G, N))
```

```python
import functools
import math

import jax
import jax.numpy as jnp
from jax import lax
from jax.experimental import pallas as pl
from jax.experimental.pallas import tpu as pltpu

F32 = jnp.float32
BF16 = jnp.bfloat16

HEAD_DIM = 64
V_DIM = 2 * HEAD_DIM
CHUNK = 64
N_EXPERT_GROUPS = 4
EXPERTS_PER_GROUP = 4
N_EXPERTS = N_EXPERT_GROUPS * EXPERTS_PER_GROUP
ROPE_THETA = 10000.0
LN_EPS = 1e-5
RMS_EPS = 1e-6
LANES = 128
SUBLANES = 8
MXU_N = 256
ATTN_UNROLL = 16
ATTN_TAIL = 4
NEG_BIG = -1e30
SUM_ROWS = 16
ROW_TILE = 512
S5_TILE = 512
LOG2E = 1.4426950408889634
VMEM_LIMIT = 52 * 1024 * 1024


def _cparams(*sem):
    return pltpu.CompilerParams(dimension_semantics=sem, vmem_limit_bytes=VMEM_LIMIT)


def _store_transposed_tiles(ref, a):
    n_t, n_h, _, t = ref.shape
    at = a.T
    for n in range(n_t):
        for h in range(n_h):
            ref[n, h] = at[V_DIM * h:V_DIM * (h + 1), t * n:t * (n + 1)].astype(ref.dtype)


def _inproj_kernel(x_ref, w_ref, cos_ref, sin_ref, q_ref, kf_ref, kb_ref, vf_ref, vb_ref, u_ref,
                   *, q_scale, n_qk, transposed):
    xb = x_ref[...].astype(BF16)
    cos = cos_ref[...]
    sin = sin_ref[...]
    lane = lax.broadcasted_iota(jnp.int32, cos.shape, 1)
    first = (lane & (HEAD_DIM - 1)) < HEAD_DIM // 2

    def rope(t):
        rot = jnp.where(first, pltpu.roll(t, LANES - HEAD_DIM // 2, 1), pltpu.roll(t, HEAD_DIM // 2, 1))
        return t * cos + rot * sin

    pq = jnp.dot(xb, w_ref[:, 0:n_qk], preferred_element_type=F32)
    qr = jnp.concatenate([rope(pq[:, LANES * j:LANES * (j + 1)]) for j in range(n_qk // LANES)], axis=1) * q_scale
    if transposed:
        _store_transposed_tiles(q_ref, qr)
    else:
        q_ref[...] = qr.astype(BF16)
    pk = jnp.dot(xb, w_ref[:, n_qk:2 * n_qk], preferred_element_type=F32)
    kr = jnp.concatenate([rope(pk[:, LANES * j:LANES * (j + 1)]) for j in range(n_qk // LANES)], axis=1)
    kb_ref[...] = kr.astype(BF16)
    n_v = u_ref.shape[1]
    pv = jnp.dot(xb, w_ref[:, 2 * n_qk:2 * n_qk + n_v], preferred_element_type=F32)
    if transposed:
        kf_ref[...] = kr.T
        n_h = n_v // V_DIM
        for h in range(n_h):
            vf_ref[pl.ds(h, pv.shape[0], stride=n_h), :] = pv[:, V_DIM * h:V_DIM * (h + 1)]
        _store_transposed_tiles(vb_ref, pv)
    else:
        kf_ref[...] = kr
        vf_ref[...] = pv
        vb_ref[...] = pv.astype(BF16)
    u_ref[...] = jnp.dot(xb, w_ref[:, 2 * n_qk + n_v:], preferred_element_type=F32)


def _in_proj(x2d, w_bf, cos_t, sin_t, *, tm, t_attn=None):
    T, D = x2d.shape
    n_cols = w_bf.shape[1]
    n_qk = n_cols // 4
    n_tab = cos_t.shape[0] // tm
    row = lambda i: (i, 0)
    tab = lambda i: (i % n_tab, 0)
    out_sds = lambda dt: jax.ShapeDtypeStruct((T, n_qk), dt)
    spec = pl.BlockSpec((tm, n_qk), row)
    qv_spec, qv_sds = spec, out_sds(BF16)
    kf_spec, kf_sds, vf_spec, vf_sds = spec, out_sds(F32), spec, out_sds(F32)
    if t_attn is not None:
        H = n_qk // V_DIM
        qv_spec = pl.BlockSpec((tm // t_attn, H, V_DIM, t_attn), lambda i: (i, 0, 0, 0))
        qv_sds = jax.ShapeDtypeStruct((T // t_attn, H, V_DIM, t_attn), BF16)
        kf_spec = pl.BlockSpec((None, n_qk, tm), lambda i: (i // n_tab, 0, i % n_tab))
        kf_sds = jax.ShapeDtypeStruct((T // cos_t.shape[0], n_qk, cos_t.shape[0]), F32)
        vf_spec = pl.BlockSpec((tm * H, V_DIM), row)
        vf_sds = jax.ShapeDtypeStruct((T * H, V_DIM), F32)
    return pl.pallas_call(
        functools.partial(_inproj_kernel, q_scale=HEAD_DIM ** -0.5 * LOG2E, n_qk=n_qk,
                          transposed=t_attn is not None),
        grid=(T // tm,),
        in_specs=[pl.BlockSpec((tm, D), row),
                  pl.BlockSpec((D, n_cols), lambda i: (0, 0)),
                  pl.BlockSpec((tm, LANES), tab),
                  pl.BlockSpec((tm, LANES), tab)],
        out_specs=[qv_spec, kf_spec, spec, vf_spec, qv_spec, spec],
        out_shape=[qv_sds, kf_sds, out_sds(BF16), vf_sds, qv_sds, out_sds(F32)],
        compiler_params=_cparams("parallel"),
        name="in_proj_rope",
    )(x2d, w_bf, cos_t, sin_t)


def _split_maps(q):
    lane = lax.broadcasted_iota(jnp.int32, q.shape, 1)
    zero = jnp.zeros_like(q)
    return jnp.concatenate([jnp.where(lane < HEAD_DIM, q, zero), jnp.where(lane >= HEAD_DIM, q, zero)], axis=0)


def _qk(qs, kb):
    return lax.dot_general(qs, kb, (((1,), (1,)), ((), ())), preferred_element_type=F32)


def _subln(o, g, out_scale):
    ms = jnp.mean(o * o, axis=-1, keepdims=True)
    return o * lax.rsqrt(ms + RMS_EPS) * g * out_scale


def _attn_prompt_kernel(lam_ref, qt_ref, k_ref, vt_ref, g_ref, o_ref, qs_ref, acc_ref, s0_ref, s1_ref, m_ref,
                        *, tq, out_scale):
    i = pl.program_id(2)
    s_buf = (s0_ref, s1_ref)
    n_cb = 2 * tq // MXU_N
    qt = qt_ref[...]
    row = lax.broadcasted_iota(jnp.int32, qt.shape, 0)
    zero = jnp.zeros_like(qt)
    q_maps = (jnp.where(row < HEAD_DIM, qt, zero), jnp.where(row >= HEAD_DIM, qt, zero))
    for cb in range(n_cb):
        j = cb % (n_cb // 2)
        qs_ref[cb] = q_maps[cb // (n_cb // 2)][:, MXU_N * j:MXU_N * (j + 1)]
    acc_ref[...] = jnp.zeros(acc_ref.shape, F32)
    m_ref[...] = jnp.full(m_ref.shape, NEG_BIG, F32)
    ones_rows = jnp.ones((SUM_ROWS, tq), BF16)

    def scores(kt, slot):
        kb = k_ref[pl.ds(pl.multiple_of(kt * tq, tq), tq), :]
        for cb in range(n_cb):
            s_buf[slot][cb] = jnp.dot(kb, qs_ref[cb], preferred_element_type=F32)

    def softmax_pv(kt, slot, masked):
        lhs = jnp.concatenate([vt_ref[kt], ones_rows], axis=0)
        for cb in range(n_cb):
            st = s_buf[slot][cb]
            if masked:
                c = lax.broadcasted_iota(jnp.int32, st.shape, 0)
                r = (lax.broadcasted_iota(jnp.int32, st.shape, 1) + MXU_N * cb) % tq
                st = jnp.where((c // CHUNK) <= (r // CHUNK), st, NEG_BIG)
            m_prev = m_ref[cb]
            m_new = jnp.maximum(m_prev, jnp.max(st, axis=0, keepdims=True))
            alpha = jnp.exp2(m_prev - m_new)
            pt = jnp.exp2(st - m_new).astype(BF16)
            m_ref[cb] = m_new
            acc_ref[cb] = alpha * acc_ref[cb] + jnp.dot(lhs, pt, preferred_element_type=F32)

    def step(t, par):
        scores(t, par)
        softmax_pv(t - 1, 1 - par, False)

    scores(0, 0)

    def steps(t0, n):
        for d in range(1, n + 1):
            step(t0 + d, d % 2)

    def body(j, carry):
        steps(ATTN_UNROLL * j, ATTN_UNROLL)
        return carry

    lax.fori_loop(0, i // ATTN_UNROLL, body, 0)
    t_done = (i // ATTN_UNROLL) * ATTN_UNROLL
    rem_len = ATTN_UNROLL
    while rem_len > ATTN_TAIL:
        rem_len //= 2

        @pl.when((i - t_done) >= rem_len)
        def _(t_done=t_done, rem_len=rem_len):
            steps(t_done, rem_len)

        t_done = t_done + jnp.where((i - t_done) >= rem_len, rem_len, 0)
    for rem in range(ATTN_TAIL):
        @pl.when(i - t_done == rem)
        def _(rem=rem, t_done=t_done):
            steps(t_done, rem)
            softmax_pv(i, rem % 2, True)

    def normalized(cb):
        return acc_ref[cb, 0:V_DIM, :] * (1.0 / acc_ref[cb, V_DIM:V_DIM + 1, :])

    half = n_cb // 2
    ot = jnp.concatenate([normalized(j) - lam_ref[0] * normalized(j + half) for j in range(half)], axis=1)
    o_ref[...] = _subln(ot.T, g_ref[...], out_scale).astype(o_ref.dtype)


def _attn_prompt(lam, qt, k, vt, g, *, out_scale):
    B, L, W = k.shape
    n_t, H, _, t = qt.shape
    nq = L // t
    return pl.pallas_call(
        functools.partial(_attn_prompt_kernel, tq=t, out_scale=out_scale),
        grid=(B, H, nq),
        in_specs=[pl.BlockSpec(memory_space=pltpu.SMEM),
                  pl.BlockSpec((None, None, V_DIM, t), lambda b, h, i: (b * nq + i, h, 0, 0)),
                  pl.BlockSpec((None, L, V_DIM), lambda b, h, i: (b, 0, h)),
                  pl.BlockSpec((nq, None, V_DIM, t), lambda b, h, i: (b, h, 0, 0)),
                  pl.BlockSpec((1, V_DIM), lambda b, h, i: (0, 0))],
        out_specs=pl.BlockSpec((None, t, V_DIM), lambda b, h, i: (b, i, h)),
        out_shape=jax.ShapeDtypeStruct((B, L, W), BF16),
        scratch_shapes=[pltpu.VMEM((2 * t // MXU_N, V_DIM, MXU_N), BF16),
                        pltpu.VMEM((2 * t // MXU_N, V_DIM + SUM_ROWS, MXU_N), F32),
                        pltpu.VMEM((2 * t // MXU_N, t, MXU_N), F32), pltpu.VMEM((2 * t // MXU_N, t, MXU_N), F32),
                        pltpu.VMEM((2 * t // MXU_N, 1, MXU_N), F32)],
        compiler_params=_cparams("parallel", "parallel", "parallel"),
        name="diff_attn_prompt",
    )(lam, qt, k, vt, g)


def _attn_sample_kernel(lam_ref, q_ref, kct_ref, vc_ref, kn_ref, vn_ref, g_ref, o_ref, *, past_len, n_heads,
                        out_scale):
    for h in range(n_heads):
        hs = slice(V_DIM * h, V_DIM * (h + 1))
        vc = vc_ref[pl.ds(h, past_len, stride=n_heads), :]
        o_ref[:, hs] = _attn_sample_head(lam_ref[0], q_ref[:, hs], kct_ref[hs, :], vc, kn_ref[:, hs], vn_ref[:, hs],
                                         g_ref[...], past_len, out_scale).astype(o_ref.dtype)


def _attn_sample_head(lam, q, kct, vc, kn, vn, g, past_len, out_scale):
    tq = q.shape[0]
    qs = _split_maps(q)
    s_c = jnp.dot(qs, kct.astype(BF16), preferred_element_type=F32)
    s_n = _qk(qs, kn)
    r = lax.broadcasted_iota(jnp.int32, s_n.shape, 0)
    c = lax.broadcasted_iota(jnp.int32, s_n.shape, 1)
    r = jnp.where(r >= tq, r - tq, r)
    s_n = jnp.where(((past_len + c) // CHUNK) <= ((past_len + r) // CHUNK), s_n, NEG_BIG)
    m = jnp.maximum(jnp.max(s_c, axis=-1, keepdims=True), jnp.max(s_n, axis=-1, keepdims=True))
    p_c = jnp.exp2(s_c - m)
    p_n = jnp.exp2(s_n - m)
    inv_l = 1.0 / (jnp.sum(p_c, axis=-1, keepdims=True) + jnp.sum(p_n, axis=-1, keepdims=True))
    p_c = p_c * inv_l
    p_n = p_n * inv_l
    w_c = p_c[0:tq] - lam * p_c[tq:2 * tq]
    w_n = p_n[0:tq] - lam * p_n[tq:2 * tq]
    o = (jnp.dot(w_c.astype(BF16), vc.astype(BF16), preferred_element_type=F32)
         + jnp.dot(w_n.astype(BF16), vn, preferred_element_type=F32))
    return _subln(o, g, out_scale)


def _attn_sample(lam, q, cache_kt, cache_v, k_new, v_new, g, *, out_scale):
    B, S, W = q.shape
    P = cache_kt.shape[1]
    H = W // V_DIM
    blk_new = pl.BlockSpec((None, S, W), lambda b: (b, 0, 0))
    return pl.pallas_call(
        functools.partial(_attn_sample_kernel, past_len=P, n_heads=H, out_scale=out_scale),
        grid=(B,),
        in_specs=[pl.BlockSpec(memory_space=pltpu.SMEM), blk_new,
                  pl.BlockSpec((W, P), lambda b: (b, 0)),
                  pl.BlockSpec((P * H, V_DIM), lambda b: (b, 0)),
                  blk_new, blk_new, pl.BlockSpec((1, V_DIM), lambda b: (0, 0))],
        out_specs=blk_new,
        out_shape=jax.ShapeDtypeStruct((B, S, W), BF16),
        compiler_params=_cparams("parallel"),
        name="diff_attn_sample",
    )(lam, q, cache_kt, cache_v, k_new, v_new, g)


def _cmul(ar, ai, br, bi):
    return ar * br - ai * bi, ar * bi + ai * br


def _s5_kernel(*refs, n_steps, chained):
    if chained:
        (u_ref, perm_ref, permt_ref, wx_ref, cb_ref, d_ref, pre_ref, pim_ref, gre_ref, gim_ref,
         wglu_ref, ng_ref, y_ref, hre_ref, him_ref, xre, xim, cre, cim) = refs
    else:
        (u_ref, perm_ref, permt_ref, wx_ref, cb_ref, d_ref, pre_ref, pim_ref, gre_ref, gim_ref,
         wglu_ref, ng_ref, h0re_ref, h0im_ref, y_ref, hre_ref, him_ref, xre, xim) = refs
    R = n_steps
    n_blk = wx_ref.shape[0]
    blk_in = wx_ref.shape[1]
    blk_st = wx_ref.shape[2] // 2
    blk_shape = (SUBLANES, blk_st)
    if chained:
        @pl.when(pl.program_id(1) == 0)
        def _():
            cre[...] = jnp.zeros(cre.shape, F32)
            cim[...] = jnp.zeros(cim.shape, F32)

    u = u_ref[...]
    u_hi = u.astype(BF16)
    u_lo = (u - u_hi.astype(F32)).astype(BF16)
    perm = perm_ref[...]
    up = (jnp.dot(perm, u_hi, preferred_element_type=F32) + jnp.dot(perm, u_lo, preferred_element_type=F32))
    upb = up.astype(BF16)
    for j in range(n_blk):
        xj = jnp.dot(upb[:, blk_in * j:blk_in * (j + 1)], wx_ref[j], preferred_element_type=F32)
        xre[:, blk_st * j:blk_st * (j + 1)] = xj[:, :blk_st]
        xim[:, blk_st * j:blk_st * (j + 1)] = xj[:, blk_st:]

    sub = lax.broadcasted_iota(jnp.int32, blk_shape, 0)
    ys = []
    for j in range(n_blk):
        cs = slice(blk_st * j, blk_st * (j + 1))
        row = lambda ref, r: jnp.broadcast_to(ref[r:r + 1, cs], blk_shape)
        ar, ai = row(pre_ref, 0), row(pim_ref, 0)
        hr = hi = jnp.zeros(blk_shape, F32)
        for r in range(R):
            rows = slice(SUBLANES * r, SUBLANES * (r + 1))
            hr, hi = ar * hr - ai * hi + xre[rows, cs], ar * hi + ai * hr + xim[rows, cs]
            xre[rows, cs] = hr
            xim[rows, cs] = hi
        g1r, g1i = row(gre_ref, 0), row(gim_ref, 0)
        if chained:
            c_r = cre[:, cs]
            c_i = cim[:, cs]
            inj_r, inj_i = _cmul(g1r, g1i, c_r, c_i)
            xr = hr + jnp.where(sub == 0, inj_r, 0.0)
            xi = hi + jnp.where(sub == 0, inj_i, 0.0)
            for lvl, k in enumerate((1, 2, 4)):
                gr, gi = row(gre_ref, lvl), row(gim_ref, lvl)
                sr = jnp.where(sub >= k, pltpu.roll(xr, k, 0), 0.0)
                si = jnp.where(sub >= k, pltpu.roll(xi, k, 0), 0.0)
                tr, ti = _cmul(gr, gi, sr, si)
                xr = xr + tr
                xi = xi + ti
            hc_r = jnp.where(sub == 0, c_r, pltpu.roll(xr, 1, 0))
            hc_i = jnp.where(sub == 0, c_i, pltpu.roll(xi, 1, 0))
            last_r = jnp.broadcast_to(xr[SUBLANES - 1:SUBLANES, :], blk_shape)
            last_i = jnp.broadcast_to(xi[SUBLANES - 1:SUBLANES, :], blk_shape)
            cre[:, cs] = last_r
            cim[:, cs] = last_i
            hre_ref[:, cs] = last_r[0:1, :]
            him_ref[:, cs] = last_i[0:1, :]
        else:
            hc_r = h0re_ref[:, cs]
            hc_i = h0im_ref[:, cs]
            tr, ti = _cmul(g1r, g1i, hc_r, hc_i)
            hre_ref[:, cs] = hr + tr
            him_ref[:, cs] = hi + ti
        for r in range(R):
            rows = slice(SUBLANES * r, SUBLANES * (r + 1))
            tr, ti = _cmul(row(pre_ref, r), row(pim_ref, r), hc_r, hc_i)
            xre[rows, cs] = xre[rows, cs] + tr
            xim[rows, cs] = xim[rows, cs] + ti
        ys.append(jnp.dot(xre[:, cs].astype(BF16), cb_ref[j, 0:blk_st, :], preferred_element_type=F32)
                  + jnp.dot(xim[:, cs].astype(BF16), cb_ref[j, blk_st:, :], preferred_element_type=F32))
    y = jnp.concatenate(ys, axis=1) + d_ref[...] * up
    y = 0.5 * y * (1.0 + jnp.tanh(math.sqrt(2.0 / math.pi) * (y + 0.044715 * (y * y * y))))
    gl = jnp.dot(y.astype(BF16), wglu_ref[...], preferred_element_type=F32)
    half = gl.shape[1] // 2
    z = gl[:, :half] * (1.0 / (1.0 + jnp.exp(-gl[:, half:])))
    z = z * lax.rsqrt(jnp.mean(z * z, axis=-1, keepdims=True) + RMS_EPS) * ng_ref[...]
    y_ref[...] = jnp.dot(permt_ref[...], z.astype(BF16), preferred_element_type=F32).astype(y_ref.dtype)


def _s5_tables(a_re, a_im, log_dt, b_re, b_im, c_re, c_im, n_steps):
    G, N = a_re.shape
    C = b_re.shape[2]
    dt = jnp.exp(log_dt.astype(F32))[:, None]
    a_re = a_re.astype(F32)
    a_im = a_im.astype(F32)

    def power(k):
        mag = jnp.exp(k * a_re * dt)
        return mag * jnp.cos(k * a_im * dt), mag * jnp.sin(k * a_im * dt)

    ab_re, ab_im = power(1.0)
    nr, ni = ab_re - 1.0, ab_im
    den = a_re * a_re + a_im * a_im
    cf_re = (nr * a_re + ni * a_im) / den
    cf_im = (ni * a_re - nr * a_im) / den
    ks = jnp.arange(1, n_steps + 1, dtype=F32)[:, None, None]
    p_re, p_im = power(ks)
    g_re, g_im = zip(*[power(float(m * n_steps)) for m in (1, 2, 4)])
    pad = jnp.zeros((SUBLANES - 3, G * N), F32)
    g_re = jnp.concatenate([jnp.stack(g_re).reshape(3, G * N), pad])
    g_im = jnp.concatenate([jnp.stack(g_im).reshape(3, G * N), pad])
    wx_re = cf_re[..., None] * b_re - cf_im[..., None] * b_im
    wx_im = cf_re[..., None] * b_im + cf_im[..., None] * b_re
    gpb = LANES // C
    nb = G // gpb
    eye = jnp.eye(gpb, dtype=F32)

    def in_block(w):
        w = w.reshape(nb, gpb, N, C)
        return jnp.einsum('jgnc,gh->jgchn', w, eye).reshape(nb, gpb * C, gpb * N)

    def out_block(w):
        w = w.reshape(nb, gpb, C, N)
        return jnp.einsum('jgcn,gh->jgnhc', w, eye).reshape(nb, gpb * N, gpb * C)

    wx = jnp.concatenate([in_block(wx_re), in_block(wx_im)], axis=2).astype(BF16)
    cb = jnp.concatenate([out_block(c_re.astype(F32)), out_block(-c_im.astype(F32))], axis=1).astype(BF16)
    return (p_re.reshape(n_steps, G * N), p_im.reshape(n_steps, G * N), g_re, g_im, wx, cb)


def _perm_matrices(n_steps):
    tt = SUBLANES * n_steps
    dst = jnp.arange(tt)
    src = (dst % SUBLANES) * n_steps + dst // SUBLANES
    perm = (src[:, None] == jnp.arange(tt)[None, :]).astype(BF16)
    return perm, perm.T


def _s5_mixer(u, tables, d, w_glu_bf, norm_g, *, n_steps, h0=None):
    B, L, W = u.shape
    p_re, p_im, g_re, g_im, wx, cb = tables
    n_state = p_re.shape[1]
    tt = SUBLANES * n_steps
    perm, permt = _perm_matrices(n_steps)
    chained = h0 is None
    const2 = lambda *_: (0, 0)
    const3 = lambda *_: (0, 0, 0)
    common_specs = [pl.BlockSpec(perm.shape, const2), pl.BlockSpec(perm.shape, const2),
                    pl.BlockSpec(wx.shape, const3), pl.BlockSpec(cb.shape, const3),
                    pl.BlockSpec((1, W), const2),
                    pl.BlockSpec(p_re.shape, const2), pl.BlockSpec(p_im.shape, const2),
                    pl.BlockSpec(g_re.shape, const2), pl.BlockSpec(g_im.shape, const2),
                    pl.BlockSpec(w_glu_bf.shape, const2), pl.BlockSpec((1, W), const2)]
    common_args = (perm, permt, wx, cb, d, p_re, p_im, g_re, g_im, w_glu_bf, norm_g)
    state_scratch = [pltpu.VMEM((tt, n_state), F32), pltpu.VMEM((tt, n_state), F32)]
    kern = functools.partial(_s5_kernel, n_steps=n_steps, chained=chained)
    if chained:
        y, hre, him = pl.pallas_call(
            kern,
            grid=(B, L // tt),
            in_specs=[pl.BlockSpec((None, tt, W), lambda b, t: (b, t, 0))] + common_specs,
            out_specs=[pl.BlockSpec((None, tt, W), lambda b, t: (b, t, 0)),
                       pl.BlockSpec((None, 1, n_state), lambda b, t: (b, 0, 0)),
                       pl.BlockSpec((None, 1, n_state), lambda b, t: (b, 0, 0))],
            out_shape=[jax.ShapeDtypeStruct((B, L, W), BF16),
                       jax.ShapeDtypeStruct((B, 1, n_state), F32),
                       jax.ShapeDtypeStruct((B, 1, n_state), F32)],
            scratch_shapes=state_scratch + [pltpu.VMEM((SUBLANES, n_state), F32)] * 2,
            compiler_params=_cparams("parallel", "arbitrary"),
            name="s5_mixer_chained",
        )(u, *common_args)
        return y, hre.reshape(B, n_state), him.reshape(B, n_state)
    u2 = u.reshape(B * L, W)
    h0re, h0im = h0
    y, hre, him = pl.pallas_call(
        kern,
        grid=(B // SUBLANES,),
        in_specs=[pl.BlockSpec((tt, W), lambda i: (i, 0))] + common_specs
        + [pl.BlockSpec((SUBLANES, n_state), lambda i: (i, 0))] * 2,
        out_specs=[pl.BlockSpec((tt, W), lambda i: (i, 0)),
                   pl.BlockSpec((SUBLANES, n_state), lambda i: (i, 0)),
                   pl.BlockSpec((SUBLANES, n_state), lambda i: (i, 0))],
        out_shape=[jax.ShapeDtypeStruct((B * L, W), BF16),
                   jax.ShapeDtypeStruct((B, n_state), F32),
                   jax.ShapeDtypeStruct((B, n_state), F32)],
        scratch_shapes=state_scratch,
        compiler_params=_cparams("parallel"),
        name="s5_mixer_batched",
    )(u2, *common_args, h0re, h0im)
    return y.reshape(B, L, W), hre, him


def _layer_norm(z, g, b):
    mu = jnp.mean(z, axis=-1, keepdims=True)
    zc = z - mu
    var = jnp.mean(zc * zc, axis=-1, keepdims=True)
    return zc * lax.rsqrt(var + LN_EPS) * g + b


def _router_gates(x1, hi, rw_ref, rb_ref):
    lo = (x1 - hi.astype(F32)).astype(BF16)
    both = jnp.dot(hi, rw_ref[...], preferred_element_type=F32)
    lg = (both[:, :LANES] + both[:, LANES:]
          + jnp.dot(lo, rw_ref[:, :LANES], preferred_element_type=F32)) + rb_ref[...]
    lane = lax.broadcasted_iota(jnp.int32, lg.shape, 1)
    big = jnp.int32(LANES)
    is_grp = (lane >= N_EXPERTS) & (lane < N_EXPERTS + N_EXPERT_GROUPS)
    gl = jnp.where(is_grp, lg, NEG_BIG)
    gexp = jnp.where(is_grp, jnp.exp(gl - jnp.max(gl, axis=-1, keepdims=True)), 0.0)
    gprob = gexp / jnp.sum(gexp, axis=-1, keepdims=True)
    g_w = jnp.max(gprob, axis=-1, keepdims=True)
    g_idx = jnp.min(jnp.where(is_grp & (gprob == g_w), lane - N_EXPERTS, big), axis=-1, keepdims=True)
    valid = (lane < N_EXPERTS) & ((lane // EXPERTS_PER_GROUP) == g_idx)
    el = jnp.where(valid, lg, NEG_BIG)
    eexp = jnp.where(valid, jnp.exp(el - jnp.max(el, axis=-1, keepdims=True)), 0.0)
    eprob = eexp / jnp.sum(eexp, axis=-1, keepdims=True)
    w1 = jnp.max(jnp.where(valid, eprob, -1.0), axis=-1, keepdims=True)
    i1 = jnp.min(jnp.where(valid & (eprob == w1), lane, big), axis=-1, keepdims=True)
    rest = valid & (lane != i1)
    w2 = jnp.max(jnp.where(rest, eprob, -1.0), axis=-1, keepdims=True)
    i2 = jnp.min(jnp.where(rest & (eprob == w2), lane, big), axis=-1, keepdims=True)
    den = w1 + w2
    return (jnp.where(lane == i1, g_w * (w1 / den), 0.0)
            + jnp.where(lane == i2, g_w * (w2 / den), 0.0))


def _tail_kernel(att_ref, ssm_ref, x_ref, wa_ref, wb_ref, g1_ref, b1_ref, rw_ref, rb_ref,
                 wg_ref, wu_ref, wd_ref, g2_ref, b2_ref, o_ref, acc_ref, *, alpha):
    mix = (jnp.dot(att_ref[...], wa_ref[...], preferred_element_type=F32)
           + jnp.dot(ssm_ref[...], wb_ref[...], preferred_element_type=F32))
    x1 = _layer_norm(alpha * x_ref[...] + mix, g1_ref[...], b1_ref[...])
    xb = x1.astype(BF16)
    gates = _router_gates(x1, xb, rw_ref, rb_ref)
    for e in range(wg_ref.shape[0]):
        hg = jnp.dot(xb, wg_ref[e], preferred_element_type=F32)
        hu = jnp.dot(xb, wu_ref[e], preferred_element_type=F32)
        h = hg * (1.0 / (1.0 + jnp.exp(-hg))) * hu * gates[:, e:e + 1]
        contrib = jnp.dot(h.astype(BF16), wd_ref[e], preferred_element_type=F32)
        if e == 0:
            acc_ref[...] = contrib
        else:
            acc_ref[...] += contrib
    o_ref[...] = _layer_norm(alpha * x1 + acc_ref[...], g2_ref[...], b2_ref[...])


def _tail(att, ssm, x2d, wa, wb, g1, b1, rw, rb, wg, wu, wd, g2, b2, *, tm, alpha):
    T, D = x2d.shape
    Wh = att.shape[1]
    row = lambda i: (i, 0)
    const2 = lambda i: (0, 0)
    const3 = lambda i: (0, 0, 0)
    once = pl.Buffered(1)
    vec = pl.BlockSpec((1, D), const2)
    return pl.pallas_call(
        functools.partial(_tail_kernel, alpha=alpha),
        grid=(T // tm,),
        in_specs=[pl.BlockSpec((tm, Wh), row), pl.BlockSpec((tm, Wh), row), pl.BlockSpec((tm, D), row),
                  pl.BlockSpec(wa.shape, const2, pipeline_mode=once),
                  pl.BlockSpec(wb.shape, const2, pipeline_mode=once), vec, vec,
                  pl.BlockSpec(rw.shape, const2, pipeline_mode=once), pl.BlockSpec((1, LANES), const2),
                  pl.BlockSpec(wg.shape, const3, pipeline_mode=once),
                  pl.BlockSpec(wu.shape, const3, pipeline_mode=once),
                  pl.BlockSpec(wd.shape, const3, pipeline_mode=once), vec, vec],
        out_specs=pl.BlockSpec((tm, D), row),
        out_shape=jax.ShapeDtypeStruct((T, D), F32),
        scratch_shapes=[pltpu.VMEM((tm, D), F32)],
        compiler_params=_cparams("parallel"),
        name="out_proj_router_moe",
    )(att, ssm, x2d, wa, wb, g1, b1, rw, rb, wg, wu, wd, g2, b2)


def _rope_tables(pos):
    half = HEAD_DIM // 2
    inv = 1.0 / (ROPE_THETA ** (jnp.arange(half, dtype=F32) / half))
    ang = pos.astype(F32)[:, None] * inv[None, :]
    cos = jnp.cos(ang)
    sin = jnp.sin(ang)
    cos_t = jnp.tile(cos, (1, LANES // half))
    sin_t = jnp.tile(jnp.concatenate([-sin, sin], axis=1), (1, LANES // HEAD_DIM))
    return cos_t, sin_t


def _row_tile(n, pref):
    t = min(n, pref)
    while n % t:
        t //= 2
    return t


def kernel(x_prompt, x_sample, cache_k, cache_v, state_ssm_re, state_ssm_im, w_in, lam_q1, lam_k1, lam_q2, lam_k2, subln_g, ssm_a_re, ssm_a_im, ssm_log_dt, ssm_b_re, ssm_b_im, ssm_c_re, ssm_c_im, ssm_d, w_glu, ssm_norm_g, w_out, ln1_g, ln1_b, w_grp, b_grp, w_rt, b_rt, w_gate, w_up, w_down, ln2_g, ln2_b):
    depth = w_in.shape[0]
    assert depth == 1, "single-layer step"
    B, L, D = x_prompt.shape
    Bs, S, _ = x_sample.shape
    P = cache_k.shape[2]
    H = cache_k.shape[3]
    n_qk = H * 2 * HEAD_DIM
    G, N = ssm_a_re.shape[1], ssm_a_re.shape[2]
    alpha = (2.0 * depth) ** 0.25
    lam_init = 0.8 - 0.6 * math.exp(-0.3 * 0)
    out_scale = 1.0 - lam_init
    l = 0

    w_in_bf = w_in[l].astype(BF16)
    lam = (jnp.exp(jnp.sum(lam_q1[l].astype(F32) * lam_k1[l].astype(F32)))
           - jnp.exp(jnp.sum(lam_q2[l].astype(F32) * lam_k2[l].astype(F32))) + lam_init).reshape(1).astype(F32)
    g_sub = subln_g[l].astype(F32).reshape(1, V_DIM)
    w_glu_bf = w_glu[l].astype(BF16)
    ssm_w = D - H * V_DIM
    d_row = ssm_d[l].astype(F32).reshape(1, ssm_w)
    ng_row = ssm_norm_g[l].astype(F32).reshape(1, ssm_w)
    wa = w_out[l, :H * V_DIM].astype(BF16)
    wb = w_out[l, H * V_DIM:].astype(BF16)
    ln1g, ln1b = ln1_g[l].astype(F32).reshape(1, D), ln1_b[l].astype(F32).reshape(1, D)
    ln2g, ln2b = ln2_g[l].astype(F32).reshape(1, D), ln2_b[l].astype(F32).reshape(1, D)
    w_r = jnp.concatenate([w_rt[l].astype(F32), w_grp[l].astype(F32),
                           jnp.zeros((D, LANES - N_EXPERTS - N_EXPERT_GROUPS), F32)], axis=1)
    rhi = w_r.astype(BF16)
    rw = jnp.concatenate([rhi, (w_r - rhi.astype(F32)).astype(BF16)], axis=1)
    rb =jnp.concatenate([b_rt[l].astype(F32), b_grp[l].astype(F32),
                          jnp.zeros((LANES - N_EXPERTS - N_EXPERT_GROUPS,), F32)]).reshape(1, LANES)
    wg, wu, wd = w_gate[l].astype(BF16), w_up[l].astype(BF16), w_down[l].astype(BF16)

    def tail(att, ssm, x2d):
        return _tail(att, ssm, x2d, wa, wb, ln1g, ln1b, rw, rb, wg, wu, wd, ln2g, ln2b,
                     tm=_row_tile(x2d.shape[0], ROW_TILE), alpha=alpha)

    cos_p, sin_p = _rope_tables(jnp.arange(L))
    xp2 = x_prompt.reshape(B * L, D)
    qp, kpf, kpb, vpf, vpb, up = _in_proj(xp2, w_in_bf, cos_p, sin_p, tm=_row_tile(L, ROW_TILE),
                                          t_attn=_row_tile(L, ROW_TILE))
    r3 = lambda a: a.reshape(B, L, -1)
    att_p = _attn_prompt(lam, qp, r3(kpb), vpb, g_sub, out_scale=out_scale)
    n_steps_p = _row_tile(L, S5_TILE) // SUBLANES
    tabs_p = _s5_tables(ssm_a_re[l], ssm_a_im[l], ssm_log_dt[l], ssm_b_re[l].astype(F32), ssm_b_im[l].astype(F32),
                        ssm_c_re[l], ssm_c_im[l], n_steps_p)
    ssm_p, hre_p, him_p = _s5_mixer(r3(up), tabs_p, d_row, w_glu_bf, ng_row, n_steps=n_steps_p)
    y_prompt = tail(att_p.reshape(B * L, -1), ssm_p.reshape(B * L, -1), xp2).reshape(B, L, D)

    cos_s, sin_s = _rope_tables(P + jnp.arange(S))
    cos_s, sin_s = jnp.tile(cos_s, (Bs, 1)), jnp.tile(sin_s, (Bs, 1))
    xs2 = x_sample.reshape(Bs * S, D)
    qs, ksf, ksb, vsf, vsb, us = _in_proj(xs2, w_in_bf, cos_s, sin_s, tm=_row_tile(Bs * S, ROW_TILE))
    s3 = lambda a: a.reshape(Bs, S, -1)
    cache_kt = jnp.transpose(cache_k[l], (0, 2, 3, 4, 1)).reshape(Bs * n_qk, P)
    att_s = _attn_sample(lam, s3(qs), cache_kt, cache_v[l].reshape(Bs * P * H, V_DIM),
                         s3(ksb), s3(vsb), g_sub, out_scale=out_scale)
    tabs_s = _s5_tables(ssm_a_re[l], ssm_a_im[l], ssm_log_dt[l], ssm_b_re[l].astype(F32), ssm_b_im[l].astype(F32),
                        ssm_c_re[l], ssm_c_im[l], S)
    h0 = (state_ssm_re[l].astype(F32).reshape(Bs, G * N), state_ssm_im[l].astype(F32).reshape(Bs, G * N))
    ssm_s, hre_s, him_s = _s5_mixer(s3(us), tabs_s, d_row, w_glu_bf, ng_row, n_steps=S, h0=h0)
    y_sample = tail(att_s.reshape(Bs * S, -1), ssm_s.reshape(Bs * S, -1), xs2).reshape(Bs, S, D)

    return (y_prompt, y_sample,
            jnp.transpose(kpf.reshape(1, B, H, 2, HEAD_DIM, L), (0, 1, 5, 2, 3, 4)), vpf.reshape(1, B, L, H, V_DIM),
            hre_p.reshape(1, B, G, N), him_p.reshape(1, B, G, N),
            ksf.reshape(1, Bs, S, H, 2, HEAD_DIM), vsf.reshape(1, Bs, S, H, V_DIM),
            hre_s.reshape(1, Bs, G, N), him_s.reshape(1, Bs, G, N))
```
